```python
import math
import jax
import jax.numpy as jnp
from jax import lax
import numpy as np

D_MODEL = 1024
BATCH = 8
SEQ = 2048
DEPTH = 1
DEC_BATCH = 128
DEC_SEQ = 8
PAST_LEN = 2048
PAGE_SIZE = 128

N_HEADS = 4
HEAD_DIM = 64
V_DIM = 2 * HEAD_DIM
D_QK = N_HEADS * 2 * HEAD_DIM
D_ATTN = N_HEADS * V_DIM
Q_BLOCK = 128
D_POOL = 512
POOL_WINDOWS = (2, 4, 8, 16)
N_POOL_GROUPS = 4
POOL_GROUP = D_POOL // N_POOL_GROUPS
POOL_STATE = max(POOL_WINDOWS) - 1
N_EXPERT_GROUPS = 4
EXPERTS_PER_GROUP = 8
TOP_K_IN_GROUP = 2
D_EXPERT = 256
D_PLE = 256
D_IN = D_POOL + 2 * D_QK + D_ATTN + 2 * D_MODEL
LN_EPS = 1e-5
DEEPNORM_ALPHA = (2 * DEPTH) ** 0.25
DEEPNORM_BETA = (8 * DEPTH) ** -0.25

kernel_name = 'hybrid_pool_diffattn_hmoe_step'


def layer_norm(x, g, b):
    xf = x.astype(jnp.float32)
    mu = jnp.mean(xf, axis=-1, keepdims=True)
    var = jnp.mean(jnp.square(xf - mu), axis=-1, keepdims=True)
    y = (xf - mu) * lax.rsqrt(var + LN_EPS) * g.astype(jnp.float32) + b.astype(jnp.float32)
    return y.astype(x.dtype)


def split_projection(z):
    o1 = D_POOL
    o2 = o1 + D_QK
    o3 = o2 + D_QK
    o4 = o3 + D_ATTN
    o5 = o4 + D_MODEL
    u = z[..., :o1]
    q = z[..., o1:o2]
    k = z[..., o2:o3]
    v = z[..., o3:o4]
    gp = jax.nn.sigmoid(z[..., o4:o5])
    ga = jax.nn.sigmoid(z[..., o5:])
    return u, q, k, v, gp, ga


def pool_mixer(u_ext, pos, w_grp, s_pool):
    P = POOL_STATE
    T = u_ext.shape[1] - P
    cs = jnp.cumsum(u_ext.astype(jnp.float32), axis=1)
    cs = jnp.concatenate([jnp.zeros_like(cs[:, :1]), cs], axis=1)
    u_new = u_ext[:, P:].astype(jnp.float32)
    outs = []
    for gi, w in enumerate(POOL_WINDOWS):
        sl = slice(gi * POOL_GROUP, (gi + 1) * POOL_GROUP)
        win = cs[:, P + 1:P + 1 + T, sl] - cs[:, P + 1 - w:P + 1 - w + T, sl]
        cnt = jnp.minimum(pos + 1, w).astype(jnp.float32)[:, None]
        d = (win / cnt - u_new[..., sl]).astype(u_ext.dtype)
        outs.append(jnp.einsum('btc,ce->bte', d, w_grp[gi]))
    return jnp.concatenate(outs, axis=-1) * s_pool


def diff_attention(q, ks, vs, q_pos, k_poss, lam, slopes):
    scores = []
    for k, k_pos in zip(ks, k_poss):
        s = jnp.einsum('bqhcd,bkhcd->bchqk', q, k).astype(jnp.float32) * (HEAD_DIM ** -0.5)
        dist = q_pos[:, None] - k_pos[None, :]
        bias = -slopes[:, None, None] * dist.astype(jnp.float32)
        scores.append(jnp.where(dist >= 0, s + bias, -jnp.inf))
    probs = jax.nn.softmax(jnp.concatenate(scores, axis=-1), axis=-1)
    a = probs[:, 0] - lam * probs[:, 1]
    outs = []
    off = 0
    for v in vs:
        n = v.shape[1]
        outs.append(jnp.einsum('bhqk,bkhe->bqhe', a[..., off:off + n].astype(v.dtype), v))
        off += n
    out = outs[0]
    for o in outs[1:]:
        out = out + o
    return out


def prompt_attention(q, k, v, lam, slopes):
    B, T = q.shape[:2]
    nb = T // Q_BLOCK
    pos = jnp.arange(T, dtype=jnp.int32)
    qb = jnp.moveaxis(q.reshape(B, nb, Q_BLOCK, N_HEADS, 2, HEAD_DIM), 1, 0)
    pb = pos.reshape(nb, Q_BLOCK)
    ob = lax.map(lambda a: diff_attention(a[0], [k], [v], a[1], [pos], lam, slopes), (qb, pb))
    return jnp.moveaxis(ob, 0, 1).reshape(B, T, N_HEADS, V_DIM)


def sample_attention(q, k, v, k_pages, v_pages, page_table, lam, slopes):
    DB, T = q.shape[:2]
    n_past = page_table.shape[1] * PAGE_SIZE
    k_past = k_pages[page_table].reshape(DB, n_past, N_HEADS, 2, HEAD_DIM)
    v_past = v_pages[page_table].reshape(DB, n_past, N_HEADS, V_DIM)
    past_pos = jnp.arange(n_past, dtype=jnp.int32)
    new_pos = n_past + jnp.arange(T, dtype=jnp.int32)
    return diff_attention(q, [k_past, k], [v_past, v], new_pos, [past_pos, new_pos], lam, slopes)


def hier_moe(x, w_rg, b_rg, w_re, b_re, w_gate, w_up, w_down):
    shp = x.shape
    xt = x.reshape(-1, shp[-1])
    lg = jnp.einsum('nd,dg->ng', xt, w_rg).astype(jnp.float32) + b_rg.astype(jnp.float32)
    pg = jax.nn.softmax(lg, axis=-1)
    _, g_idx = lax.top_k(lg, 1)
    oh_g = jax.nn.one_hot(g_idx[:, 0], N_EXPERT_GROUPS, dtype=jnp.float32)
    g_w = jnp.sum(pg * oh_g, axis=-1)
    le = jnp.einsum('nd,dge->nge', xt, w_re).astype(jnp.float32) + b_re.astype(jnp.float32)
    le_sel = jnp.sum(le * oh_g[:, :, None], axis=1)
    e_val, e_idx = lax.top_k(le_sel, TOP_K_IN_GROUP)
    e_w = jax.nn.softmax(e_val, axis=-1)
    w_e = jnp.sum(jax.nn.one_hot(e_idx, EXPERTS_PER_GROUP, dtype=jnp.float32) * e_w[..., None], axis=1)
    comb = (g_w[:, None, None] * oh_g[:, :, None] * w_e[:, None, :]).astype(x.dtype)
    y = jnp.zeros_like(xt)
    for g in range(N_EXPERT_GROUPS):
        h = jax.nn.silu(jnp.einsum('nd,edf->nef', xt, w_gate[g])) * jnp.einsum('nd,edf->nef', xt, w_up[g])
        y = y + jnp.einsum('nef,efd->nd', h * comb[:, g, :, None], w_down[g])
    return y.reshape(shp)


def setup_inputs(seed: int = 0) -> dict:
    key = jax.random.key(seed)
    keys = jax.random.split(key, 64)
    ctr = [0]

    def nxt():
        ctr[0] += 1
        return keys[ctr[0] - 1]

    def nrm(shape, scale):
        return jax.random.normal(nxt(), shape, jnp.float32) * scale

    n_pages = PAST_LEN // PAGE_SIZE
    n_used = DEC_BATCH * n_pages
    n_phys = n_used + n_used // 4
    L = DEPTH
    G, E, F = N_EXPERT_GROUPS, EXPERTS_PER_GROUP, D_EXPERT
    x_prompt = nrm((BATCH, SEQ, D_MODEL), 1.0)
    x_sample = nrm((DEC_BATCH, DEC_SEQ, D_MODEL), 1.0)
    p_prompt = nrm((DEPTH, BATCH, SEQ, D_PLE), 1.0)
    p_sample = nrm((DEPTH, DEC_BATCH, DEC_SEQ, D_PLE), 1.0)
    cache_k = nrm((DEPTH, n_phys, PAGE_SIZE, N_HEADS, V_DIM), 1.0)
    cache_v = nrm((DEPTH, n_phys, PAGE_SIZE, N_HEADS, V_DIM), DEEPNORM_BETA)
    state_pool = nrm((DEPTH, DEC_BATCH, POOL_STATE, D_POOL), 1.0)
    page_table = jax.random.permutation(nxt(), n_phys)[:n_used].reshape(DEC_BATCH, n_pages).astype(jnp.int32)
    sd = D_MODEL ** -0.5
    w_in = jnp.concatenate([
        nrm((L, D_MODEL, D_POOL + 2 * D_QK), sd),
        nrm((L, D_MODEL, D_ATTN), sd * DEEPNORM_BETA),
        nrm((L, D_MODEL, 2 * D_MODEL), sd)], axis=-1)
    return {
        'x_prompt': x_prompt,
        'x_sample': x_sample,
        'p_prompt': p_prompt,
        'p_sample': p_sample,
        'cache_k': cache_k,
        'cache_v': cache_v,
        'state_pool': state_pool,
        'page_table': page_table,
        'w_in': w_in,
        'b_in': nrm((L, D_IN), 0.01),
        'lam_q1': nrm((L, HEAD_DIM), 0.1),
        'lam_k1': nrm((L, HEAD_DIM), 0.1),
        'lam_q2': nrm((L, HEAD_DIM), 0.1),
        'lam_k2': nrm((L, HEAD_DIM), 0.1),
        'g_sub': 1.0 + nrm((L, V_DIM), 0.1),
        'w_pool_grp': nrm((L, N_POOL_GROUPS, POOL_GROUP, POOL_GROUP), POOL_GROUP ** -0.5),
        's_pool': 1.0 + nrm((L, D_POOL), 0.1),
        'w_pool_br': nrm((L, D_POOL, D_MODEL), D_POOL ** -0.5),
        'w_attn_br': nrm((L, D_ATTN, D_MODEL), D_ATTN ** -0.5),
        'w_o': nrm((L, D_MODEL, D_MODEL), sd * DEEPNORM_BETA),
        'ln1_g': 1.0 + nrm((L, D_MODEL), 0.1),
        'ln1_b': nrm((L, D_MODEL), 0.01),
        'w_rg': nrm((L, D_MODEL, G), sd),
        'b_rg': nrm((L, G), 0.01),
        'w_re': nrm((L, D_MODEL, G, E), sd),
        'b_re': nrm((L, G, E), 0.01),
        'w_gate': nrm((L, G, E, D_MODEL, F), sd),
        'w_up': nrm((L, G, E, D_MODEL, F), sd),
        'w_down': nrm((L, G, E, F, D_MODEL), (F ** -0.5) * DEEPNORM_BETA),
        'ln2_g': 1.0 + nrm((L, D_MODEL), 0.1),
        'ln2_b': nrm((L, D_MODEL), 0.01),
        'w_ple_gate': nrm((L, D_MODEL, D_MODEL), sd),
        'b_ple_gate': nrm((L, D_MODEL), 0.01),
        'w_ple': nrm((L, D_PLE, D_MODEL), (D_PLE ** -0.5) * DEEPNORM_BETA),
        'ln3_g': 1.0 + nrm((L, D_MODEL), 0.1),
        'ln3_b': nrm((L, D_MODEL), 0.01),
    }


def reference(x_prompt, x_sample, p_prompt, p_sample, cache_k, cache_v, state_pool, page_table,
              w_in, b_in, lam_q1, lam_k1, lam_q2, lam_k2, g_sub, w_pool_grp, s_pool,
              w_pool_br, w_attn_br, w_o, ln1_g, ln1_b, w_rg, b_rg, w_re, b_re,
              w_gate, w_up, w_down, ln2_g, ln2_b, w_ple_gate, b_ple_gate, w_ple, ln3_g, ln3_b):
    f32 = jnp.float32
    slopes = 2.0 ** (-8.0 * jnp.arange(1, N_HEADS + 1, dtype=f32) / N_HEADS)
    n_past = page_table.shape[1] * PAGE_SIZE
    xp, xs = x_prompt, x_sample
    kp_l, vp_l, sp_l, ks_l, vs_l, ss_l = [], [], [], [], [], []
    for i in range(DEPTH):
        lam_init = 0.8 - 0.6 * math.exp(-0.3 * i)
        lam = (jnp.exp(jnp.sum(lam_q1[i].astype(f32) * lam_k1[i].astype(f32)))
               - jnp.exp(jnp.sum(lam_q2[i].astype(f32) * lam_k2[i].astype(f32))) + lam_init)

        def block(x, p_emb, prefix, pos, attend):
            B, T = x.shape[:2]
            z = jnp.einsum('btd,de->bte', x, w_in[i]) + b_in[i]
            u, q, k, v, gp, ga = split_projection(z)
            q = q.reshape(B, T, N_HEADS, 2, HEAD_DIM)
            k = k.reshape(B, T, N_HEADS, 2, HEAD_DIM)
            v = v.reshape(B, T, N_HEADS, V_DIM)
            u_ext = jnp.concatenate([prefix.astype(u.dtype), u], axis=1)
            y_pool = pool_mixer(u_ext, pos, w_pool_grp[i], s_pool[i])
            o = attend(q, k, v).astype(f32)
            o = (o * lax.rsqrt(jnp.mean(jnp.square(o), axis=-1, keepdims=True) + LN_EPS)
                 * g_sub[i].astype(f32) * (1.0 - lam_init))
            y_attn = o.astype(x.dtype).reshape(B, T, D_ATTN)
            merged = (gp * jnp.einsum('btc,cd->btd', y_pool, w_pool_br[i])
                      + ga * jnp.einsum('btc,cd->btd', y_attn, w_attn_br[i]))
            x1 = layer_norm(DEEPNORM_ALPHA * x + jnp.einsum('btd,de->bte', merged, w_o[i]), ln1_g[i], ln1_b[i])
            moe = hier_moe(x1, w_rg[i], b_rg[i], w_re[i], b_re[i], w_gate[i], w_up[i], w_down[i])
            x2 = layer_norm(DEEPNORM_ALPHA * x1 + moe, ln2_g[i], ln2_b[i])
            gate = jax.nn.sigmoid(jnp.einsum('btd,de->bte', x2, w_ple_gate[i]) + b_ple_gate[i])
            pe = jnp.einsum('btp,pd->btd', p_emb, w_ple[i])
            x3 = layer_norm(DEEPNORM_ALPHA * x2 + gate * pe, ln3_g[i], ln3_b[i])
            return x3, k.reshape(B, T, N_HEADS, V_DIM), v, u_ext[:, -POOL_STATE:]

        pos_p = jnp.arange(xp.shape[1], dtype=jnp.int32)
        prefix_p = jnp.zeros((xp.shape[0], POOL_STATE, D_POOL), xp.dtype)
        xp, kp, vp, sp = block(xp, p_prompt[i], prefix_p, pos_p,
                               lambda q, k, v: prompt_attention(q, k, v, lam, slopes))
        pos_s = n_past + jnp.arange(xs.shape[1], dtype=jnp.int32)
        xs, ks_, vs_, ss_ = block(xs, p_sample[i], state_pool[i], pos_s,
                                  lambda q, k, v: sample_attention(q, k, v, cache_k[i], cache_v[i],
                                                                   page_table, lam, slopes))
        kp_l.append(kp)
        vp_l.append(vp)
        sp_l.append(sp)
        ks_l.append(ks_)
        vs_l.append(vs_)
        ss_l.append(ss_)
    return (xp, xs, jnp.stack(kp_l), jnp.stack(vp_l), jnp.stack(sp_l),
            jnp.stack(ks_l), jnp.stack(vs_l), jnp.stack(ss_l))
```

```python
import functools
import math

import jax
import jax.numpy as jnp
from jax import lax
from jax.experimental import pallas as pl
from jax.experimental.pallas import tpu as pltpu

F32 = jnp.float32
BF16 = jnp.bfloat16

D_MODEL = 1024
N_HEADS = 4
HEAD_DIM = 64
V_DIM = 2 * HEAD_DIM
D_QK = N_HEADS * 2 * HEAD_DIM
D_ATTN = N_HEADS * V_DIM
D_POOL = 512
POOL_WINDOWS = (2, 4, 8, 16)
POOL_GROUP = D_POOL // len(POOL_WINDOWS)
POOL_STATE = max(POOL_WINDOWS) - 1
POOL_HIST = POOL_STATE + 1
N_EXPERT_GROUPS = 4
EXPERTS_PER_GROUP = 8
N_EXPERTS = N_EXPERT_GROUPS * EXPERTS_PER_GROUP
D_EXPERT = 256
D_PLE = 256
PAGE_SIZE = 128
LN_EPS = 1e-5
DEPTH = 1
DEEPNORM_ALPHA = (2 * DEPTH) ** 0.25
LAM_INIT = 0.8 - 0.6 * math.exp(-0.3 * 0)
D_QKVU = D_POOL + 2 * D_QK + D_ATTN
ROUTER_COLS = 128
V7X_VMEM_LIMIT = 56 * 1024 * 1024
NEG_INF = float("-inf")


def _sigmoid(x):
    return 1.0 / (1.0 + jnp.exp(-x))


def _layer_norm(x, g, b):
    mu = jnp.mean(x, axis=-1, keepdims=True)
    xc = x - mu
    var = jnp.mean(xc * xc, axis=-1, keepdims=True)
    return xc * lax.rsqrt(var + LN_EPS) * g + b


def _lam_value(lq1_ref, lk1_ref, lq2_ref, lk2_ref):
    a = jnp.sum(lq1_ref[...] * lk1_ref[...], axis=1, keepdims=True)
    b = jnp.sum(lq2_ref[...] * lk2_ref[...], axis=1, keepdims=True)
    return jnp.exp(a) - jnp.exp(b) + LAM_INIT


def _window_sums(e):
    outs = []
    for gi, w in enumerate(POOL_WINDOWS):
        s = e[:, gi * POOL_GROUP:(gi + 1) * POOL_GROUP]
        step = 1
        while step < w:
            s = s + pltpu.roll(s, step, axis=0)
            step *= 2
        outs.append(s)
    return outs


def _pool_branch(wins, u, inv_cnts, wg_ref, sp_ref):
    ds = []
    for gi in range(len(POOL_WINDOWS)):
        ds.append(wins[gi] * inv_cnts[gi] - u[:, gi * POOL_GROUP:(gi + 1) * POOL_GROUP])
    d = jnp.concatenate(ds, axis=1).astype(BF16)
    return jnp.dot(d, wg_ref[...], preferred_element_type=F32) * sp_ref[...]


def _proj_prompt_kernel(x_ref, w_ref, b_ref, wg_ref, sp_ref,
                        q_ref, kf_ref, vf_ref, kb_ref, vb_ref, yp_ref, tail_ref,
                        ext_ref, *, tm, tiles_per_seq):
    t_in_seq = pl.program_id(0) % tiles_per_seq
    z = jnp.dot(x_ref[...].astype(BF16), w_ref[...], preferred_element_type=F32) + b_ref[...]
    u = z[:, 0:D_POOL]
    q = z[:, D_POOL:D_POOL + D_QK]
    k = z[:, D_POOL + D_QK:D_POOL + 2 * D_QK]
    v = z[:, D_POOL + 2 * D_QK:D_QKVU]
    q_ref[...] = (q * (HEAD_DIM ** -0.5)).astype(BF16)
    kf_ref[...] = k
    vf_ref[...] = v
    kb_ref[...] = k.astype(BF16)
    vb_ref[...] = v.astype(BF16)

    @pl.when(t_in_seq == 0)
    def _():
        ext_ref[0:POOL_HIST, :] = jnp.zeros((POOL_HIST, D_POOL), F32)

    ext_ref[POOL_HIST:POOL_HIST + tm, :] = u
    wins = [s[POOL_HIST:, :] for s in _window_sums(ext_ref[...])]
    pos = t_in_seq * tm + lax.broadcasted_iota(jnp.int32, (tm, 1), 0)
    inv_cnts = [1.0 / jnp.minimum(pos + 1, w).astype(F32) for w in POOL_WINDOWS]
    yp_ref[...] = _pool_branch(wins, u, inv_cnts, wg_ref, sp_ref).astype(BF16)
    tail = u[tm - POOL_HIST:, :]
    ext_ref[0:POOL_HIST, :] = tail
    tail_ref[...] = tail


def _proj_sample_kernel(x_ref, w_ref, b_ref, wg_ref, sp_ref, st_ref,
                        q_ref, kf_ref, vf_ref, yp_ref, tail_ref,
                        ext_ref, *, n_seq, t_new, n_past):
    z = jnp.dot(x_ref[...].astype(BF16), w_ref[...], preferred_element_type=F32) + b_ref[...]
    u = z[:, 0:D_POOL]
    q_ref[...] = z[:, D_POOL:D_POOL + D_QK] * (HEAD_DIM ** -0.5)
    kf_ref[...] = z[:, D_POOL + D_QK:D_POOL + 2 * D_QK]
    vf_ref[...] = z[:, D_POOL + 2 * D_QK:D_QKVU]
    rows = POOL_HIST + t_new
    ext_ref[:, 0:POOL_HIST, :] = st_ref[...]
    ext_ref[:, POOL_HIST:rows, :] = u.reshape(n_seq, t_new, D_POOL)
    e3 = ext_ref[...]
    wins = [s.reshape(n_seq, rows, POOL_GROUP)[:, POOL_HIST:, :].reshape(n_seq * t_new, POOL_GROUP)
            for s in _window_sums(e3.reshape(n_seq * rows, D_POOL))]
    inv_cnts = [1.0 / float(min(n_past + 1, w)) for w in POOL_WINDOWS]
    yp_ref[...] = _pool_branch(wins, u, inv_cnts, wg_ref, sp_ref).astype(BF16)
    tail_ref[...] = e3[:, rows - POOL_HIST:, :]


def _attn_prompt_kernel(slopes_ref, lq1_ref, lk1_ref, lq2_ref, lk2_ref,
                        q_ref, k_ref, v_ref, boff_ref, bdiag_ref, g_ref, o_ref,
                        m_ref, l_ref, acc_ref, *, tq):
    h = pl.program_id(1)
    qi = pl.program_id(2)
    slope = slopes_ref[h]
    q = q_ref[...]
    col = lax.broadcasted_iota(jnp.int32, q.shape, 1)
    zero = jnp.zeros_like(q)
    q2 = jnp.concatenate([jnp.where(col < HEAD_DIM, q, zero), jnp.where(col >= HEAD_DIM, q, zero)], axis=0)
    m_ref[...] = jnp.full(m_ref.shape, NEG_INF, F32)
    l_ref[...] = jnp.zeros(l_ref.shape, F32)
    acc_ref[...] = jnp.zeros(acc_ref.shape, F32)

    def block(kb, bias_ref, shift):
        k0 = pl.multiple_of(kb * tq, tq)
        kblk = k_ref[pl.ds(k0, tq), :]
        vblk = v_ref[pl.ds(k0, tq), :]
        t = lax.dot_general(q2, kblk, (((1,), (1,)), ((), ())), preferred_element_type=F32) + bias_ref[...]
        m_prev = m_ref[...]
        m_new = jnp.maximum(m_prev, jnp.max(t, axis=1, keepdims=True) + shift)
        p = jnp.exp(t - (m_new - shift))
        alpha = jnp.exp(m_prev - m_new)
        l_ref[...] = alpha * l_ref[...] + jnp.sum(p, axis=1, keepdims=True)
        acc_ref[...] = alpha * acc_ref[...] + jnp.dot(p.astype(BF16), vblk, preferred_element_type=F32)
        m_ref[...] = m_new

    def body(kb, carry):
        block(kb, boff_ref, -slope * ((qi - kb) * tq).astype(F32))
        return carry

    lax.fori_loop(0, qi, body, 0)
    block(qi, bdiag_ref, jnp.float32(0.0))

    lam = _lam_value(lq1_ref, lk1_ref, lq2_ref, lk2_ref)
    on = acc_ref[...] / l_ref[...]
    o = on[0:tq, :] - lam * on[tq:, :]
    o = o * lax.rsqrt(jnp.mean(o * o, axis=1, keepdims=True) + LN_EPS) * g_ref[...] * (1.0 - LAM_INIT)
    o_ref[...] = o.astype(BF16)


def _attn_sample_kernel(pt_ref, lq1_ref, lk1_ref, lq2_ref, lk2_ref,
                        q_ref, kn_ref, vn_ref, qmask_ref, bpast_ref, bnew_ref, g_ref, *rest,
                        n_pages, t_new):
    k_pages = rest[:n_pages]
    v_pages = rest[n_pages:2 * n_pages]
    o_ref = rest[2 * n_pages]
    n_rows = 2 * N_HEADS * t_new
    qrows = (jnp.concatenate([q_ref[...]] * (2 * N_HEADS), axis=0) * qmask_ref[...]).astype(BF16)
    nt_dims = (((1,), (1,)), ((), ()))
    s_parts = [lax.dot_general(qrows, kp[...].astype(BF16), nt_dims, preferred_element_type=F32)
               for kp in k_pages]
    pad = jnp.zeros((PAGE_SIZE - t_new, D_QK), BF16)
    kn = jnp.concatenate([kn_ref[...].astype(BF16), pad], axis=0)
    vn = jnp.concatenate([vn_ref[...].astype(BF16), pad], axis=0)
    s_past = jnp.concatenate(s_parts, axis=1) + bpast_ref[...]
    s_new = lax.dot_general(qrows, kn, nt_dims, preferred_element_type=F32) + bnew_ref[...]
    s = jnp.concatenate([s_past, s_new], axis=1)
    m = jnp.max(s, axis=1, keepdims=True)
    p = jnp.exp(s - m)
    p = p / jnp.sum(p, axis=1, keepdims=True)
    lam = _lam_value(lq1_ref, lk1_ref, lq2_ref, lk2_ref)
    half = n_rows // 2
    a = (p[0:half, :] - lam * p[half:, :]).astype(BF16)
    o = jnp.dot(a[:, n_pages * PAGE_SIZE:], vn, preferred_element_type=F32)
    for pi, vp in enumerate(v_pages):
        o = o + jnp.dot(a[:, pi * PAGE_SIZE:(pi + 1) * PAGE_SIZE], vp[...].astype(BF16),
                        preferred_element_type=F32)
    outs = []
    for hh in range(N_HEADS):
        oh = o[hh * t_new:(hh + 1) * t_new, hh * V_DIM:(hh + 1) * V_DIM]
        oh = oh * lax.rsqrt(jnp.mean(oh * oh, axis=1, keepdims=True) + LN_EPS) * g_ref[...] * (1.0 - LAM_INIT)
        outs.append(oh)
    o_ref[...] = jnp.concatenate(outs, axis=1)


def _merge_kernel(x_ref, yp_ref, ya_ref, wgate_ref, bgate_ref, wpb_ref, wab_ref, wo_ref,
                  g1_ref, b1_ref, wr_hi_ref, wr_lo_ref, br_ref,
                  x1f_ref, x1b_ref, comb_ref):
    x = x_ref[...]
    zg = jnp.dot(x.astype(BF16), wgate_ref[...], preferred_element_type=F32) + bgate_ref[...]
    gp = _sigmoid(zg[:, 0:D_MODEL])
    ga = _sigmoid(zg[:, D_MODEL:])
    merged = (gp * jnp.dot(yp_ref[...].astype(BF16), wpb_ref[...], preferred_element_type=F32)
              + ga * jnp.dot(ya_ref[...].astype(BF16), wab_ref[...], preferred_element_type=F32))
    x1 = _layer_norm(DEEPNORM_ALPHA * x + jnp.dot(merged.astype(BF16), wo_ref[...], preferred_element_type=F32),
                     g1_ref[...], b1_ref[...])
    x1f_ref[...] = x1
    x1_hi = x1.astype(BF16)
    x1b_ref[...] = x1_hi
    x1_lo = (x1 - x1_hi.astype(F32)).astype(BF16)
    lg = (jnp.dot(x1_hi, wr_hi_ref[...], preferred_element_type=F32)
          + jnp.dot(x1_lo, wr_hi_ref[...], preferred_element_type=F32)
          + jnp.dot(x1_hi, wr_lo_ref[...], preferred_element_type=F32)) + br_ref[...]
    col = lax.broadcasted_iota(jnp.int32, lg.shape, 1)
    big = jnp.int32(4 * ROUTER_COLS)
    is_g = col < N_EXPERT_GROUPS
    lgg = jnp.where(is_g, lg, NEG_INF)
    gmax = jnp.max(lgg, axis=1, keepdims=True)
    g_w = 1.0 / jnp.sum(jnp.exp(lgg - gmax), axis=1, keepdims=True)
    g_idx = jnp.min(jnp.where(lgg == gmax, col, big), axis=1, keepdims=True)
    lo_col = N_EXPERT_GROUPS + g_idx * EXPERTS_PER_GROUP
    le = jnp.where((col >= lo_col) & (col < lo_col + EXPERTS_PER_GROUP), lg, NEG_INF)
    v1 = jnp.max(le, axis=1, keepdims=True)
    i1 = jnp.min(jnp.where(le == v1, col, big), axis=1, keepdims=True)
    le2 = jnp.where(col == i1, NEG_INF, le)
    v2 = jnp.max(le2, axis=1, keepdims=True)
    i2 = jnp.min(jnp.where(le2 == v2, col, big), axis=1, keepdims=True)
    e21 = jnp.exp(v2 - v1)
    w1 = 1.0 / (1.0 + e21)
    w2 = e21 / (1.0 + e21)
    comb_ref[...] = jnp.where(col == i1, g_w * w1, 0.0) + jnp.where(col == i2, g_w * w2, 0.0)


def _moe_kernel(x1b_ref, x1f_ref, comb_ref, p_ref, wgu_ref, wd_ref,
                g2_ref, b2_ref, wpg_ref, bpg_ref, wple_ref, g3_ref, b3_ref,
                y_ref, acc_ref):
    e = pl.program_id(1)

    @pl.when(e == 0)
    def _():
        acc_ref[...] = jnp.zeros(acc_ref.shape, F32)

    gu = jnp.dot(x1b_ref[...], wgu_ref[...], preferred_element_type=F32)
    gate = gu[:, 0:D_EXPERT]
    up = gu[:, D_EXPERT:]
    comb = comb_ref[...]
    col = lax.broadcasted_iota(jnp.int32, comb.shape, 1)
    cw = jnp.sum(jnp.where(col == e + N_EXPERT_GROUPS, comb, 0.0), axis=1, keepdims=True)
    hact = (gate * _sigmoid(gate) * up * cw).astype(BF16)
    acc_ref[...] += jnp.dot(hact, wd_ref[...], preferred_element_type=F32)

    @pl.when(e == N_EXPERTS - 1)
    def _():
        x2 = _layer_norm(DEEPNORM_ALPHA * x1f_ref[...] + acc_ref[...], g2_ref[...], b2_ref[...])
        gate_ple = _sigmoid(jnp.dot(x2.astype(BF16), wpg_ref[...], preferred_element_type=F32) + bpg_ref[...])
        pe = jnp.dot(p_ref[...].astype(BF16), wple_ref[...], preferred_element_type=F32)
        y_ref[...] = _layer_norm(DEEPNORM_ALPHA * x2 + gate_ple * pe, g3_ref[...], b3_ref[...])


def _full(shape):
    nd = len(shape)
    return pl.BlockSpec(shape, lambda *_: (0,) * nd)


def _cparams(sem):
    return pltpu.CompilerParams(dimension_semantics=sem, vmem_limit_bytes=V7X_VMEM_LIMIT)


def _proj_prompt(x2d, w_qkvu, b_qkvu, wg_bd, s_pool, n_seq, seq_len, tm=512):
    n = x2d.shape[0]
    tps = seq_len // tm
    row = lambda i: (i, 0)
    outs = pl.pallas_call(
        functools.partial(_proj_prompt_kernel, tm=tm, tiles_per_seq=tps),
        grid=(n // tm,),
        in_specs=[pl.BlockSpec((tm, D_MODEL), row), _full(w_qkvu.shape), _full(b_qkvu.shape),
                  _full(wg_bd.shape), _full(s_pool.shape)],
        out_specs=[pl.BlockSpec((tm, D_QK), row), pl.BlockSpec((tm, D_QK), row), pl.BlockSpec((tm, D_ATTN), row),
                   pl.BlockSpec((tm, D_QK), row), pl.BlockSpec((tm, D_ATTN), row), pl.BlockSpec((tm, D_POOL), row),
                   pl.BlockSpec((None, POOL_HIST, D_POOL), lambda i: (i // tps, 0, 0))],
        out_shape=[jax.ShapeDtypeStruct((n, D_QK), BF16), jax.ShapeDtypeStruct((n, D_QK), F32),
                   jax.ShapeDtypeStruct((n, D_ATTN), F32), jax.ShapeDtypeStruct((n, D_QK), BF16),
                   jax.ShapeDtypeStruct((n, D_ATTN), BF16), jax.ShapeDtypeStruct((n, D_POOL), BF16),
                   jax.ShapeDtypeStruct((n_seq, POOL_HIST, D_POOL), F32)],
        scratch_shapes=[pltpu.VMEM((POOL_HIST + tm, D_POOL), F32)],
        compiler_params=_cparams(("arbitrary",)),
        name="proj_pool_prompt",
    )(x2d, w_qkvu, b_qkvu, wg_bd, s_pool)
    return outs


def _proj_sample(x2d, w_qkvu, b_qkvu, wg_bd, s_pool, state_pad, n_seq, t_new, n_past):
    n = x2d.shape[0]
    rows = POOL_HIST + t_new
    return pl.pallas_call(
        functools.partial(_proj_sample_kernel, n_seq=n_seq, t_new=t_new, n_past=n_past),
        grid=(1,),
        in_specs=[_full(x2d.shape), _full(w_qkvu.shape), _full(b_qkvu.shape), _full(wg_bd.shape),
                  _full(s_pool.shape), _full(state_pad.shape)],
        out_specs=[_full((n, D_QK)), _full((n, D_QK)), _full((n, D_ATTN)), _full((n, D_POOL)),
                   _full((n_seq, POOL_HIST, D_POOL))],
        out_shape=[jax.ShapeDtypeStruct((n, D_QK), F32), jax.ShapeDtypeStruct((n, D_QK), F32),
                   jax.ShapeDtypeStruct((n, D_ATTN), F32), jax.ShapeDtypeStruct((n, D_POOL), BF16),
                   jax.ShapeDtypeStruct((n_seq, POOL_HIST, D_POOL), F32)],
        scratch_shapes=[pltpu.VMEM((n_seq, rows, D_POOL), F32)],
        compiler_params=_cparams(("arbitrary",)),
        name="proj_pool_sample",
    )(x2d, w_qkvu, b_qkvu, wg_bd, s_pool, state_pad)


def _alibi_slopes():
    return 2.0 ** (-8.0 * jnp.arange(1, N_HEADS + 1, dtype=F32) / N_HEADS)


def _attn_prompt(q, kb, vb, lam_vecs, g_sub, n_seq, seq_len, tq=512):
    n = q.shape[0]
    nq = seq_len // tq
    slopes = _alibi_slopes()
    rel = (jnp.arange(tq, dtype=jnp.int32)[:, None] - jnp.arange(tq, dtype=jnp.int32)[None, :])
    rel2 = jnp.concatenate([rel, rel], axis=0)
    b_off = -slopes[:, None, None] * rel2.astype(F32)[None]
    b_diag = jnp.where(rel2[None] >= 0, b_off, NEG_INF)
    vec = pl.BlockSpec((1, HEAD_DIM), lambda b, h, i: (0, 0))
    return pl.pallas_call(
        functools.partial(_attn_prompt_kernel, tq=tq),
        grid=(n_seq, N_HEADS, nq),
        in_specs=[pl.BlockSpec(memory_space=pltpu.SMEM), vec, vec, vec, vec,
                  pl.BlockSpec((tq, V_DIM), lambda b, h, i: (b * nq + i, h)),
                  pl.BlockSpec((seq_len, V_DIM), lambda b, h, i: (b, h)),
                  pl.BlockSpec((seq_len, V_DIM), lambda b, h, i: (b, h)),
                  pl.BlockSpec((None, 2 * tq, tq), lambda b, h, i: (h, 0, 0)),
                  pl.BlockSpec((None, 2 * tq, tq), lambda b, h, i: (h, 0, 0)),
                  pl.BlockSpec((1, V_DIM), lambda b, h, i: (0, 0))],
        out_specs=pl.BlockSpec((tq, V_DIM), lambda b, h, i: (b * nq + i, h)),
        out_shape=jax.ShapeDtypeStruct((n, D_ATTN), BF16),
        scratch_shapes=[pltpu.VMEM((2 * tq, 1), F32), pltpu.VMEM((2 * tq, 1), F32),
                        pltpu.VMEM((2 * tq, V_DIM), F32)],
        compiler_params=_cparams(("arbitrary", "arbitrary", "arbitrary")),
        name="attn_prompt",
    )(slopes, *lam_vecs, q, kb, vb, b_off, b_diag, g_sub)


def _attn_sample(q, k_new, v_new, cache_k, cache_v, page_table, lam_vecs, g_sub, t_new):
    n = q.shape[0]
    n_seq, n_pages = page_table.shape
    n_past = n_pages * PAGE_SIZE
    n_rows = 2 * N_HEADS * t_new
    slopes = _alibi_slopes()
    r = jnp.arange(n_rows, dtype=jnp.int32)
    r_map, r_head, r_q = r // (N_HEADS * t_new), (r // t_new) % N_HEADS, r % t_new
    c = jnp.arange(D_QK, dtype=jnp.int32)
    qmask = ((c[None, :] // V_DIM == r_head[:, None])
             & ((c[None, :] % V_DIM) // HEAD_DIM == r_map[:, None])).astype(F32)
    row_slope = slopes[r_head][:, None]
    kpos = jnp.arange(n_past, dtype=jnp.int32)[None, :]
    b_past = -row_slope * (n_past + r_q[:, None] - kpos).astype(F32)
    j = jnp.arange(PAGE_SIZE, dtype=jnp.int32)[None, :]
    dist_new = r_q[:, None] - j
    b_new = jnp.where((dist_new >= 0) & (j < t_new), -row_slope * dist_new.astype(F32), NEG_INF)
    ck = cache_k.reshape(cache_k.shape[0], PAGE_SIZE, D_QK)
    cv = cache_v.reshape(cache_v.shape[0], PAGE_SIZE, D_ATTN)
    pt = page_table.reshape(-1).astype(jnp.int32)
    tok = pl.BlockSpec((t_new, D_QK), lambda b, pt_ref: (b, 0))
    vec = pl.BlockSpec((1, HEAD_DIM), lambda b, pt_ref: (0, 0))

    def page_spec(pi):
        return pl.BlockSpec((None, PAGE_SIZE, D_QK), lambda b, pt_ref: (pt_ref[b * n_pages + pi], 0, 0))

    grid_spec = pltpu.PrefetchScalarGridSpec(
        num_scalar_prefetch=1,
        grid=(n_seq,),
        in_specs=[vec, vec, vec, vec, tok, tok, tok,
                  pl.BlockSpec(qmask.shape, lambda b, pt_ref: (0, 0)),
                  pl.BlockSpec(b_past.shape, lambda b, pt_ref: (0, 0)),
                  pl.BlockSpec(b_new.shape, lambda b, pt_ref: (0, 0)),
                  pl.BlockSpec((1, V_DIM), lambda b, pt_ref: (0, 0))]
                 + [page_spec(pi) for pi in range(n_pages)] * 2,
        out_specs=tok,
    )
    return pl.pallas_call(
        functools.partial(_attn_sample_kernel, n_pages=n_pages, t_new=t_new),
        grid_spec=grid_spec,
        out_shape=jax.ShapeDtypeStruct((n, D_ATTN), F32),
        compiler_params=_cparams(("arbitrary",)),
        name="attn_sample",
    )(pt, *lam_vecs, q, k_new, v_new, qmask, b_past, b_new, g_sub, *([ck] * n_pages), *([cv] * n_pages))


def _merge(x2d, y_pool, y_attn, wts, tm=512):
    n = x2d.shape[0]
    row = lambda i: (i, 0)
    consts = [wts[k] for k in ("w_gate", "b_gate", "w_pool_br", "w_attn_br", "w_o", "ln1_g", "ln1_b",
                               "wr_hi", "wr_lo", "b_r")]
    return pl.pallas_call(
        _merge_kernel,
        grid=(n // tm,),
        in_specs=[pl.BlockSpec((tm, D_MODEL), row), pl.BlockSpec((tm, D_POOL), row),
                  pl.BlockSpec((tm, D_ATTN), row)] + [_full(c.shape) for c in consts],
        out_specs=[pl.BlockSpec((tm, D_MODEL), row), pl.BlockSpec((tm, D_MODEL), row),
                   pl.BlockSpec((tm, ROUTER_COLS), row)],
        out_shape=[jax.ShapeDtypeStruct((n, D_MODEL), F32), jax.ShapeDtypeStruct((n, D_MODEL), BF16),
                   jax.ShapeDtypeStruct((n, ROUTER_COLS), F32)],
        compiler_params=_cparams(("arbitrary",)),
        name="merge_ln1_router",
    )(x2d, y_pool, y_attn, *consts)


def _moe(x1b, x1f, comb, p2d, wts, tm=1024):
    n = x1b.shape[0]
    row = lambda i, e: (i, 0)
    consts = [wts[k] for k in ("ln2_g", "ln2_b", "w_ple_gate", "b_ple_gate", "w_ple", "ln3_g", "ln3_b")]
    return pl.pallas_call(
        _moe_kernel,
        grid=(n // tm, N_EXPERTS),
        in_specs=[pl.BlockSpec((tm, D_MODEL), row), pl.BlockSpec((tm, D_MODEL), row),
                  pl.BlockSpec((tm, ROUTER_COLS), row), pl.BlockSpec((tm, D_PLE), row),
                  pl.BlockSpec((None, D_MODEL, 2 * D_EXPERT), lambda i, e: (e, 0, 0)),
                  pl.BlockSpec((None, D_EXPERT, D_MODEL), lambda i, e: (e, 0, 0))]
                 + [pl.BlockSpec(c.shape, lambda i, e: (0, 0)) for c in consts],
        out_specs=pl.BlockSpec((tm, D_MODEL), row),
        out_shape=jax.ShapeDtypeStruct((n, D_MODEL), F32),
        scratch_shapes=[pltpu.VMEM((tm, D_MODEL), F32)],
        compiler_params=_cparams(("arbitrary", "arbitrary")),
        name="moe_ln2_ple_ln3",
    )(x1b, x1f, comb, p2d, wts["w_gu"], wts["w_d"], *consts)


def _prepare_weights(w_in, b_in, w_pool_grp, s_pool, w_pool_br, w_attn_br, w_o, ln1_g, ln1_b,
                     w_rg, b_rg, w_re, b_re, w_gate, w_up, w_down, ln2_g, ln2_b,
                     w_ple_gate, b_ple_gate, w_ple, ln3_g, ln3_b):
    i = 0
    row = lambda a: a[i].reshape(1, -1).astype(F32)
    wg_bd = jnp.zeros((D_POOL, D_POOL), F32)
    for gi in range(len(POOL_WINDOWS)):
        sl = slice(gi * POOL_GROUP, (gi + 1) * POOL_GROUP)
        wg_bd = wg_bd.at[sl, sl].set(w_pool_grp[i, gi])
    w_r = jnp.zeros((D_MODEL, ROUTER_COLS), F32)
    w_r = w_r.at[:, 0:N_EXPERT_GROUPS].set(w_rg[i])
    w_r = w_r.at[:, N_EXPERT_GROUPS:N_EXPERT_GROUPS + N_EXPERTS].set(w_re[i].reshape(D_MODEL, N_EXPERTS))
    b_r = jnp.zeros((1, ROUTER_COLS), F32)
    b_r = b_r.at[0, 0:N_EXPERT_GROUPS].set(b_rg[i])
    b_r = b_r.at[0, N_EXPERT_GROUPS:N_EXPERT_GROUPS + N_EXPERTS].set(b_re[i].reshape(-1))
    wr_hi = w_r.astype(BF16)
    wr_lo = (w_r - wr_hi.astype(F32)).astype(BF16)
    w_gu = jnp.concatenate([w_gate[i], w_up[i]], axis=-1).reshape(N_EXPERTS, D_MODEL, 2 * D_EXPERT).astype(BF16)
    w_d = w_down[i].reshape(N_EXPERTS, D_EXPERT, D_MODEL).astype(BF16)
    return dict(
        w_qkvu=w_in[i, :, 0:D_QKVU].astype(BF16), b_qkvu=b_in[i, 0:D_QKVU].reshape(1, -1),
        w_gate=w_in[i, :, D_QKVU:].astype(BF16), b_gate=b_in[i, D_QKVU:].reshape(1, -1),
        wg_bd=wg_bd.astype(BF16), s_pool=row(s_pool),
        w_pool_br=w_pool_br[i].astype(BF16), w_attn_br=w_attn_br[i].astype(BF16), w_o=w_o[i].astype(BF16),
        ln1_g=row(ln1_g), ln1_b=row(ln1_b), wr_hi=wr_hi, wr_lo=wr_lo, b_r=b_r,
        w_gu=w_gu, w_d=w_d, ln2_g=row(ln2_g), ln2_b=row(ln2_b),
        w_ple_gate=w_ple_gate[i].astype(BF16), b_ple_gate=row(b_ple_gate), w_ple=w_ple[i].astype(BF16),
        ln3_g=row(ln3_g), ln3_b=row(ln3_b))


def kernel(x_prompt, x_sample, p_prompt, p_sample, cache_k, cache_v, state_pool, page_table, w_in, b_in, lam_q1, lam_k1, lam_q2, lam_k2, g_sub, w_pool_grp, s_pool, w_pool_br, w_attn_br, w_o, ln1_g, ln1_b, w_rg, b_rg, w_re, b_re, w_gate, w_up, w_down, ln2_g, ln2_b, w_ple_gate, b_ple_gate, w_ple, ln3_g, ln3_b):
    assert w_in.shape[0] == DEPTH == 1
    bsz, seq, _ = x_prompt.shape
    dbs, dseq, _ = x_sample.shape
    n_past = page_table.shape[1] * PAGE_SIZE
    wts = _prepare_weights(w_in, b_in, w_pool_grp, s_pool, w_pool_br, w_attn_br, w_o, ln1_g, ln1_b,
                           w_rg, b_rg, w_re, b_re, w_gate, w_up, w_down, ln2_g, ln2_b,
                           w_ple_gate, b_ple_gate, w_ple, ln3_g, ln3_b)
    lam_vecs = [a[0].reshape(1, HEAD_DIM).astype(F32) for a in (lam_q1, lam_k1, lam_q2, lam_k2)]
    g_row = g_sub[0].reshape(1, V_DIM).astype(F32)

    xp = x_prompt.reshape(bsz * seq, D_MODEL)
    q_p, kf_p, vf_p, kb_p, vb_p, yp_p, tail_p = _proj_prompt(
        xp, wts["w_qkvu"], wts["b_qkvu"], wts["wg_bd"], wts["s_pool"], bsz, seq)
    ya_p = _attn_prompt(q_p, kb_p, vb_p, lam_vecs, g_row, bsz, seq)
    x1f_p, x1b_p, comb_p = _merge(xp, yp_p, ya_p, wts)
    y_p = _moe(x1b_p, x1f_p, comb_p, p_prompt[0].reshape(bsz * seq, D_PLE), wts)

    xs = x_sample.reshape(dbs * dseq, D_MODEL)
    state_pad = jnp.pad(state_pool[0], ((0, 0), (POOL_HIST - POOL_STATE, 0), (0, 0)))
    q_s, kf_s, vf_s, yp_s, tail_s = _proj_sample(
        xs, wts["w_qkvu"], wts["b_qkvu"], wts["wg_bd"], wts["s_pool"], state_pad, dbs, dseq, n_past)
    ya_s = _attn_sample(q_s, kf_s, vf_s, cache_k[0], cache_v[0], page_table, lam_vecs, g_row, dseq)
    x1f_s, x1b_s, comb_s = _merge(xs, yp_s, ya_s, wts)
    y_s = _moe(x1b_s, x1f_s, comb_s, p_sample[0].reshape(dbs * dseq, D_PLE), wts)

    drop = POOL_HIST - POOL_STATE
    return (y_p.reshape(bsz, seq, D_MODEL),
            y_s.reshape(dbs, dseq, D_MODEL),
            kf_p.reshape(1, bsz, seq, N_HEADS, V_DIM),
            vf_p.reshape(1, bsz, seq, N_HEADS, V_DIM),
            tail_p[None, :, drop:, :],
            kf_s.reshape(1, dbs, dseq, N_HEADS, V_DIM),
            vf_s.reshape(1, dbs, dseq, N_HEADS, V_DIM),
            tail_s[None, :, drop:, :])
```

```python
import functools
import math

import jax
import jax.numpy as jnp
from jax import lax
from jax.experimental import pallas as pl
from jax.experimental.pallas import tpu as pltpu

F32 = jnp.float32
BF16 = jnp.bfloat16

D_MODEL = 1024
N_HEADS = 4
HEAD_DIM = 64
V_DIM = 2 * HEAD_DIM
D_QK = N_HEADS * 2 * HEAD_DIM
D_ATTN = N_HEADS * V_DIM
D_POOL = 512
POOL_WINDOWS = (2, 4, 8, 16)
POOL_GROUP = D_POOL // len(POOL_WINDOWS)
POOL_STATE = max(POOL_WINDOWS) - 1
POOL_HIST = POOL_STATE + 1
N_EXPERT_GROUPS = 4
EXPERTS_PER_GROUP = 8
N_EXPERTS = N_EXPERT_GROUPS * EXPERTS_PER_GROUP
D_EXPERT = 256
D_PLE = 256
PAGE_SIZE = 128
LN_EPS = 1e-5
DEPTH = 1
DEEPNORM_ALPHA = (2 * DEPTH) ** 0.25
LAM_INIT = 0.8 - 0.6 * math.exp(-0.3 * 0)
D_QKVU = D_POOL + 2 * D_QK + D_ATTN
ROUTER_COLS = 128
V7X_VMEM_LIMIT = 56 * 1024 * 1024
NEG_INF = float("-inf")


def _sigmoid(x):
    return 1.0 / (1.0 + jnp.exp(-x))


def _layer_norm(x, g, b):
    mu = jnp.mean(x, axis=-1, keepdims=True)
    xc = x - mu
    var = jnp.mean(xc * xc, axis=-1, keepdims=True)
    return xc * lax.rsqrt(var + LN_EPS) * g + b


def _lam_value(lq1_ref, lk1_ref, lq2_ref, lk2_ref):
    a = jnp.sum(lq1_ref[...] * lk1_ref[...], axis=1, keepdims=True)
    b = jnp.sum(lq2_ref[...] * lk2_ref[...], axis=1, keepdims=True)
    return jnp.exp(a) - jnp.exp(b) + LAM_INIT


def _window_sums(e):
    outs = []
    for gi, w in enumerate(POOL_WINDOWS):
        s = e[:, gi * POOL_GROUP:(gi + 1) * POOL_GROUP]
        step = 1
        while step < w:
            s = s + pltpu.roll(s, step, axis=0)
            step *= 2
        outs.append(s)
    return outs


def _pool_branch(wins, u, inv_cnts, wg_ref, sp_ref):
    ds = []
    for gi in range(len(POOL_WINDOWS)):
        ds.append(wins[gi] * inv_cnts[gi] - u[:, gi * POOL_GROUP:(gi + 1) * POOL_GROUP])
    d = jnp.concatenate(ds, axis=1).astype(BF16)
    return jnp.dot(d, wg_ref[...], preferred_element_type=F32) * sp_ref[...]


def _store_head_major(ref, x, n_rows):
    for hh in range(N_HEADS):
        ref[pl.ds(hh, n_rows, stride=N_HEADS), :] = x[:, hh * V_DIM:(hh + 1) * V_DIM]


def _proj_prompt_kernel(x_ref, w_ref, b_ref, wg_ref, sp_ref,
                        qt_ref, kf_ref, vf_ref, kb_ref, vt_ref, yp_ref, tail_ref,
                        ext_ref, *, tm, tiles_per_seq):
    t_in_seq = pl.program_id(0) % tiles_per_seq
    z = jnp.dot(x_ref[...].astype(BF16), w_ref[...], preferred_element_type=F32) + b_ref[...]
    u = z[:, 0:D_POOL]
    q = z[:, D_POOL:D_POOL + D_QK]
    k = z[:, D_POOL + D_QK:D_POOL + 2 * D_QK]
    v = z[:, D_POOL + 2 * D_QK:D_QKVU]
    qt_ref[...] = (q * (HEAD_DIM ** -0.5)).T.astype(BF16)
    vt_ref[...] = v.T.astype(BF16)
    kb_ref[...] = k.astype(BF16)
    _store_head_major(kf_ref, k, tm)
    _store_head_major(vf_ref, v, tm)

    @pl.when(t_in_seq == 0)
    def _():
        ext_ref[0:POOL_HIST, :] = jnp.zeros((POOL_HIST, D_POOL), F32)

    ext_ref[POOL_HIST:POOL_HIST + tm, :] = u
    wins = [s[POOL_HIST:, :] for s in _window_sums(ext_ref[...])]
    pos = t_in_seq * tm + lax.broadcasted_iota(jnp.int32, (tm, 1), 0)
    inv_cnts = [1.0 / jnp.minimum(pos + 1, w).astype(F32) for w in POOL_WINDOWS]
    yp_ref[...] = _pool_branch(wins, u, inv_cnts, wg_ref, sp_ref).astype(BF16)
    tail = u[tm - POOL_HIST:, :]
    ext_ref[0:POOL_HIST, :] = tail
    tail_ref[...] = tail


def _proj_sample_kernel(x_ref, w_ref, b_ref, wg_ref, sp_ref, st_ref,
                        q_ref, kf_ref, vf_ref, yp_ref, tail_ref,
                        ext_ref, *, n_seq, t_new, n_past):
    z = jnp.dot(x_ref[...].astype(BF16), w_ref[...], preferred_element_type=F32) + b_ref[...]
    u = z[:, 0:D_POOL]
    q_ref[...] = z[:, D_POOL:D_POOL + D_QK] * (HEAD_DIM ** -0.5)
    _store_head_major(kf_ref, z[:, D_POOL + D_QK:D_POOL + 2 * D_QK], n_seq * t_new)
    _store_head_major(vf_ref, z[:, D_POOL + 2 * D_QK:D_QKVU], n_seq * t_new)
    rows = POOL_HIST + t_new
    ext_ref[:, 0:POOL_HIST, :] = st_ref[...]
    ext_ref[:, POOL_HIST:rows, :] = u.reshape(n_seq, t_new, D_POOL)
    e3 = ext_ref[...]
    wins = [s.reshape(n_seq, rows, POOL_GROUP)[:, POOL_HIST:, :].reshape(n_seq * t_new, POOL_GROUP)
            for s in _window_sums(e3.reshape(n_seq * rows, D_POOL))]
    inv_cnts = [1.0 / float(min(n_past + 1, w)) for w in POOL_WINDOWS]
    yp_ref[...] = _pool_branch(wins, u, inv_cnts, wg_ref, sp_ref).astype(BF16)
    tail_ref[...] = e3[:, rows - POOL_HIST:, :]


def _attn_prompt_kernel(slopes_ref, lq1_ref, lk1_ref, lq2_ref, lk2_ref,
                        qt_ref, k_ref, vt_ref, boff_ref, bdiag_ref, g_ref, o_ref,
                        m_ref, l_ref, acc_ref, *, tq):
    h = pl.program_id(1)
    qi = pl.program_id(2)
    slope = slopes_ref[h]
    qt = qt_ref[...]
    row = lax.broadcasted_iota(jnp.int32, qt.shape, 0)
    zero = jnp.zeros_like(qt)
    q2t = jnp.concatenate([jnp.where(row < HEAD_DIM, qt, zero), jnp.where(row >= HEAD_DIM, qt, zero)], axis=1)
    m_ref[...] = jnp.full(m_ref.shape, NEG_INF, F32)
    l_ref[...] = jnp.zeros(l_ref.shape, F32)
    acc_ref[...] = jnp.zeros(acc_ref.shape, F32)

    def block(kb, bias_ref, shift):
        k0 = pl.multiple_of(kb * tq, tq)
        kblk = k_ref[pl.ds(k0, tq), :]
        vtblk = vt_ref[:, pl.ds(k0, tq)]
        t = jnp.dot(kblk, q2t, preferred_element_type=F32) + bias_ref[...]
        m_prev = m_ref[...]
        m_new = jnp.maximum(m_prev, jnp.max(t, axis=0, keepdims=True) + shift)
        p = jnp.exp(t - (m_new - shift))
        alpha = jnp.exp(m_prev - m_new)
        l_ref[...] = alpha * l_ref[...] + jnp.sum(p, axis=0, keepdims=True)
        acc_ref[...] = alpha * acc_ref[...] + jnp.dot(vtblk, p.astype(BF16), preferred_element_type=F32)
        m_ref[...] = m_new

    def body(kb, carry):
        block(kb, boff_ref, -slope * ((qi - kb) * tq).astype(F32))
        return carry

    lax.fori_loop(0, qi, body, 0)
    block(qi, bdiag_ref, jnp.float32(0.0))

    lam = _lam_value(lq1_ref, lk1_ref, lq2_ref, lk2_ref)
    on = acc_ref[...] / l_ref[...]
    o = (on[:, 0:tq] - lam * on[:, tq:]).T
    o = o * lax.rsqrt(jnp.mean(o * o, axis=1, keepdims=True) + LN_EPS) * g_ref[...] * (1.0 - LAM_INIT)
    o_ref[...] = o.astype(BF16)


def _attn_sample_kernel(pt_ref, lq1_ref, lk1_ref, lq2_ref, lk2_ref,
                        q_ref, kn_ref, vn_ref, bias_ref, g_ref, *rest, n_pages, t_new):
    k_pages = rest[:n_pages]
    v_pages = rest[n_pages:2 * n_pages]
    o_ref = rest[2 * n_pages]
    n_past = n_pages * PAGE_SIZE
    q = q_ref[...]
    lane = lax.broadcasted_iota(jnp.int32, (t_new, V_DIM), 1)
    pad = jnp.zeros((PAGE_SIZE - t_new, V_DIM), BF16)
    nt_dims = (((1,), (1,)), ((), ()))

    def head_rows(pages, new_ref, hh):
        past = jnp.concatenate([pg[pl.ds(hh, PAGE_SIZE, stride=N_HEADS), :] for pg in pages], axis=0)
        new = jnp.concatenate([new_ref[pl.ds(hh, t_new, stride=N_HEADS), :].astype(BF16), pad], axis=0)
        return past.astype(BF16), new

    s_rows = []
    for hh in range(N_HEADS):
        qh = q[:, hh * V_DIM:(hh + 1) * V_DIM]
        qh2 = jnp.concatenate([jnp.where(lane < HEAD_DIM, qh, 0.0), jnp.where(lane >= HEAD_DIM, qh, 0.0)],
                              axis=0).astype(BF16)
        k_past, k_new = head_rows(k_pages, kn_ref, hh)
        s_rows.append(jnp.concatenate(
            [lax.dot_general(qh2, k_past, nt_dims, preferred_element_type=F32),
             lax.dot_general(qh2, k_new, nt_dims, preferred_element_type=F32)], axis=1))
    s = jnp.concatenate(s_rows, axis=0) + bias_ref[...]
    m = jnp.max(s, axis=1, keepdims=True)
    p = jnp.exp(s - m)
    p = p / jnp.sum(p, axis=1, keepdims=True)
    lam = _lam_value(lq1_ref, lk1_ref, lq2_ref, lk2_ref)
    outs = []
    for hh in range(N_HEADS):
        r0 = hh * 2 * t_new
        a = (p[r0:r0 + t_new, :] - lam * p[r0 + t_new:r0 + 2 * t_new, :]).astype(BF16)
        v_past, v_new = head_rows(v_pages, vn_ref, hh)
        oh = (jnp.dot(a[:, 0:n_past], v_past, preferred_element_type=F32)
              + jnp.dot(a[:, n_past:], v_new, preferred_element_type=F32))
        oh = oh * lax.rsqrt(jnp.mean(oh * oh, axis=1, keepdims=True) + LN_EPS) * g_ref[...] * (1.0 - LAM_INIT)
        outs.append(oh)
    o_ref[...] = jnp.concatenate(outs, axis=1)


def _merge_kernel(x_ref, yp_ref, ya_ref, wgate_ref, bgate_ref, wpb_ref, wab_ref, wo_ref,
                  g1_ref, b1_ref, wr_hi_ref, wr_lo_ref, br_ref,
                  x1f_ref, x1b_ref, comb_ref):
    x = x_ref[...]
    zg = jnp.dot(x.astype(BF16), wgate_ref[...], preferred_element_type=F32) + bgate_ref[...]
    gp = _sigmoid(zg[:, 0:D_MODEL])
    ga = _sigmoid(zg[:, D_MODEL:])
    merged = (gp * jnp.dot(yp_ref[...].astype(BF16), wpb_ref[...], preferred_element_type=F32)
              + ga * jnp.dot(ya_ref[...].astype(BF16), wab_ref[...], preferred_element_type=F32))
    x1 = _layer_norm(DEEPNORM_ALPHA * x + jnp.dot(merged.astype(BF16), wo_ref[...], preferred_element_type=F32),
                     g1_ref[...], b1_ref[...])
    x1f_ref[...] = x1
    x1_hi = x1.astype(BF16)
    x1b_ref[...] = x1_hi
    x1_lo = (x1 - x1_hi.astype(F32)).astype(BF16)
    lg = (jnp.dot(x1_hi, wr_hi_ref[...], preferred_element_type=F32)
          + jnp.dot(x1_lo, wr_hi_ref[...], preferred_element_type=F32)
          + jnp.dot(x1_hi, wr_lo_ref[...], preferred_element_type=F32)) + br_ref[...]
    col = lax.broadcasted_iota(jnp.int32, lg.shape, 1)
    big = jnp.int32(4 * ROUTER_COLS)
    is_g = col < N_EXPERT_GROUPS
    lgg = jnp.where(is_g, lg, NEG_INF)
    gmax = jnp.max(lgg, axis=1, keepdims=True)
    g_w = 1.0 / jnp.sum(jnp.exp(lgg - gmax), axis=1, keepdims=True)
    g_idx = jnp.min(jnp.where(lgg == gmax, col, big), axis=1, keepdims=True)
    lo_col = N_EXPERT_GROUPS + g_idx * EXPERTS_PER_GROUP
    le = jnp.where((col >= lo_col) & (col < lo_col + EXPERTS_PER_GROUP), lg, NEG_INF)
    v1 = jnp.max(le, axis=1, keepdims=True)
    i1 = jnp.min(jnp.where(le == v1, col, big), axis=1, keepdims=True)
    le2 = jnp.where(col == i1, NEG_INF, le)
    v2 = jnp.max(le2, axis=1, keepdims=True)
    i2 = jnp.min(jnp.where(le2 == v2, col, big), axis=1, keepdims=True)
    e21 = jnp.exp(v2 - v1)
    w1 = 1.0 / (1.0 + e21)
    w2 = e21 / (1.0 + e21)
    comb_ref[...] = jnp.where(col == i1, g_w * w1, 0.0) + jnp.where(col == i2, g_w * w2, 0.0)


def _moe_kernel(x1b_ref, x1f_ref, comb_ref, p_ref, wgu_ref, wd_ref,
                g2_ref, b2_ref, wpg_ref, bpg_ref, wple_ref, g3_ref, b3_ref,
                y_ref, acc_ref):
    e = pl.program_id(1)

    @pl.when(e == 0)
    def _():
        acc_ref[...] = jnp.zeros(acc_ref.shape, F32)

    gu = jnp.dot(x1b_ref[...], wgu_ref[...], preferred_element_type=F32)
    gate = gu[:, 0:D_EXPERT]
    up = gu[:, D_EXPERT:]
    comb = comb_ref[...]
    col = lax.broadcasted_iota(jnp.int32, comb.shape, 1)
    cw = jnp.sum(jnp.where(col == e + N_EXPERT_GROUPS, comb, 0.0), axis=1, keepdims=True)
    hact = (gate * _sigmoid(gate) * up * cw).astype(BF16)
    acc_ref[...] += jnp.dot(hact, wd_ref[...], preferred_element_type=F32)

    @pl.when(e == N_EXPERTS - 1)
    def _():
        x2 = _layer_norm(DEEPNORM_ALPHA * x1f_ref[...] + acc_ref[...], g2_ref[...], b2_ref[...])
        gate_ple = _sigmoid(jnp.dot(x2.astype(BF16), wpg_ref[...], preferred_element_type=F32) + bpg_ref[...])
        pe = jnp.dot(p_ref[...].astype(BF16), wple_ref[...], preferred_element_type=F32)
        y_ref[...] = _layer_norm(DEEPNORM_ALPHA * x2 + gate_ple * pe, g3_ref[...], b3_ref[...])


def _full(shape):
    nd = len(shape)
    return pl.BlockSpec(shape, lambda *_: (0,) * nd)


def _cparams(sem):
    return pltpu.CompilerParams(dimension_semantics=sem, vmem_limit_bytes=V7X_VMEM_LIMIT)


def _proj_prompt(x2d, w_qkvu, b_qkvu, wg_bd, s_pool, n_seq, seq_len, tm=512):
    n = x2d.shape[0]
    tps = seq_len // tm
    row = lambda i: (i, 0)
    col = lambda i: (0, i)
    outs = pl.pallas_call(
        functools.partial(_proj_prompt_kernel, tm=tm, tiles_per_seq=tps),
        grid=(n // tm,),
        in_specs=[pl.BlockSpec((tm, D_MODEL), row), _full(w_qkvu.shape), _full(b_qkvu.shape),
                  _full(wg_bd.shape), _full(s_pool.shape)],
        out_specs=[pl.BlockSpec((D_QK, tm), col), pl.BlockSpec((tm * N_HEADS, V_DIM), row),
                   pl.BlockSpec((tm * N_HEADS, V_DIM), row), pl.BlockSpec((tm, D_QK), row),
                   pl.BlockSpec((D_ATTN, tm), col), pl.BlockSpec((tm, D_POOL), row),
                   pl.BlockSpec((None, POOL_HIST, D_POOL), lambda i: (i // tps, 0, 0))],
        out_shape=[jax.ShapeDtypeStruct((D_QK, n), BF16), jax.ShapeDtypeStruct((n * N_HEADS, V_DIM), F32),
                   jax.ShapeDtypeStruct((n * N_HEADS, V_DIM), F32), jax.ShapeDtypeStruct((n, D_QK), BF16),
                   jax.ShapeDtypeStruct((D_ATTN, n), BF16), jax.ShapeDtypeStruct((n, D_POOL), BF16),
                   jax.ShapeDtypeStruct((n_seq, POOL_HIST, D_POOL), F32)],
        scratch_shapes=[pltpu.VMEM((POOL_HIST + tm, D_POOL), F32)],
        compiler_params=_cparams(("arbitrary",)),
        name="proj_pool_prompt",
    )(x2d, w_qkvu, b_qkvu, wg_bd, s_pool)
    return outs


def _proj_sample(x2d, w_qkvu, b_qkvu, wg_bd, s_pool, state_pad, n_seq, t_new, n_past):
    n = x2d.shape[0]
    rows = POOL_HIST + t_new
    return pl.pallas_call(
        functools.partial(_proj_sample_kernel, n_seq=n_seq, t_new=t_new, n_past=n_past),
        grid=(1,),
        in_specs=[_full(x2d.shape), _full(w_qkvu.shape), _full(b_qkvu.shape), _full(wg_bd.shape),
                  _full(s_pool.shape), _full(state_pad.shape)],
        out_specs=[_full((n, D_QK)), _full((n * N_HEADS, V_DIM)), _full((n * N_HEADS, V_DIM)),
                   _full((n, D_POOL)), _full((n_seq, POOL_HIST, D_POOL))],
        out_shape=[jax.ShapeDtypeStruct((n, D_QK), F32), jax.ShapeDtypeStruct((n * N_HEADS, V_DIM), F32),
                   jax.ShapeDtypeStruct((n * N_HEADS, V_DIM), F32), jax.ShapeDtypeStruct((n, D_POOL), BF16),
                   jax.ShapeDtypeStruct((n_seq, POOL_HIST, D_POOL), F32)],
        scratch_shapes=[pltpu.VMEM((n_seq, rows, D_POOL), F32)],
        compiler_params=_cparams(("arbitrary",)),
        name="proj_pool_sample",
    )(x2d, w_qkvu, b_qkvu, wg_bd, s_pool, state_pad)


def _alibi_slopes():
    return 2.0 ** (-8.0 * jnp.arange(1, N_HEADS + 1, dtype=F32) / N_HEADS)


def _attn_prompt(qt, kb, vt, lam_vecs, g_sub, n_seq, seq_len, tq=512):
    n = kb.shape[0]
    nq = seq_len // tq
    slopes = _alibi_slopes()
    rel = (jnp.arange(tq, dtype=jnp.int32)[None, :] - jnp.arange(tq, dtype=jnp.int32)[:, None])
    rel2 = jnp.concatenate([rel, rel], axis=1)
    b_off = -slopes[:, None, None] * rel2.astype(F32)[None]
    b_diag = jnp.where(rel2[None] >= 0, b_off, NEG_INF)
    vec = pl.BlockSpec((1, HEAD_DIM), lambda b, h, i: (0, 0))
    return pl.pallas_call(
        functools.partial(_attn_prompt_kernel, tq=tq),
        grid=(n_seq, N_HEADS, nq),
        in_specs=[pl.BlockSpec(memory_space=pltpu.SMEM), vec, vec, vec, vec,
                  pl.BlockSpec((V_DIM, tq), lambda b, h, i: (h, b * nq + i)),
                  pl.BlockSpec((seq_len, V_DIM), lambda b, h, i: (b, h)),
                  pl.BlockSpec((V_DIM, seq_len), lambda b, h, i: (h, b)),
                  pl.BlockSpec((None, tq, 2 * tq), lambda b, h, i: (h, 0, 0)),
                  pl.BlockSpec((None, tq, 2 * tq), lambda b, h, i: (h, 0, 0)),
                  pl.BlockSpec((1, V_DIM), lambda b, h, i: (0, 0))],
        out_specs=pl.BlockSpec((tq, V_DIM), lambda b, h, i: (b * nq + i, h)),
        out_shape=jax.ShapeDtypeStruct((n, D_ATTN), BF16),
        scratch_shapes=[pltpu.VMEM((1, 2 * tq), F32), pltpu.VMEM((1, 2 * tq), F32),
                        pltpu.VMEM((V_DIM, 2 * tq), F32)],
        compiler_params=_cparams(("arbitrary", "arbitrary", "arbitrary")),
        name="attn_prompt",
    )(slopes, *lam_vecs, qt, kb, vt, b_off, b_diag, g_sub)


def _attn_sample(q, k_new, v_new, cache_k, cache_v, page_table, lam_vecs, g_sub, t_new):
    n = q.shape[0]
    n_seq, n_pages = page_table.shape
    n_past = n_pages * PAGE_SIZE
    n_rows = 2 * N_HEADS * t_new
    slopes = _alibi_slopes()
    r = jnp.arange(n_rows, dtype=jnp.int32)
    r_head, r_q = r // (2 * t_new), r % t_new
    row_slope = slopes[r_head][:, None]
    kpos = jnp.arange(n_past, dtype=jnp.int32)[None, :]
    b_past = -row_slope * (n_past + r_q[:, None] - kpos).astype(F32)
    j = jnp.arange(PAGE_SIZE, dtype=jnp.int32)[None, :]
    dist_new = r_q[:, None] - j
    b_new = jnp.where((dist_new >= 0) & (j < t_new), -row_slope * dist_new.astype(F32), NEG_INF)
    bias = jnp.concatenate([b_past, b_new], axis=1)
    page_rows = PAGE_SIZE * N_HEADS
    ck = cache_k.reshape(-1, V_DIM)
    cv = cache_v.reshape(-1, V_DIM)
    pt = page_table.reshape(-1).astype(jnp.int32)
    tok = pl.BlockSpec((t_new, D_QK), lambda b, pt_ref: (b, 0))
    tok_hm = pl.BlockSpec((t_new * N_HEADS, V_DIM), lambda b, pt_ref: (b, 0))
    vec = pl.BlockSpec((1, HEAD_DIM), lambda b, pt_ref: (0, 0))

    def page_spec(pi):
        return pl.BlockSpec((page_rows, V_DIM), lambda b, pt_ref: (pt_ref[b * n_pages + pi], 0))

    grid_spec = pltpu.PrefetchScalarGridSpec(
        num_scalar_prefetch=1,
        grid=(n_seq,),
        in_specs=[vec, vec, vec, vec, tok, tok_hm, tok_hm,
                  pl.BlockSpec(bias.shape, lambda b, pt_ref: (0, 0)),
                  pl.BlockSpec((1, V_DIM), lambda b, pt_ref: (0, 0))]
                 + [page_spec(pi) for pi in range(n_pages)] * 2,
        out_specs=tok,
    )
    return pl.pallas_call(
        functools.partial(_attn_sample_kernel, n_pages=n_pages, t_new=t_new),
        grid_spec=grid_spec,
        out_shape=jax.ShapeDtypeStruct((n, D_ATTN), F32),
        compiler_params=_cparams(("arbitrary",)),
        name="attn_sample",
    )(pt, *lam_vecs, q, k_new, v_new, bias, g_sub, *([ck] * n_pages), *([cv] * n_pages))


def _merge(x2d, y_pool, y_attn, wts, tm=512):
    n = x2d.shape[0]
    row = lambda i: (i, 0)
    consts = [wts[k] for k in ("w_gate", "b_gate", "w_pool_br", "w_attn_br", "w_o", "ln1_g", "ln1_b",
                               "wr_hi", "wr_lo", "b_r")]
    return pl.pallas_call(
        _merge_kernel,
        grid=(n // tm,),
        in_specs=[pl.BlockSpec((tm, D_MODEL), row), pl.BlockSpec((tm, D_POOL), row),
                  pl.BlockSpec((tm, D_ATTN), row)] + [_full(c.shape) for c in consts],
        out_specs=[pl.BlockSpec((tm, D_MODEL), row), pl.BlockSpec((tm, D_MODEL), row),
                   pl.BlockSpec((tm, ROUTER_COLS), row)],
        out_shape=[jax.ShapeDtypeStruct((n, D_MODEL), F32), jax.ShapeDtypeStruct((n, D_MODEL), BF16),
                   jax.ShapeDtypeStruct((n, ROUTER_COLS), F32)],
        compiler_params=_cparams(("arbitrary",)),
        name="merge_ln1_router",
    )(x2d, y_pool, y_attn, *consts)


def _moe(x1b, x1f, comb, p2d, wts, tm=1024):
    n = x1b.shape[0]
    row = lambda i, e: (i, 0)
    consts = [wts[k] for k in ("ln2_g", "ln2_b", "w_ple_gate", "b_ple_gate", "w_ple", "ln3_g", "ln3_b")]
    return pl.pallas_call(
        _moe_kernel,
        grid=(n // tm, N_EXPERTS),
        in_specs=[pl.BlockSpec((tm, D_MODEL), row), pl.BlockSpec((tm, D_MODEL), row),
                  pl.BlockSpec((tm, ROUTER_COLS), row), pl.BlockSpec((tm, D_PLE), row),
                  pl.BlockSpec((None, D_MODEL, 2 * D_EXPERT), lambda i, e: (e, 0, 0)),
                  pl.BlockSpec((None, D_EXPERT, D_MODEL), lambda i, e: (e, 0, 0))]
                 + [pl.BlockSpec(c.shape, lambda i, e: (0, 0)) for c in consts],
        out_specs=pl.BlockSpec((tm, D_MODEL), row),
        out_shape=jax.ShapeDtypeStruct((n, D_MODEL), F32),
        scratch_shapes=[pltpu.VMEM((tm, D_MODEL), F32)],
        compiler_params=_cparams(("arbitrary", "arbitrary")),
        name="moe_ln2_ple_ln3",
    )(x1b, x1f, comb, p2d, wts["w_gu"], wts["w_d"], *consts)


def _prepare_weights(w_in, b_in, w_pool_grp, s_pool, w_pool_br, w_attn_br, w_o, ln1_g, ln1_b,
                     w_rg, b_rg, w_re, b_re, w_gate, w_up, w_down, ln2_g, ln2_b,
                     w_ple_gate, b_ple_gate, w_ple, ln3_g, ln3_b):
    i = 0
    row = lambda a: a[i].reshape(1, -1).astype(F32)
    wg_bd = jnp.zeros((D_POOL, D_POOL), F32)
    for gi in range(len(POOL_WINDOWS)):
        sl = slice(gi * POOL_GROUP, (gi + 1) * POOL_GROUP)
        wg_bd = wg_bd.at[sl, sl].set(w_pool_grp[i, gi])
    w_r = jnp.zeros((D_MODEL, ROUTER_COLS), F32)
    w_r = w_r.at[:, 0:N_EXPERT_GROUPS].set(w_rg[i])
    w_r = w_r.at[:, N_EXPERT_GROUPS:N_EXPERT_GROUPS + N_EXPERTS].set(w_re[i].reshape(D_MODEL, N_EXPERTS))
    b_r = jnp.zeros((1, ROUTER_COLS), F32)
    b_r = b_r.at[0, 0:N_EXPERT_GROUPS].set(b_rg[i])
    b_r = b_r.at[0, N_EXPERT_GROUPS:N_EXPERT_GROUPS + N_EXPERTS].set(b_re[i].reshape(-1))
    wr_hi = w_r.astype(BF16)
    wr_lo = (w_r - wr_hi.astype(F32)).astype(BF16)
    w_gu = jnp.concatenate([w_gate[i], w_up[i]], axis=-1).reshape(N_EXPERTS, D_MODEL, 2 * D_EXPERT).astype(BF16)
    w_d = w_down[i].reshape(N_EXPERTS, D_EXPERT, D_MODEL).astype(BF16)
    return dict(
        w_qkvu=w_in[i, :, 0:D_QKVU].astype(BF16), b_qkvu=b_in[i, 0:D_QKVU].reshape(1, -1),
        w_gate=w_in[i, :, D_QKVU:].astype(BF16), b_gate=b_in[i, D_QKVU:].reshape(1, -1),
        wg_bd=wg_bd.astype(BF16), s_pool=row(s_pool),
        w_pool_br=w_pool_br[i].astype(BF16), w_attn_br=w_attn_br[i].astype(BF16), w_o=w_o[i].astype(BF16),
        ln1_g=row(ln1_g), ln1_b=row(ln1_b), wr_hi=wr_hi, wr_lo=wr_lo, b_r=b_r,
        w_gu=w_gu, w_d=w_d, ln2_g=row(ln2_g), ln2_b=row(ln2_b),
        w_ple_gate=w_ple_gate[i].astype(BF16), b_ple_gate=row(b_ple_gate), w_ple=w_ple[i].astype(BF16),
        ln3_g=row(ln3_g), ln3_b=row(ln3_b))


def kernel(x_prompt, x_sample, p_prompt, p_sample, cache_k, cache_v, state_pool, page_table, w_in, b_in, lam_q1, lam_k1, lam_q2, lam_k2, g_sub, w_pool_grp, s_pool, w_pool_br, w_attn_br, w_o, ln1_g, ln1_b, w_rg, b_rg, w_re, b_re, w_gate, w_up, w_down, ln2_g, ln2_b, w_ple_gate, b_ple_gate, w_ple, ln3_g, ln3_b):
    assert w_in.shape[0] == DEPTH == 1
    bsz, seq, _ = x_prompt.shape
    dbs, dseq, _ = x_sample.shape
    n_past = page_table.shape[1] * PAGE_SIZE
    wts = _prepare_weights(w_in, b_in, w_pool_grp, s_pool, w_pool_br, w_attn_br, w_o, ln1_g, ln1_b,
                           w_rg, b_rg, w_re, b_re, w_gate, w_up, w_down, ln2_g, ln2_b,
                           w_ple_gate, b_ple_gate, w_ple, ln3_g, ln3_b)
    lam_vecs = [a[0].reshape(1, HEAD_DIM).astype(F32) for a in (lam_q1, lam_k1, lam_q2, lam_k2)]
    g_row = g_sub[0].reshape(1, V_DIM).astype(F32)

    xp = x_prompt.reshape(bsz * seq, D_MODEL)
    qt_p, kf_p, vf_p, kb_p, vt_p, yp_p, tail_p = _proj_prompt(
        xp, wts["w_qkvu"], wts["b_qkvu"], wts["wg_bd"], wts["s_pool"], bsz, seq)
    ya_p = _attn_prompt(qt_p, kb_p, vt_p, lam_vecs, g_row, bsz, seq)
    x1f_p, x1b_p, comb_p = _merge(xp, yp_p, ya_p, wts)
    y_p = _moe(x1b_p, x1f_p, comb_p, p_prompt[0].reshape(bsz * seq, D_PLE), wts)

    xs = x_sample.reshape(dbs * dseq, D_MODEL)
    state_pad = jnp.pad(state_pool[0], ((0, 0), (POOL_HIST - POOL_STATE, 0), (0, 0)))
    q_s, kf_s, vf_s, yp_s, tail_s = _proj_sample(
        xs, wts["w_qkvu"], wts["b_qkvu"], wts["wg_bd"], wts["s_pool"], state_pad, dbs, dseq, n_past)
    ya_s = _attn_sample(q_s, kf_s, vf_s, cache_k, cache_v, page_table, lam_vecs, g_row, dseq)
    x1f_s, x1b_s, comb_s = _merge(xs, yp_s, ya_s, wts)
    y_s = _moe(x1b_s, x1f_s, comb_s, p_sample[0].reshape(dbs * dseq, D_PLE), wts)

    drop = POOL_HIST - POOL_STATE
    return (y_p.reshape(bsz, seq, D_MODEL),
            y_s.reshape(dbs, dseq, D_MODEL),
            kf_p.reshape(1, bsz, seq, N_HEADS, V_DIM),
            vf_p.reshape(1, bsz, seq, N_HEADS, V_DIM),
            tail_p[None, :, drop:, :],
            kf_s.reshape(1, dbs, dseq, N_HEADS, V_DIM),
            vf_s.reshape(1, dbs, dseq, N_HEADS, V_DIM),
            tail_s[None, :, drop:, :])
```

```python
import functools
import math

import jax
import jax.numpy as jnp
from jax import lax
from jax.experimental import pallas as pl
from jax.experimental.pallas import tpu as pltpu

F32 = jnp.float32
BF16 = jnp.bfloat16

D_MODEL = 1024
N_HEADS = 4
HEAD_DIM = 64
V_DIM = 2 * HEAD_DIM
D_QK = N_HEADS * 2 * HEAD_DIM
D_ATTN = N_HEADS * V_DIM
D_POOL = 512
POOL_WINDOWS = (2, 4, 8, 16)
POOL_GROUP = D_POOL // len(POOL_WINDOWS)
POOL_STATE = max(POOL_WINDOWS) - 1
POOL_HIST = POOL_STATE + 1
N_EXPERT_GROUPS = 4
EXPERTS_PER_GROUP = 8
N_EXPERTS = N_EXPERT_GROUPS * EXPERTS_PER_GROUP
D_EXPERT = 256
D_PLE = 256
PAGE_SIZE = 128
LN_EPS = 1e-5
DEPTH = 1
DEEPNORM_ALPHA = (2 * DEPTH) ** 0.25
LAM_INIT = 0.8 - 0.6 * math.exp(-0.3 * 0)
D_QKVU = D_POOL + 2 * D_QK + D_ATTN
ROUTER_COLS = 128
TOP_K_IN_GROUP = 2
MOE_TILE = 512
MOE_GRANULE = 16
MOE_REGION = -(-(TOP_K_IN_GROUP * MOE_TILE + N_EXPERTS * (MOE_GRANULE - 1)) // 128) * 128
REGION_GRANULES = MOE_REGION // MOE_GRANULE
MOE_SLOTS = 32
V7X_VMEM_LIMIT = 56 * 1024 * 1024
NEG_INF = float("-inf")


def _sigmoid(x):
    return 1.0 / (1.0 + jnp.exp(-x))


def _layer_norm(x, g, b):
    mu = jnp.mean(x, axis=-1, keepdims=True)
    xc = x - mu
    var = jnp.mean(xc * xc, axis=-1, keepdims=True)
    return xc * lax.rsqrt(var + LN_EPS) * g + b


def _lam_value(lq1_ref, lk1_ref, lq2_ref, lk2_ref):
    a = jnp.sum(lq1_ref[...] * lk1_ref[...], axis=1, keepdims=True)
    b = jnp.sum(lq2_ref[...] * lk2_ref[...], axis=1, keepdims=True)
    return jnp.exp(a) - jnp.exp(b) + LAM_INIT


def _window_sums(e):
    outs = []
    for gi, w in enumerate(POOL_WINDOWS):
        s = e[:, gi * POOL_GROUP:(gi + 1) * POOL_GROUP]
        step = 1
        while step < w:
            s = s + pltpu.roll(s, step, axis=0)
            step *= 2
        outs.append(s)
    return outs


def _pool_branch(wins, u, inv_cnts, wg_ref, sp_ref):
    ds = []
    for gi in range(len(POOL_WINDOWS)):
        ds.append(wins[gi] * inv_cnts[gi] - u[:, gi * POOL_GROUP:(gi + 1) * POOL_GROUP])
    d = jnp.concatenate(ds, axis=1).astype(BF16)
    return jnp.dot(d, wg_ref[...], preferred_element_type=F32) * sp_ref[...]


def _store_head_major(ref, x, n_rows):
    for hh in range(N_HEADS):
        ref[pl.ds(hh, n_rows, stride=N_HEADS), :] = x[:, hh * V_DIM:(hh + 1) * V_DIM]


def _proj_prompt_kernel(x_ref, w_ref, b_ref, wg_ref, sp_ref,
                        qt_ref, kf_ref, vf_ref, kb_ref, vt_ref, yp_ref, tail_ref,
                        ext_ref, *, tm, tiles_per_seq):
    t_in_seq = pl.program_id(0) % tiles_per_seq
    z = jnp.dot(x_ref[...].astype(BF16), w_ref[...], preferred_element_type=F32) + b_ref[...]
    u = z[:, 0:D_POOL]
    q = z[:, D_POOL:D_POOL + D_QK]
    k = z[:, D_POOL + D_QK:D_POOL + 2 * D_QK]
    v = z[:, D_POOL + 2 * D_QK:D_QKVU]
    qt_ref[...] = (q * (HEAD_DIM ** -0.5)).T.astype(BF16)
    vt_ref[...] = v.T.astype(BF16)
    kb_ref[...] = k.astype(BF16)
    _store_head_major(kf_ref, k, tm)
    _store_head_major(vf_ref, v, tm)

    @pl.when(t_in_seq == 0)
    def _():
        ext_ref[0:POOL_HIST, :] = jnp.zeros((POOL_HIST, D_POOL), F32)

    ext_ref[POOL_HIST:POOL_HIST + tm, :] = u
    wins = [s[POOL_HIST:, :] for s in _window_sums(ext_ref[...])]
    pos = t_in_seq * tm + lax.broadcasted_iota(jnp.int32, (tm, 1), 0)
    inv_cnts = [1.0 / jnp.minimum(pos + 1, w).astype(F32) for w in POOL_WINDOWS]
    yp_ref[...] = _pool_branch(wins, u, inv_cnts, wg_ref, sp_ref).astype(BF16)
    tail = u[tm - POOL_HIST:, :]
    ext_ref[0:POOL_HIST, :] = tail
    tail_ref[...] = tail


def _proj_sample_kernel(x_ref, w_ref, b_ref, wg_ref, sp_ref, st_ref,
                        q_ref, kf_ref, vf_ref, yp_ref, tail_ref,
                        ext_ref, *, n_seq, t_new, n_past):
    z = jnp.dot(x_ref[...].astype(BF16), w_ref[...], preferred_element_type=F32) + b_ref[...]
    u = z[:, 0:D_POOL]
    q_ref[...] = z[:, D_POOL:D_POOL + D_QK] * (HEAD_DIM ** -0.5)
    _store_head_major(kf_ref, z[:, D_POOL + D_QK:D_POOL + 2 * D_QK], n_seq * t_new)
    _store_head_major(vf_ref, z[:, D_POOL + 2 * D_QK:D_QKVU], n_seq * t_new)
    rows = POOL_HIST + t_new
    ext_ref[:, 0:POOL_HIST, :] = st_ref[...]
    ext_ref[:, POOL_HIST:rows, :] = u.reshape(n_seq, t_new, D_POOL)
    e3 = ext_ref[...]
    wins = [s.reshape(n_seq, rows, POOL_GROUP)[:, POOL_HIST:, :].reshape(n_seq * t_new, POOL_GROUP)
            for s in _window_sums(e3.reshape(n_seq * rows, D_POOL))]
    inv_cnts = [1.0 / float(min(n_past + 1, w)) for w in POOL_WINDOWS]
    yp_ref[...] = _pool_branch(wins, u, inv_cnts, wg_ref, sp_ref).astype(BF16)
    tail_ref[...] = e3[:, rows - POOL_HIST:, :]


def _attn_prompt_kernel(slopes_ref, lq1_ref, lk1_ref, lq2_ref, lk2_ref,
                        qt_ref, k_ref, vt_ref, boff_ref, bdiag_ref, g_ref, o_ref,
                        m_ref, l_ref, acc_ref, *, tq):
    h = pl.program_id(1)
    qi = pl.program_id(2)
    slope = slopes_ref[h]
    qt = qt_ref[...]
    row = lax.broadcasted_iota(jnp.int32, qt.shape, 0)
    zero = jnp.zeros_like(qt)
    q2t = jnp.concatenate([jnp.where(row < HEAD_DIM, qt, zero), jnp.where(row >= HEAD_DIM, qt, zero)], axis=1)
    m_ref[...] = jnp.full(m_ref.shape, NEG_INF, F32)
    l_ref[...] = jnp.zeros(l_ref.shape, F32)
    acc_ref[...] = jnp.zeros(acc_ref.shape, F32)

    def block(kb, bias_ref, shift):
        k0 = pl.multiple_of(kb * tq, tq)
        kblk = k_ref[pl.ds(k0, tq), :]
        vtblk = vt_ref[:, pl.ds(k0, tq)]
        t = jnp.dot(kblk, q2t, preferred_element_type=F32) + bias_ref[...]
        m_prev = m_ref[...]
        m_new = jnp.maximum(m_prev, jnp.max(t, axis=0, keepdims=True) + shift)
        p = jnp.exp(t - (m_new - shift))
        alpha = jnp.exp(m_prev - m_new)
        l_ref[...] = alpha * l_ref[...] + jnp.sum(p, axis=0, keepdims=True)
        acc_ref[...] = alpha * acc_ref[...] + jnp.dot(vtblk, p.astype(BF16), preferred_element_type=F32)
        m_ref[...] = m_new

    def body(kb, carry):
        block(kb, boff_ref, -slope * ((qi - kb) * tq).astype(F32))
        return carry

    lax.fori_loop(0, qi, body, 0)
    block(qi, bdiag_ref, jnp.float32(0.0))

    lam = _lam_value(lq1_ref, lk1_ref, lq2_ref, lk2_ref)
    on = acc_ref[...] / l_ref[...]
    o = (on[:, 0:tq] - lam * on[:, tq:]).T
    o = o * lax.rsqrt(jnp.mean(o * o, axis=1, keepdims=True) + LN_EPS) * g_ref[...] * (1.0 - LAM_INIT)
    o_ref[...] = o.astype(BF16)


def _attn_sample_kernel(pt_ref, lq1_ref, lk1_ref, lq2_ref, lk2_ref,
                        q_ref, kn_ref, vn_ref, bias_ref, g_ref, *rest, n_pages, t_new):
    k_pages = rest[:n_pages]
    v_pages = rest[n_pages:2 * n_pages]
    o_ref = rest[2 * n_pages]
    n_past = n_pages * PAGE_SIZE
    q = q_ref[...]
    lane = lax.broadcasted_iota(jnp.int32, (t_new, V_DIM), 1)
    pad = jnp.zeros((PAGE_SIZE - t_new, V_DIM), BF16)
    nt_dims = (((1,), (1,)), ((), ()))

    def head_rows(pages, new_ref, hh):
        past = jnp.concatenate([pg[pl.ds(hh, PAGE_SIZE, stride=N_HEADS), :] for pg in pages], axis=0)
        new = jnp.concatenate([new_ref[pl.ds(hh, t_new, stride=N_HEADS), :].astype(BF16), pad], axis=0)
        return past.astype(BF16), new

    s_rows = []
    for hh in range(N_HEADS):
        qh = q[:, hh * V_DIM:(hh + 1) * V_DIM]
        qh2 = jnp.concatenate([jnp.where(lane < HEAD_DIM, qh, 0.0), jnp.where(lane >= HEAD_DIM, qh, 0.0)],
                              axis=0).astype(BF16)
        k_past, k_new = head_rows(k_pages, kn_ref, hh)
        s_rows.append(jnp.concatenate(
            [lax.dot_general(qh2, k_past, nt_dims, preferred_element_type=F32),
             lax.dot_general(qh2, k_new, nt_dims, preferred_element_type=F32)], axis=1))
    s = jnp.concatenate(s_rows, axis=0) + bias_ref[...]
    m = jnp.max(s, axis=1, keepdims=True)
    p = jnp.exp(s - m)
    p = p / jnp.sum(p, axis=1, keepdims=True)
    lam = _lam_value(lq1_ref, lk1_ref, lq2_ref, lk2_ref)
    outs = []
    for hh in range(N_HEADS):
        r0 = hh * 2 * t_new
        a = (p[r0:r0 + t_new, :] - lam * p[r0 + t_new:r0 + 2 * t_new, :]).astype(BF16)
        v_past, v_new = head_rows(v_pages, vn_ref, hh)
        oh = (jnp.dot(a[:, 0:n_past], v_past, preferred_element_type=F32)
              + jnp.dot(a[:, n_past:], v_new, preferred_element_type=F32))
        oh = oh * lax.rsqrt(jnp.mean(oh * oh, axis=1, keepdims=True) + LN_EPS) * g_ref[...] * (1.0 - LAM_INIT)
        outs.append(oh)
    o_ref[...] = jnp.concatenate(outs, axis=1)


def _merge_kernel(x_ref, yp_ref, ya_ref, wgate_ref, bgate_ref, wpb_ref, wab_ref, wo_ref,
                  g1_ref, b1_ref, wrt_hi_ref, wrt_lo_ref, brt_ref, upper_ref, lower_ref, *rest, tm):
    x1f_ref, xs_ref, route_ref, cnt_ref = rest[-4:]
    x = x_ref[...]
    zg = jnp.dot(x.astype(BF16), wgate_ref[...], preferred_element_type=F32) + bgate_ref[...]
    gp = _sigmoid(zg[:, 0:D_MODEL])
    ga = _sigmoid(zg[:, D_MODEL:])
    merged = (gp * jnp.dot(yp_ref[...].astype(BF16), wpb_ref[...], preferred_element_type=F32)
              + ga * jnp.dot(ya_ref[...].astype(BF16), wab_ref[...], preferred_element_type=F32))
    x1 = _layer_norm(DEEPNORM_ALPHA * x + jnp.dot(merged.astype(BF16), wo_ref[...], preferred_element_type=F32),
                     g1_ref[...], b1_ref[...])
    x1f_ref[...] = x1
    x1_hi = x1.astype(BF16)
    x1_lo = (x1 - x1_hi.astype(F32)).astype(BF16)
    nt_dims = (((1,), (1,)), ((), ()))
    lg = (lax.dot_general(wrt_hi_ref[...], x1_hi, nt_dims, preferred_element_type=F32)
          + lax.dot_general(wrt_hi_ref[...], x1_lo, nt_dims, preferred_element_type=F32)
          + lax.dot_general(wrt_lo_ref[...], x1_hi, nt_dims, preferred_element_type=F32)) + brt_ref[...]
    row = lax.broadcasted_iota(jnp.int32, lg.shape, 0)
    big = jnp.int32(4 * ROUTER_COLS)
    lgg = jnp.where(row < N_EXPERT_GROUPS, lg, NEG_INF)
    gmax = jnp.max(lgg, axis=0, keepdims=True)
    g_w = 1.0 / jnp.sum(jnp.exp(lgg - gmax), axis=0, keepdims=True)
    g_idx = jnp.min(jnp.where(lgg == gmax, row, big), axis=0, keepdims=True)
    lo_row = N_EXPERT_GROUPS + g_idx * EXPERTS_PER_GROUP
    le = jnp.where((row >= lo_row) & (row < lo_row + EXPERTS_PER_GROUP), lg, NEG_INF)
    v1 = jnp.max(le, axis=0, keepdims=True)
    i1 = jnp.min(jnp.where(le == v1, row, big), axis=0, keepdims=True)
    le2 = jnp.where(row == i1, NEG_INF, le)
    v2 = jnp.max(le2, axis=0, keepdims=True)
    i2 = jnp.min(jnp.where(le2 == v2, row, big), axis=0, keepdims=True)
    e21 = jnp.exp(v2 - v1)
    c1 = g_w / (1.0 + e21)
    c2 = g_w * e21 / (1.0 + e21)
    sel1 = row == i1
    sel2 = row == i2
    member = jnp.where(sel1 | sel2, 1.0, 0.0)
    before = jnp.dot(member.astype(BF16), upper_ref[...], preferred_element_type=F32)
    cnt = jnp.sum(member, axis=1, keepdims=True)
    cnt_pad = jnp.ceil(cnt * (1.0 / MOE_GRANULE)) * MOE_GRANULE
    cnt_b = jnp.broadcast_to(cnt_pad, (ROUTER_COLS, ROUTER_COLS))
    seg_start = jnp.dot(lower_ref[...], cnt_b.astype(BF16), preferred_element_type=F32)
    cnt_ref[...] = cnt_b
    slot = seg_start[:, 0:1] + before
    pos1 = jnp.sum(jnp.where(sel1, slot, 0.0), axis=0, keepdims=True)
    pos2 = jnp.sum(jnp.where(sel2, slot, 0.0), axis=0, keepdims=True)
    srow = lax.broadcasted_iota(jnp.int32, (xs_ref.shape[0], tm), 0).astype(F32)
    perm = jnp.where((srow == pos1) | (srow == pos2), 1.0, 0.0).astype(BF16)
    xs_ref[...] = jnp.dot(perm, x1_hi, preferred_element_type=F32).astype(BF16)
    info = (jnp.where(row == 0, pos1, 0.0) + jnp.where(row == 1, pos2, 0.0)
            + jnp.where(row == 2, c1, 0.0) + jnp.where(row == 3, c2, 0.0))
    route_ref[...] = info.T


def _experts_kernel(gin_ref, sexp_ref, *rest, n_slots):
    x_refs = rest[:n_slots]
    wg_ref, wu_ref, wd_ref, y_ref, wg_b, wu_b, wd_b = rest[n_slots:]
    t = pl.program_id(0)
    expert = sexp_ref[t]
    is_expert = expert < N_EXPERTS

    @pl.when(is_expert & ((t == 0) | (expert != sexp_ref[jnp.maximum(t - 1, 0)])))
    def _():
        wg_b[...] = wg_ref[...].astype(BF16)
        wu_b[...] = wu_ref[...].astype(BF16)
        wd_b[...] = wd_ref[...].astype(BF16)

    @pl.when(is_expert)
    def _():
        xg = jnp.concatenate([r[...] for r in x_refs], axis=0)
        gate = jnp.dot(xg, wg_b[...], preferred_element_type=F32)
        up = jnp.dot(xg, wu_b[...], preferred_element_type=F32)
        hact = (gate * _sigmoid(gate) * up).astype(BF16)
        y_ref[...] = jnp.dot(hact, wd_b[...], preferred_element_type=F32).astype(BF16)

    @pl.when(jnp.logical_not(is_expert))
    def _():
        y_ref[...] = jnp.zeros(y_ref.shape, BF16)


def _final_kernel(loc_ref, x1f_ref, route_ref, *rest, n_granules):
    g_refs = rest[:n_granules]
    p_ref, g2_ref, b2_ref, wpg_ref, bpg_ref, wple_ref, g3_ref, b3_ref, y_ref = rest[n_granules:]
    route = route_ref[...]
    ys = jnp.concatenate([r[...] for r in g_refs], axis=0)
    scol = lax.broadcasted_iota(jnp.int32, (route.shape[0], ys.shape[0]), 1).astype(F32)
    comb = (jnp.where(scol == route[:, 0:1], route[:, 2:3], 0.0)
            + jnp.where(scol == route[:, 1:2], route[:, 3:4], 0.0)).astype(BF16)
    moe = jnp.dot(comb, ys, preferred_element_type=F32)
    x2 = _layer_norm(DEEPNORM_ALPHA * x1f_ref[...] + moe, g2_ref[...], b2_ref[...])
    gate_ple = _sigmoid(jnp.dot(x2.astype(BF16), wpg_ref[...], preferred_element_type=F32) + bpg_ref[...])
    pe = jnp.dot(p_ref[...].astype(BF16), wple_ref[...], preferred_element_type=F32)
    y_ref[...] = _layer_norm(DEEPNORM_ALPHA * x2 + gate_ple * pe, g3_ref[...], b3_ref[...])


def _full(shape):
    nd = len(shape)
    return pl.BlockSpec(shape, lambda *_: (0,) * nd)


def _cparams(sem):
    return pltpu.CompilerParams(dimension_semantics=sem, vmem_limit_bytes=V7X_VMEM_LIMIT)


def _proj_prompt(x2d, w_qkvu, b_qkvu, wg_bd, s_pool, n_seq, seq_len, tm=512):
    n = x2d.shape[0]
    tps = seq_len // tm
    row = lambda i: (i, 0)
    col = lambda i: (0, i)
    outs = pl.pallas_call(
        functools.partial(_proj_prompt_kernel, tm=tm, tiles_per_seq=tps),
        grid=(n // tm,),
        in_specs=[pl.BlockSpec((tm, D_MODEL), row), _full(w_qkvu.shape), _full(b_qkvu.shape),
                  _full(wg_bd.shape), _full(s_pool.shape)],
        out_specs=[pl.BlockSpec((D_QK, tm), col), pl.BlockSpec((tm * N_HEADS, V_DIM), row),
                   pl.BlockSpec((tm * N_HEADS, V_DIM), row), pl.BlockSpec((tm, D_QK), row),
                   pl.BlockSpec((D_ATTN, tm), col), pl.BlockSpec((tm, D_POOL), row),
                   pl.BlockSpec((None, POOL_HIST, D_POOL), lambda i: (i // tps, 0, 0))],
        out_shape=[jax.ShapeDtypeStruct((D_QK, n), BF16), jax.ShapeDtypeStruct((n * N_HEADS, V_DIM), F32),
                   jax.ShapeDtypeStruct((n * N_HEADS, V_DIM), F32), jax.ShapeDtypeStruct((n, D_QK), BF16),
                   jax.ShapeDtypeStruct((D_ATTN, n), BF16), jax.ShapeDtypeStruct((n, D_POOL), BF16),
                   jax.ShapeDtypeStruct((n_seq, POOL_HIST, D_POOL), F32)],
        scratch_shapes=[pltpu.VMEM((POOL_HIST + tm, D_POOL), F32)],
        compiler_params=_cparams(("arbitrary",)),
        name="proj_pool_prompt",
    )(x2d, w_qkvu, b_qkvu, wg_bd, s_pool)
    return outs


def _proj_sample(x2d, w_qkvu, b_qkvu, wg_bd, s_pool, state_pad, n_seq, t_new, n_past):
    n = x2d.shape[0]
    rows = POOL_HIST + t_new
    return pl.pallas_call(
        functools.partial(_proj_sample_kernel, n_seq=n_seq, t_new=t_new, n_past=n_past),
        grid=(1,),
        in_specs=[_full(x2d.shape), _full(w_qkvu.shape), _full(b_qkvu.shape), _full(wg_bd.shape),
                  _full(s_pool.shape), _full(state_pad.shape)],
        out_specs=[_full((n, D_QK)), _full((n * N_HEADS, V_DIM)), _full((n * N_HEADS, V_DIM)),
                   _full((n, D_POOL)), _full((n_seq, POOL_HIST, D_POOL))],
        out_shape=[jax.ShapeDtypeStruct((n, D_QK), F32), jax.ShapeDtypeStruct((n * N_HEADS, V_DIM), F32),
                   jax.ShapeDtypeStruct((n * N_HEADS, V_DIM), F32), jax.ShapeDtypeStruct((n, D_POOL), BF16),
                   jax.ShapeDtypeStruct((n_seq, POOL_HIST, D_POOL), F32)],
        scratch_shapes=[pltpu.VMEM((n_seq, rows, D_POOL), F32)],
        compiler_params=_cparams(("arbitrary",)),
        name="proj_pool_sample",
    )(x2d, w_qkvu, b_qkvu, wg_bd, s_pool, state_pad)


def _alibi_slopes():
    return 2.0 ** (-8.0 * jnp.arange(1, N_HEADS + 1, dtype=F32) / N_HEADS)


def _attn_prompt(qt, kb, vt, lam_vecs, g_sub, n_seq, seq_len, tq=512):
    n = kb.shape[0]
    nq = seq_len // tq
    slopes = _alibi_slopes()
    rel = (jnp.arange(tq, dtype=jnp.int32)[None, :] - jnp.arange(tq, dtype=jnp.int32)[:, None])
    rel2 = jnp.concatenate([rel, rel], axis=1)
    b_off = -slopes[:, None, None] * rel2.astype(F32)[None]
    b_diag = jnp.where(rel2[None] >= 0, b_off, NEG_INF)
    vec = pl.BlockSpec((1, HEAD_DIM), lambda b, h, i: (0, 0))
    return pl.pallas_call(
        functools.partial(_attn_prompt_kernel, tq=tq),
        grid=(n_seq, N_HEADS, nq),
        in_specs=[pl.BlockSpec(memory_space=pltpu.SMEM), vec, vec, vec, vec,
                  pl.BlockSpec((V_DIM, tq), lambda b, h, i: (h, b * nq + i)),
                  pl.BlockSpec((seq_len, V_DIM), lambda b, h, i: (b, h)),
                  pl.BlockSpec((V_DIM, seq_len), lambda b, h, i: (h, b)),
                  pl.BlockSpec((None, tq, 2 * tq), lambda b, h, i: (h, 0, 0)),
                  pl.BlockSpec((None, tq, 2 * tq), lambda b, h, i: (h, 0, 0)),
                  pl.BlockSpec((1, V_DIM), lambda b, h, i: (0, 0))],
        out_specs=pl.BlockSpec((tq, V_DIM), lambda b, h, i: (b * nq + i, h)),
        out_shape=jax.ShapeDtypeStruct((n, D_ATTN), BF16),
        scratch_shapes=[pltpu.VMEM((1, 2 * tq), F32), pltpu.VMEM((1, 2 * tq), F32),
                        pltpu.VMEM((V_DIM, 2 * tq), F32)],
        compiler_params=_cparams(("arbitrary", "arbitrary", "arbitrary")),
        name="attn_prompt",
    )(slopes, *lam_vecs, qt, kb, vt, b_off, b_diag, g_sub)


def _attn_sample(q, k_new, v_new, cache_k, cache_v, page_table, lam_vecs, g_sub, t_new):
    n = q.shape[0]
    n_seq, n_pages = page_table.shape
    n_past = n_pages * PAGE_SIZE
    n_rows = 2 * N_HEADS * t_new
    slopes = _alibi_slopes()
    r = jnp.arange(n_rows, dtype=jnp.int32)
    r_head, r_q = r // (2 * t_new), r % t_new
    row_slope = slopes[r_head][:, None]
    kpos = jnp.arange(n_past, dtype=jnp.int32)[None, :]
    b_past = -row_slope * (n_past + r_q[:, None] - kpos).astype(F32)
    j = jnp.arange(PAGE_SIZE, dtype=jnp.int32)[None, :]
    dist_new = r_q[:, None] - j
    b_new = jnp.where((dist_new >= 0) & (j < t_new), -row_slope * dist_new.astype(F32), NEG_INF)
    bias = jnp.concatenate([b_past, b_new], axis=1)
    page_rows = PAGE_SIZE * N_HEADS
    ck = cache_k.reshape(-1, V_DIM)
    cv = cache_v.reshape(-1, V_DIM)
    pt = page_table.reshape(-1).astype(jnp.int32)
    tok = pl.BlockSpec((t_new, D_QK), lambda b, pt_ref: (b, 0))
    tok_hm = pl.BlockSpec((t_new * N_HEADS, V_DIM), lambda b, pt_ref: (b, 0))
    vec = pl.BlockSpec((1, HEAD_DIM), lambda b, pt_ref: (0, 0))

    def page_spec(pi):
        return pl.BlockSpec((page_rows, V_DIM), lambda b, pt_ref: (pt_ref[b * n_pages + pi], 0))

    grid_spec = pltpu.PrefetchScalarGridSpec(
        num_scalar_prefetch=1,
        grid=(n_seq,),
        in_specs=[vec, vec, vec, vec, tok, tok_hm, tok_hm,
                  pl.BlockSpec(bias.shape, lambda b, pt_ref: (0, 0)),
                  pl.BlockSpec((1, V_DIM), lambda b, pt_ref: (0, 0))]
                 + [page_spec(pi) for pi in range(n_pages)] * 2,
        out_specs=tok,
    )
    return pl.pallas_call(
        functools.partial(_attn_sample_kernel, n_pages=n_pages, t_new=t_new),
        grid_spec=grid_spec,
        out_shape=jax.ShapeDtypeStruct((n, D_ATTN), F32),
        compiler_params=_cparams(("arbitrary",)),
        name="attn_sample",
    )(pt, *lam_vecs, q, k_new, v_new, bias, g_sub, *([ck] * n_pages), *([cv] * n_pages))


def _merge(x2d, y_pool, y_attn, wts, tile0, n_tiles_total, carried=None):
    tm = MOE_TILE
    n = x2d.shape[0]
    row = lambda i: (i, 0)
    out_row = lambda i: (i + tile0, 0)
    consts = [wts[k] for k in ("w_gate", "b_gate", "w_pool_br", "w_attn_br", "w_o", "ln1_g", "ln1_b",
                               "wrt_hi", "wrt_lo", "b_rt", "upper", "lower")]
    operands = [x2d, y_pool, y_attn, *consts]
    in_specs = [pl.BlockSpec((tm, D_MODEL), row), pl.BlockSpec((tm, D_POOL), row),
                pl.BlockSpec((tm, D_ATTN), row)] + [_full(c.shape) for c in consts]
    aliases = {}
    if carried is not None:
        aliases = {len(operands) + k: k for k in range(len(carried))}
        operands += list(carried)
        in_specs += [pl.BlockSpec(memory_space=pl.ANY)] * len(carried)
    return pl.pallas_call(
        functools.partial(_merge_kernel, tm=tm),
        grid=(n // tm,),
        in_specs=in_specs,
        out_specs=[pl.BlockSpec((tm, D_MODEL), out_row), pl.BlockSpec((MOE_REGION, D_MODEL), out_row),
                   pl.BlockSpec((tm, ROUTER_COLS), out_row), pl.BlockSpec((ROUTER_COLS, ROUTER_COLS), out_row)],
        out_shape=[jax.ShapeDtypeStruct((n_tiles_total * tm, D_MODEL), F32),
                   jax.ShapeDtypeStruct((n_tiles_total * MOE_REGION, D_MODEL), BF16),
                   jax.ShapeDtypeStruct((n_tiles_total * tm, ROUTER_COLS), F32),
                   jax.ShapeDtypeStruct((n_tiles_total * ROUTER_COLS, ROUTER_COLS), F32)],
        input_output_aliases=aliases,
        compiler_params=_cparams(("arbitrary",)),
        name="merge_ln1_route_dispatch",
    )(*operands)


def _granule_schedule(cnt_pad, n_steps):
    gc = cnt_pad // MOE_GRANULE
    seg = jnp.cumsum(gc, axis=1) - gc
    tot = jnp.sum(gc, axis=0)
    ahead = jnp.cumsum(gc, axis=0) - gc
    steps_e = (tot + MOE_SLOTS - 1) // MOE_SLOTS
    first_slot = (jnp.cumsum(steps_e) - steps_e) * MOE_SLOTS
    p = jnp.arange(n_steps * MOE_SLOTS, dtype=jnp.int32)
    e_of = jnp.sum((p[:, None] >= first_slot[None, :]).astype(jnp.int32), axis=1) - 1
    q = p - first_slot[e_of]
    valid = q < tot[e_of]
    ahead_e = ahead.T[e_of]
    tile_of = jnp.sum((q[:, None] >= ahead_e).astype(jnp.int32), axis=1) - 1
    gid = tile_of * REGION_GRANULES + seg[tile_of, e_of] + q - ahead[tile_of, e_of]
    gin = jnp.where(valid, gid, 0).astype(jnp.int32)
    step_expert = jnp.where(valid[::MOE_SLOTS], e_of[::MOE_SLOTS], N_EXPERTS).astype(jnp.int32)
    s = jnp.arange(REGION_GRANULES, dtype=jnp.int32)
    e_g = jnp.sum((s[None, :, None] >= seg[:, None, :]).astype(jnp.int32), axis=2) - 1
    t_g = s[None, :] - jnp.take_along_axis(seg, e_g, axis=1)
    used = t_g < jnp.take_along_axis(gc, e_g, axis=1)
    loc = first_slot[e_g] + jnp.take_along_axis(ahead, e_g, axis=1) + t_g
    loc = jnp.where(used, loc, loc[:, 0:1]).astype(jnp.int32)
    return gin, step_expert, loc.reshape(-1)


def _experts(xs, gin, step_expert, w_gate, w_up, w_down, n_steps):
    rows = MOE_SLOTS * MOE_GRANULE

    def slot_spec(k):
        return pl.BlockSpec((MOE_GRANULE, D_MODEL), lambda t, gin_ref, se_ref: (gin_ref[t * MOE_SLOTS + k], 0))

    w_sel = lambda t, gin_ref, se_ref: (jnp.minimum(se_ref[t], N_EXPERTS - 1), 0, 0)
    grid_spec = pltpu.PrefetchScalarGridSpec(
        num_scalar_prefetch=2,
        grid=(n_steps,),
        in_specs=[slot_spec(k) for k in range(MOE_SLOTS)]
                 + [pl.BlockSpec((None, D_MODEL, D_EXPERT), w_sel), pl.BlockSpec((None, D_MODEL, D_EXPERT), w_sel),
                    pl.BlockSpec((None, D_EXPERT, D_MODEL), w_sel)],
        out_specs=pl.BlockSpec((rows, D_MODEL), lambda t, gin_ref, se_ref: (t, 0)),
        scratch_shapes=[pltpu.VMEM((D_MODEL, D_EXPERT), BF16), pltpu.VMEM((D_MODEL, D_EXPERT), BF16),
                        pltpu.VMEM((D_EXPERT, D_MODEL), BF16)],
    )
    return pl.pallas_call(
        functools.partial(_experts_kernel, n_slots=MOE_SLOTS),
        grid_spec=grid_spec,
        out_shape=jax.ShapeDtypeStruct((n_steps * rows, D_MODEL), BF16),
        compiler_params=_cparams(("arbitrary",)),
        name="expert_mlps",
    )(gin, step_expert, *([xs] * MOE_SLOTS), w_gate, w_up, w_down)


def _final(x1f, route, ys, loc, p2d, wts, tile0):
    tm = MOE_TILE
    n = p2d.shape[0]
    consts = [wts[k] for k in ("ln2_g", "ln2_b", "w_ple_gate", "b_ple_gate", "w_ple", "ln3_g", "ln3_b")]

    def granule_spec(s):
        return pl.BlockSpec((MOE_GRANULE, D_MODEL),
                            lambda i, loc_ref: (loc_ref[(i + tile0) * REGION_GRANULES + s], 0))

    grid_spec = pltpu.PrefetchScalarGridSpec(
        num_scalar_prefetch=1,
        grid=(n // tm,),
        in_specs=[pl.BlockSpec((tm, D_MODEL), lambda i, loc_ref: (i + tile0, 0)),
                  pl.BlockSpec((tm, ROUTER_COLS), lambda i, loc_ref: (i + tile0, 0))]
                 + [granule_spec(s) for s in range(REGION_GRANULES)]
                 + [pl.BlockSpec((tm, D_PLE), lambda i, loc_ref: (i, 0))]
                 + [pl.BlockSpec(c.shape, lambda i, loc_ref: (0, 0)) for c in consts],
        out_specs=pl.BlockSpec((tm, D_MODEL), lambda i, loc_ref: (i, 0)),
    )
    return pl.pallas_call(
        functools.partial(_final_kernel, n_granules=REGION_GRANULES),
        grid_spec=grid_spec,
        out_shape=jax.ShapeDtypeStruct((n, D_MODEL), F32),
        compiler_params=_cparams(("arbitrary",)),
        name="combine_ln2_ple_ln3",
    )(loc, x1f, route, *([ys] * REGION_GRANULES), p2d, *consts)


def _prepare_weights(w_in, b_in, w_pool_grp, s_pool, w_pool_br, w_attn_br, w_o, ln1_g, ln1_b,
                     w_rg, b_rg, w_re, b_re, w_gate, w_up, w_down, ln2_g, ln2_b,
                     w_ple_gate, b_ple_gate, w_ple, ln3_g, ln3_b):
    i = 0
    row = lambda a: a[i].reshape(1, -1).astype(F32)
    wg_bd = jnp.zeros((D_POOL, D_POOL), F32)
    for gi in range(len(POOL_WINDOWS)):
        sl = slice(gi * POOL_GROUP, (gi + 1) * POOL_GROUP)
        wg_bd = wg_bd.at[sl, sl].set(w_pool_grp[i, gi])
    w_r = jnp.zeros((D_MODEL, ROUTER_COLS), F32)
    w_r = w_r.at[:, 0:N_EXPERT_GROUPS].set(w_rg[i])
    w_r = w_r.at[:, N_EXPERT_GROUPS:N_EXPERT_GROUPS + N_EXPERTS].set(w_re[i].reshape(D_MODEL, N_EXPERTS))
    b_r = jnp.zeros((1, ROUTER_COLS), F32)
    b_r = b_r.at[0, 0:N_EXPERT_GROUPS].set(b_rg[i])
    b_r = b_r.at[0, N_EXPERT_GROUPS:N_EXPERT_GROUPS + N_EXPERTS].set(b_re[i].reshape(-1))
    wrt_hi = w_r.T.astype(BF16)
    wrt_lo = (w_r.T - wrt_hi.astype(F32)).astype(BF16)
    tok = jnp.arange(MOE_TILE, dtype=jnp.int32)
    upper = (tok[:, None] < tok[None, :]).astype(BF16)
    rr = jnp.arange(ROUTER_COLS, dtype=jnp.int32)
    lower = (rr[None, :] < rr[:, None]).astype(BF16)
    return dict(
        w_gate_e=w_gate[i].reshape(N_EXPERTS, D_MODEL, D_EXPERT), w_up_e=w_up[i].reshape(N_EXPERTS, D_MODEL, D_EXPERT),
        w_down_e=w_down[i].reshape(N_EXPERTS, D_EXPERT, D_MODEL),
        wrt_hi=wrt_hi, wrt_lo=wrt_lo, b_rt=b_r.reshape(ROUTER_COLS, 1), upper=upper, lower=lower,
        w_qkvu=w_in[i, :, 0:D_QKVU].astype(BF16), b_qkvu=b_in[i, 0:D_QKVU].reshape(1, -1),
        w_gate=w_in[i, :, D_QKVU:].astype(BF16), b_gate=b_in[i, D_QKVU:].reshape(1, -1),
        wg_bd=wg_bd.astype(BF16), s_pool=row(s_pool),
        w_pool_br=w_pool_br[i].astype(BF16), w_attn_br=w_attn_br[i].astype(BF16), w_o=w_o[i].astype(BF16),
        ln1_g=row(ln1_g), ln1_b=row(ln1_b), ln2_g=row(ln2_g), ln2_b=row(ln2_b),
        w_ple_gate=w_ple_gate[i].astype(BF16), b_ple_gate=row(b_ple_gate), w_ple=w_ple[i].astype(BF16),
        ln3_g=row(ln3_g), ln3_b=row(ln3_b))


def kernel(x_prompt, x_sample, p_prompt, p_sample, cache_k, cache_v, state_pool, page_table, w_in, b_in, lam_q1, lam_k1, lam_q2, lam_k2, g_sub, w_pool_grp, s_pool, w_pool_br, w_attn_br, w_o, ln1_g, ln1_b, w_rg, b_rg, w_re, b_re, w_gate, w_up, w_down, ln2_g, ln2_b, w_ple_gate, b_ple_gate, w_ple, ln3_g, ln3_b):
    assert w_in.shape[0] == DEPTH == 1
    bsz, seq, _ = x_prompt.shape
    dbs, dseq, _ = x_sample.shape
    n_past = page_table.shape[1] * PAGE_SIZE
    wts = _prepare_weights(w_in, b_in, w_pool_grp, s_pool, w_pool_br, w_attn_br, w_o, ln1_g, ln1_b,
                           w_rg, b_rg, w_re, b_re, w_gate, w_up, w_down, ln2_g, ln2_b,
                           w_ple_gate, b_ple_gate, w_ple, ln3_g, ln3_b)
    lam_vecs = [a[0].reshape(1, HEAD_DIM).astype(F32) for a in (lam_q1, lam_k1, lam_q2, lam_k2)]
    g_row = g_sub[0].reshape(1, V_DIM).astype(F32)

    xp = x_prompt.reshape(bsz * seq, D_MODEL)
    qt_p, kf_p, vf_p, kb_p, vt_p, yp_p, tail_p = _proj_prompt(
        xp, wts["w_qkvu"], wts["b_qkvu"], wts["wg_bd"], wts["s_pool"], bsz, seq)
    ya_p = _attn_prompt(qt_p, kb_p, vt_p, lam_vecs, g_row, bsz, seq)

    xs = x_sample.reshape(dbs * dseq, D_MODEL)
    state_pad = jnp.pad(state_pool[0], ((0, 0), (POOL_HIST - POOL_STATE, 0), (0, 0)))
    q_s, kf_s, vf_s, yp_s, tail_s = _proj_sample(
        xs, wts["w_qkvu"], wts["b_qkvu"], wts["wg_bd"], wts["s_pool"], state_pad, dbs, dseq, n_past)
    ya_s = _attn_sample(q_s, kf_s, vf_s, cache_k, cache_v, page_table, lam_vecs, g_row, dseq)

    tiles_p = (bsz * seq) // MOE_TILE
    tiles_s = (dbs * dseq) // MOE_TILE
    n_tiles = tiles_p + tiles_s
    shared = _merge(xp, yp_p, ya_p, wts, 0, n_tiles)
    x1f, xsort, route, cnt = _merge(xs, yp_s, ya_s, wts, tiles_p, n_tiles, carried=shared)
    cnt_pad = cnt[:, 0].reshape(n_tiles, ROUTER_COLS)[:, N_EXPERT_GROUPS:N_EXPERT_GROUPS + N_EXPERTS]
    n_steps = (n_tiles * REGION_GRANULES + N_EXPERTS * (MOE_SLOTS - 1) + MOE_SLOTS - 1) // MOE_SLOTS
    gin, step_expert, loc = _granule_schedule(cnt_pad.astype(jnp.int32), n_steps)
    ysort = _experts(xsort, gin, step_expert, wts["w_gate_e"], wts["w_up_e"], wts["w_down_e"], n_steps)
    y_p = _final(x1f, route, ysort, loc, p_prompt[0].reshape(bsz * seq, D_PLE), wts, 0)
    y_s = _final(x1f, route, ysort, loc, p_sample[0].reshape(dbs * dseq, D_PLE), wts, tiles_p)

    drop = POOL_HIST - POOL_STATE
    return (y_p.reshape(bsz, seq, D_MODEL),
            y_s.reshape(dbs, dseq, D_MODEL),
            kf_p.reshape(1, bsz, seq, N_HEADS, V_DIM),
            vf_p.reshape(1, bsz, seq, N_HEADS, V_DIM),
            tail_p[None, :, drop:, :],
            kf_s.reshape(1, dbs, dseq, N_HEADS, V_DIM),
            vf_s.reshape(1, dbs, dseq, N_HEADS, V_DIM),
            tail_s[None, :, drop:, :])
```

```python
import functools
import math

import jax
import jax.numpy as jnp
from jax import lax
from jax.experimental import pallas as pl
from jax.experimental.pallas import tpu as pltpu

F32 = jnp.float32
BF16 = jnp.bfloat16

D_MODEL = 1024
N_HEADS = 4
HEAD_DIM = 64
V_DIM = 2 * HEAD_DIM
D_QK = N_HEADS * 2 * HEAD_DIM
D_ATTN = N_HEADS * V_DIM
D_POOL = 512
POOL_WINDOWS = (2, 4, 8, 16)
POOL_GROUP = D_POOL // len(POOL_WINDOWS)
POOL_STATE = max(POOL_WINDOWS) - 1
POOL_HIST = POOL_STATE + 1
N_EXPERT_GROUPS = 4
EXPERTS_PER_GROUP = 8
N_EXPERTS = N_EXPERT_GROUPS * EXPERTS_PER_GROUP
D_EXPERT = 256
D_PLE = 256
PAGE_SIZE = 128
LN_EPS = 1e-5
DEPTH = 1
DEEPNORM_ALPHA = (2 * DEPTH) ** 0.25
LAM_INIT = 0.8 - 0.6 * math.exp(-0.3 * 0)
D_QKVU = D_POOL + 2 * D_QK + D_ATTN
ROUTER_COLS = 128
TOP_K_IN_GROUP = 2
MOE_TILE = 512
MOE_GRANULE = 16
MOE_REGION = -(-(TOP_K_IN_GROUP * MOE_TILE + N_EXPERTS * (MOE_GRANULE - 1)) // 128) * 128
REGION_GRANULES = MOE_REGION // MOE_GRANULE
MOE_SLOTS = 32
V7X_VMEM_LIMIT = 56 * 1024 * 1024
NEG_INF = float("-inf")


def _sigmoid(x):
    return 1.0 / (1.0 + jnp.exp(-x))


def _layer_norm(x, g, b):
    mu = jnp.mean(x, axis=-1, keepdims=True)
    xc = x - mu
    var = jnp.mean(xc * xc, axis=-1, keepdims=True)
    return xc * lax.rsqrt(var + LN_EPS) * g + b


def _lam_value(lq1_ref, lk1_ref, lq2_ref, lk2_ref):
    a = jnp.sum(lq1_ref[...] * lk1_ref[...], axis=1, keepdims=True)
    b = jnp.sum(lq2_ref[...] * lk2_ref[...], axis=1, keepdims=True)
    return jnp.exp(a) - jnp.exp(b) + LAM_INIT


def _window_sums(e):
    outs = []
    for gi, w in enumerate(POOL_WINDOWS):
        s = e[:, gi * POOL_GROUP:(gi + 1) * POOL_GROUP]
        step = 1
        while step < w:
            s = s + pltpu.roll(s, step, axis=0)
            step *= 2
        outs.append(s)
    return outs


def _pool_branch(wins, u, inv_cnts, wg_ref, sp_ref):
    ds = []
    for gi in range(len(POOL_WINDOWS)):
        ds.append(wins[gi] * inv_cnts[gi] - u[:, gi * POOL_GROUP:(gi + 1) * POOL_GROUP])
    d = jnp.concatenate(ds, axis=1).astype(BF16)
    return jnp.dot(d, wg_ref[...], preferred_element_type=F32) * sp_ref[...]


def _store_head_major(ref, x, n_rows):
    for hh in range(N_HEADS):
        ref[pl.ds(hh, n_rows, stride=N_HEADS), :] = x[:, hh * V_DIM:(hh + 1) * V_DIM]


def _proj_prompt_kernel(x_ref, w_ref, b_ref, wg_ref, sp_ref,
                        qt_ref, kf_ref, vf_ref, kb_ref, vt_ref, yp_ref, tail_ref,
                        ext_ref, *, tm, tiles_per_seq):
    t_in_seq = pl.program_id(0) % tiles_per_seq
    z = jnp.dot(x_ref[...].astype(BF16), w_ref[...], preferred_element_type=F32) + b_ref[...]
    u = z[:, 0:D_POOL]
    q = z[:, D_POOL:D_POOL + D_QK]
    k = z[:, D_POOL + D_QK:D_POOL + 2 * D_QK]
    v = z[:, D_POOL + 2 * D_QK:D_QKVU]
    qt_ref[...] = (q * (HEAD_DIM ** -0.5)).T.astype(BF16)
    vt_ref[...] = v.T.astype(BF16)
    kb_ref[...] = k.astype(BF16)
    _store_head_major(kf_ref, k, tm)
    _store_head_major(vf_ref, v, tm)

    @pl.when(t_in_seq == 0)
    def _():
        ext_ref[0:POOL_HIST, :] = jnp.zeros((POOL_HIST, D_POOL), F32)

    ext_ref[POOL_HIST:POOL_HIST + tm, :] = u
    wins = [s[POOL_HIST:, :] for s in _window_sums(ext_ref[...])]
    pos = t_in_seq * tm + lax.broadcasted_iota(jnp.int32, (tm, 1), 0)
    inv_cnts = [1.0 / jnp.minimum(pos + 1, w).astype(F32) for w in POOL_WINDOWS]
    yp_ref[...] = _pool_branch(wins, u, inv_cnts, wg_ref, sp_ref).astype(BF16)
    tail = u[tm - POOL_HIST:, :]
    ext_ref[0:POOL_HIST, :] = tail
    tail_ref[...] = tail


def _proj_sample_kernel(x_ref, w_ref, b_ref, wg_ref, sp_ref, st_ref,
                        q_ref, kf_ref, vf_ref, yp_ref, tail_ref,
                        ext_ref, *, n_seq, t_new, n_past):
    z = jnp.dot(x_ref[...].astype(BF16), w_ref[...], preferred_element_type=F32) + b_ref[...]
    u = z[:, 0:D_POOL]
    q_ref[...] = z[:, D_POOL:D_POOL + D_QK] * (HEAD_DIM ** -0.5)
    _store_head_major(kf_ref, z[:, D_POOL + D_QK:D_POOL + 2 * D_QK], n_seq * t_new)
    _store_head_major(vf_ref, z[:, D_POOL + 2 * D_QK:D_QKVU], n_seq * t_new)
    rows = POOL_HIST + t_new
    ext_ref[:, 0:POOL_HIST, :] = st_ref[...]
    ext_ref[:, POOL_HIST:rows, :] = u.reshape(n_seq, t_new, D_POOL)
    e3 = ext_ref[...]
    wins = [s.reshape(n_seq, rows, POOL_GROUP)[:, POOL_HIST:, :].reshape(n_seq * t_new, POOL_GROUP)
            for s in _window_sums(e3.reshape(n_seq * rows, D_POOL))]
    inv_cnts = [1.0 / float(min(n_past + 1, w)) for w in POOL_WINDOWS]
    yp_ref[...] = _pool_branch(wins, u, inv_cnts, wg_ref, sp_ref).astype(BF16)
    tail_ref[...] = e3[:, rows - POOL_HIST:, :]


def _attn_prompt_body(slope, lam, qt_ref, k_ref, vt_ref, boff_ref, bdiag_ref, g_ref, o_ref,
                      m_ref, l_ref, acc_ref, *, tq):
    qi = pl.program_id(2)
    qt = qt_ref[...]
    row = lax.broadcasted_iota(jnp.int32, qt.shape, 0)
    zero = jnp.zeros_like(qt)
    q2t = jnp.concatenate([jnp.where(row < HEAD_DIM, qt, zero), jnp.where(row >= HEAD_DIM, qt, zero)], axis=1)
    m_ref[...] = jnp.full(m_ref.shape, NEG_INF, F32)
    l_ref[...] = jnp.zeros(l_ref.shape, F32)
    acc_ref[...] = jnp.zeros(acc_ref.shape, F32)

    def block(kb, bias_ref, shift):
        k0 = pl.multiple_of(kb * tq, tq)
        kblk = k_ref[pl.ds(k0, tq), :]
        vtblk = vt_ref[:, pl.ds(k0, tq)]
        t = jnp.dot(kblk, q2t, preferred_element_type=F32) + bias_ref[...]
        m_prev = m_ref[...]
        m_new = jnp.maximum(m_prev, jnp.max(t, axis=0, keepdims=True) + shift)
        p = jnp.exp(t - (m_new - shift))
        alpha = jnp.exp(m_prev - m_new)
        l_ref[...] = alpha * l_ref[...] + jnp.sum(p, axis=0, keepdims=True)
        acc_ref[...] = alpha * acc_ref[...] + jnp.dot(vtblk, p.astype(BF16), preferred_element_type=F32)
        m_ref[...] = m_new

    def body(kb, carry):
        block(kb, boff_ref, -slope * ((qi - kb) * tq).astype(F32))
        return carry

    lax.fori_loop(0, qi, body, 0)
    block(qi, bdiag_ref, jnp.float32(0.0))

    on = acc_ref[...] / l_ref[...]
    o = (on[:, 0:tq] - lam * on[:, tq:]).T
    o = o * lax.rsqrt(jnp.mean(o * o, axis=1, keepdims=True) + LN_EPS) * g_ref[...] * (1.0 - LAM_INIT)
    o_ref[...] = o.astype(BF16)


def _attn_sample_body(lam, q_ref, kn_ref, vn_ref, bias_ref, g_ref, k_pages, v_pages, o_ref, *, t_new):
    n_pages = len(k_pages)
    n_past = n_pages * PAGE_SIZE
    q = q_ref[...]
    lane = lax.broadcasted_iota(jnp.int32, (t_new, V_DIM), 1)
    pad = jnp.zeros((PAGE_SIZE - t_new, V_DIM), BF16)
    nt_dims = (((1,), (1,)), ((), ()))

    def head_rows(pages, new_ref, hh):
        past = jnp.concatenate([pg[pl.ds(hh, PAGE_SIZE, stride=N_HEADS), :] for pg in pages], axis=0)
        new = jnp.concatenate([new_ref[pl.ds(hh, t_new, stride=N_HEADS), :].astype(BF16), pad], axis=0)
        return past.astype(BF16), new

    s_rows = []
    for hh in range(N_HEADS):
        qh = q[:, hh * V_DIM:(hh + 1) * V_DIM]
        qh2 = jnp.concatenate([jnp.where(lane < HEAD_DIM, qh, 0.0), jnp.where(lane >= HEAD_DIM, qh, 0.0)],
                              axis=0).astype(BF16)
        k_past, k_new = head_rows(k_pages, kn_ref, hh)
        s_rows.append(jnp.concatenate(
            [lax.dot_general(qh2, k_past, nt_dims, preferred_element_type=F32),
             lax.dot_general(qh2, k_new, nt_dims, preferred_element_type=F32)], axis=1))
    s = jnp.concatenate(s_rows, axis=0) + bias_ref[...]
    m = jnp.max(s, axis=1, keepdims=True)
    p = jnp.exp(s - m)
    p = p / jnp.sum(p, axis=1, keepdims=True)
    outs = []
    for hh in range(N_HEADS):
        r0 = hh * 2 * t_new
        a = (p[r0:r0 + t_new, :] - lam * p[r0 + t_new:r0 + 2 * t_new, :]).astype(BF16)
        v_past, v_new = head_rows(v_pages, vn_ref, hh)
        oh = (jnp.dot(a[:, 0:n_past], v_past, preferred_element_type=F32)
              + jnp.dot(a[:, n_past:], v_new, preferred_element_type=F32))
        oh = oh * lax.rsqrt(jnp.mean(oh * oh, axis=1, keepdims=True) + LN_EPS) * g_ref[...] * (1.0 - LAM_INIT)
        outs.append(oh)
    o_ref[...] = jnp.concatenate(outs, axis=1)


def _attn_kernel(pt_ref, slopes_ref, lq1_ref, lk1_ref, lq2_ref, lk2_ref,
                 qt_ref, k_ref, vt_ref, boff_ref, bdiag_ref, g_ref,
                 qs_ref, kn_ref, vn_ref, bias_s_ref, *rest, n_pages, t_new, tq):
    k_pages = rest[:n_pages]
    v_pages = rest[n_pages:2 * n_pages]
    op_ref, os_ref, m_ref, l_ref, acc_ref = rest[2 * n_pages:]
    lam = _lam_value(lq1_ref, lk1_ref, lq2_ref, lk2_ref)
    _attn_sample_body(lam, qs_ref, kn_ref, vn_ref, bias_s_ref, g_ref, k_pages, v_pages, os_ref, t_new=t_new)
    _attn_prompt_body(slopes_ref[pl.program_id(1)], lam, qt_ref, k_ref, vt_ref, boff_ref, bdiag_ref, g_ref,
                      op_ref, m_ref, l_ref, acc_ref, tq=tq)


def _merge_kernel(x_ref, yp_ref, ya_ref, wgate_ref, bgate_ref, wpb_ref, wab_ref, wo_ref,
                  g1_ref, b1_ref, wrt_hi_ref, wrt_lo_ref, brt_ref, upper_ref, lower_ref, *rest, tm):
    x1f_ref, xs_ref, route_ref, cnt_ref = rest[-4:]
    x = x_ref[...]
    zg = jnp.dot(x.astype(BF16), wgate_ref[...], preferred_element_type=F32) + bgate_ref[...]
    gp = _sigmoid(zg[:, 0:D_MODEL])
    ga = _sigmoid(zg[:, D_MODEL:])
    merged = (gp * jnp.dot(yp_ref[...].astype(BF16), wpb_ref[...], preferred_element_type=F32)
              + ga * jnp.dot(ya_ref[...].astype(BF16), wab_ref[...], preferred_element_type=F32))
    x1 = _layer_norm(DEEPNORM_ALPHA * x + jnp.dot(merged.astype(BF16), wo_ref[...], preferred_element_type=F32),
                     g1_ref[...], b1_ref[...])
    x1f_ref[...] = x1
    x1_hi = x1.astype(BF16)
    x1_lo = (x1 - x1_hi.astype(F32)).astype(BF16)
    nt_dims = (((1,), (1,)), ((), ()))
    lg = (lax.dot_general(wrt_hi_ref[...], x1_hi, nt_dims, preferred_element_type=F32)
          + lax.dot_general(wrt_hi_ref[...], x1_lo, nt_dims, preferred_element_type=F32)
          + lax.dot_general(wrt_lo_ref[...], x1_hi, nt_dims, preferred_element_type=F32)) + brt_ref[...]
    row = lax.broadcasted_iota(jnp.int32, lg.shape, 0)
    big = jnp.int32(4 * ROUTER_COLS)
    lgg = jnp.where(row < N_EXPERT_GROUPS, lg, NEG_INF)
    gmax = jnp.max(lgg, axis=0, keepdims=True)
    g_w = 1.0 / jnp.sum(jnp.exp(lgg - gmax), axis=0, keepdims=True)
    g_idx = jnp.min(jnp.where(lgg == gmax, row, big), axis=0, keepdims=True)
    lo_row = N_EXPERT_GROUPS + g_idx * EXPERTS_PER_GROUP
    le = jnp.where((row >= lo_row) & (row < lo_row + EXPERTS_PER_GROUP), lg, NEG_INF)
    v1 = jnp.max(le, axis=0, keepdims=True)
    i1 = jnp.min(jnp.where(le == v1, row, big), axis=0, keepdims=True)
    le2 = jnp.where(row == i1, NEG_INF, le)
    v2 = jnp.max(le2, axis=0, keepdims=True)
    i2 = jnp.min(jnp.where(le2 == v2, row, big), axis=0, keepdims=True)
    e21 = jnp.exp(v2 - v1)
    c1 = g_w / (1.0 + e21)
    c2 = g_w * e21 / (1.0 + e21)
    sel1 = row == i1
    sel2 = row == i2
    member = jnp.where(sel1 | sel2, 1.0, 0.0)
    before = jnp.dot(member.astype(BF16), upper_ref[...], preferred_element_type=F32)
    cnt = jnp.sum(member, axis=1, keepdims=True)
    cnt_pad = jnp.ceil(cnt * (1.0 / MOE_GRANULE)) * MOE_GRANULE
    cnt_b = jnp.broadcast_to(cnt_pad, (ROUTER_COLS, ROUTER_COLS))
    seg_start = jnp.dot(lower_ref[...], cnt_b.astype(BF16), preferred_element_type=F32)
    cnt_ref[...] = cnt_b
    slot = seg_start[:, 0:1] + before
    pos1 = jnp.sum(jnp.where(sel1, slot, 0.0), axis=0, keepdims=True)
    pos2 = jnp.sum(jnp.where(sel2, slot, 0.0), axis=0, keepdims=True)
    srow = lax.broadcasted_iota(jnp.int32, (xs_ref.shape[0], tm), 0).astype(F32)
    perm = jnp.where((srow == pos1) | (srow == pos2), 1.0, 0.0).astype(BF16)
    xs_ref[...] = jnp.dot(perm, x1_hi, preferred_element_type=F32).astype(BF16)
    info = (jnp.where(row == 0, pos1, 0.0) + jnp.where(row == 1, pos2, 0.0)
            + jnp.where(row == 2, c1, 0.0) + jnp.where(row == 3, c2, 0.0))
    route_ref[...] = info.T


def _experts_kernel(gin_ref, sexp_ref, *rest, n_slots):
    x_refs = rest[:n_slots]
    wg_ref, wu_ref, wd_ref, y_ref, wg_b, wu_b, wd_b = rest[n_slots:]
    t = pl.program_id(0)
    expert = sexp_ref[t]
    is_expert = expert < N_EXPERTS

    @pl.when(is_expert & ((t == 0) | (expert != sexp_ref[jnp.maximum(t - 1, 0)])))
    def _():
        wg_b[...] = wg_ref[...].astype(BF16)
        wu_b[...] = wu_ref[...].astype(BF16)
        wd_b[...] = wd_ref[...].astype(BF16)

    @pl.when(is_expert)
    def _():
        xg = jnp.concatenate([r[...] for r in x_refs], axis=0)
        gate = jnp.dot(xg, wg_b[...], preferred_element_type=F32)
        up = jnp.dot(xg, wu_b[...], preferred_element_type=F32)
        hact = (gate * _sigmoid(gate) * up).astype(BF16)
        y_ref[...] = jnp.dot(hact, wd_b[...], preferred_element_type=F32).astype(BF16)

    @pl.when(jnp.logical_not(is_expert))
    def _():
        y_ref[...] = jnp.zeros(y_ref.shape, BF16)


def _final_kernel(loc_ref, x1f_ref, route_ref, *rest, n_granules):
    g_refs = rest[:n_granules]
    p_ref, g2_ref, b2_ref, wpg_ref, bpg_ref, wple_ref, g3_ref, b3_ref, y_ref = rest[n_granules:]
    route = route_ref[...]
    ys = jnp.concatenate([r[...] for r in g_refs], axis=0)
    scol = lax.broadcasted_iota(jnp.int32, (route.shape[0], ys.shape[0]), 1).astype(F32)
    comb = (jnp.where(scol == route[:, 0:1], route[:, 2:3], 0.0)
            + jnp.where(scol == route[:, 1:2], route[:, 3:4], 0.0)).astype(BF16)
    moe = jnp.dot(comb, ys, preferred_element_type=F32)
    x2 = _layer_norm(DEEPNORM_ALPHA * x1f_ref[...] + moe, g2_ref[...], b2_ref[...])
    gate_ple = _sigmoid(jnp.dot(x2.astype(BF16), wpg_ref[...], preferred_element_type=F32) + bpg_ref[...])
    pe = jnp.dot(p_ref[...].astype(BF16), wple_ref[...], preferred_element_type=F32)
    y_ref[...] = _layer_norm(DEEPNORM_ALPHA * x2 + gate_ple * pe, g3_ref[...], b3_ref[...])


def _full(shape):
    nd = len(shape)
    return pl.BlockSpec(shape, lambda *_: (0,) * nd)


def _cparams(sem):
    return pltpu.CompilerParams(dimension_semantics=sem, vmem_limit_bytes=V7X_VMEM_LIMIT)


def _proj_prompt(x2d, w_qkvu, b_qkvu, wg_bd, s_pool, n_seq, seq_len, tm=512):
    n = x2d.shape[0]
    tps = seq_len // tm
    row = lambda i: (i, 0)
    col = lambda i: (0, i)
    outs = pl.pallas_call(
        functools.partial(_proj_prompt_kernel, tm=tm, tiles_per_seq=tps),
        grid=(n // tm,),
        in_specs=[pl.BlockSpec((tm, D_MODEL), row), _full(w_qkvu.shape), _full(b_qkvu.shape),
                  _full(wg_bd.shape), _full(s_pool.shape)],
        out_specs=[pl.BlockSpec((D_QK, tm), col), pl.BlockSpec((tm * N_HEADS, V_DIM), row),
                   pl.BlockSpec((tm * N_HEADS, V_DIM), row), pl.BlockSpec((tm, D_QK), row),
                   pl.BlockSpec((D_ATTN, tm), col), pl.BlockSpec((tm, D_POOL), row),
                   pl.BlockSpec((None, POOL_HIST, D_POOL), lambda i: (i // tps, 0, 0))],
        out_shape=[jax.ShapeDtypeStruct((D_QK, n), BF16), jax.ShapeDtypeStruct((n * N_HEADS, V_DIM), F32),
                   jax.ShapeDtypeStruct((n * N_HEADS, V_DIM), F32), jax.ShapeDtypeStruct((n, D_QK), BF16),
                   jax.ShapeDtypeStruct((D_ATTN, n), BF16), jax.ShapeDtypeStruct((n, D_POOL), BF16),
                   jax.ShapeDtypeStruct((n_seq, POOL_HIST, D_POOL), F32)],
        scratch_shapes=[pltpu.VMEM((POOL_HIST + tm, D_POOL), F32)],
        compiler_params=_cparams(("arbitrary",)),
        name="proj_pool_prompt",
    )(x2d, w_qkvu, b_qkvu, wg_bd, s_pool)
    return outs


def _proj_sample(x2d, w_qkvu, b_qkvu, wg_bd, s_pool, state_pad, n_seq, t_new, n_past):
    n = x2d.shape[0]
    rows = POOL_HIST + t_new
    return pl.pallas_call(
        functools.partial(_proj_sample_kernel, n_seq=n_seq, t_new=t_new, n_past=n_past),
        grid=(1,),
        in_specs=[_full(x2d.shape), _full(w_qkvu.shape), _full(b_qkvu.shape), _full(wg_bd.shape),
                  _full(s_pool.shape), _full(state_pad.shape)],
        out_specs=[_full((n, D_QK)), _full((n * N_HEADS, V_DIM)), _full((n * N_HEADS, V_DIM)),
                   _full((n, D_POOL)), _full((n_seq, POOL_HIST, D_POOL))],
        out_shape=[jax.ShapeDtypeStruct((n, D_QK), F32), jax.ShapeDtypeStruct((n * N_HEADS, V_DIM), F32),
                   jax.ShapeDtypeStruct((n * N_HEADS, V_DIM), F32), jax.ShapeDtypeStruct((n, D_POOL), BF16),
                   jax.ShapeDtypeStruct((n_seq, POOL_HIST, D_POOL), F32)],
        scratch_shapes=[pltpu.VMEM((n_seq, rows, D_POOL), F32)],
        compiler_params=_cparams(("arbitrary",)),
        name="proj_pool_sample",
    )(x2d, w_qkvu, b_qkvu, wg_bd, s_pool, state_pad)


def _alibi_slopes():
    return 2.0 ** (-8.0 * jnp.arange(1, N_HEADS + 1, dtype=F32) / N_HEADS)


def _attention(qt, kb, vt, q_s, k_new, v_new, cache_k, cache_v, page_table, lam_vecs, g_sub,
               n_seq, seq_len, t_new, tq=512):
    n = kb.shape[0]
    nq = seq_len // tq
    n_smp, n_pages = page_table.shape
    n_past = n_pages * PAGE_SIZE
    assert n_smp == n_seq * N_HEADS * nq, (n_smp, n_seq, nq)
    slopes = _alibi_slopes()
    rel = (jnp.arange(tq, dtype=jnp.int32)[None, :] - jnp.arange(tq, dtype=jnp.int32)[:, None])
    rel2 = jnp.concatenate([rel, rel], axis=1)
    b_off = -slopes[:, None, None] * rel2.astype(F32)[None]
    b_diag = jnp.where(rel2[None] >= 0, b_off, NEG_INF)
    r = jnp.arange(2 * N_HEADS * t_new, dtype=jnp.int32)
    r_head, r_q = r // (2 * t_new), r % t_new
    row_slope = slopes[r_head][:, None]
    kpos = jnp.arange(n_past, dtype=jnp.int32)[None, :]
    b_past = -row_slope * (n_past + r_q[:, None] - kpos).astype(F32)
    j = jnp.arange(PAGE_SIZE, dtype=jnp.int32)[None, :]
    dist_new = r_q[:, None] - j
    b_new = jnp.where((dist_new >= 0) & (j < t_new), -row_slope * dist_new.astype(F32), NEG_INF)
    bias_s = jnp.concatenate([b_past, b_new], axis=1)
    page_rows = PAGE_SIZE * N_HEADS
    ck = cache_k.reshape(-1, V_DIM)
    cv = cache_v.reshape(-1, V_DIM)
    pt = page_table.reshape(-1).astype(jnp.int32)
    step = lambda b, h, i: (b * N_HEADS + h) * nq + i
    const2 = lambda b, h, i, pt_ref: (0, 0)
    vec = pl.BlockSpec((1, HEAD_DIM), const2)
    tok = pl.BlockSpec((t_new, D_QK), lambda b, h, i, pt_ref: (step(b, h, i), 0))
    tok_hm = pl.BlockSpec((t_new * N_HEADS, V_DIM), lambda b, h, i, pt_ref: (step(b, h, i), 0))
    prompt_tile = pl.BlockSpec((tq, V_DIM), lambda b, h, i, pt_ref: (b * nq + i, h))

    def page_spec(pi):
        return pl.BlockSpec((page_rows, V_DIM), lambda b, h, i, pt_ref: (pt_ref[step(b, h, i) * n_pages + pi], 0))

    grid_spec = pltpu.PrefetchScalarGridSpec(
        num_scalar_prefetch=1,
        grid=(n_seq, N_HEADS, nq),
        in_specs=[pl.BlockSpec(memory_space=pltpu.SMEM), vec, vec, vec, vec,
                  pl.BlockSpec((V_DIM, tq), lambda b, h, i, pt_ref: (h, b * nq + i)),
                  pl.BlockSpec((seq_len, V_DIM), lambda b, h, i, pt_ref: (b, h)),
                  pl.BlockSpec((V_DIM, seq_len), lambda b, h, i, pt_ref: (h, b)),
                  pl.BlockSpec((None, tq, 2 * tq), lambda b, h, i, pt_ref: (h, 0, 0)),
                  pl.BlockSpec((None, tq, 2 * tq), lambda b, h, i, pt_ref: (h, 0, 0)),
                  pl.BlockSpec((1, V_DIM), const2),
                  tok, tok_hm, tok_hm, pl.BlockSpec(bias_s.shape, const2)]
                 + [page_spec(pi) for pi in range(n_pages)] * 2,
        out_specs=[prompt_tile, tok],
        scratch_shapes=[pltpu.VMEM((1, 2 * tq), F32), pltpu.VMEM((1, 2 * tq), F32),
                        pltpu.VMEM((V_DIM, 2 * tq), F32)],
    )
    return pl.pallas_call(
        functools.partial(_attn_kernel, n_pages=n_pages, t_new=t_new, tq=tq),
        grid_spec=grid_spec,
        out_shape=[jax.ShapeDtypeStruct((n, D_ATTN), BF16), jax.ShapeDtypeStruct((n_smp * t_new, D_ATTN), F32)],
        compiler_params=_cparams(("arbitrary", "arbitrary", "arbitrary")),
        name="attention",
    )(pt, slopes, *lam_vecs, qt, kb, vt, b_off, b_diag, g_sub, q_s, k_new, v_new, bias_s,
      *([ck] * n_pages), *([cv] * n_pages))


def _merge(x2d, y_pool, y_attn, wts, tile0, n_tiles_total, carried=None):
    tm = MOE_TILE
    n = x2d.shape[0]
    row = lambda i: (i, 0)
    out_row = lambda i: (i + tile0, 0)
    consts = [wts[k] for k in ("w_gate", "b_gate", "w_pool_br", "w_attn_br", "w_o", "ln1_g", "ln1_b",
                               "wrt_hi", "wrt_lo", "b_rt", "upper", "lower")]
    operands = [x2d, y_pool, y_attn, *consts]
    in_specs = [pl.BlockSpec((tm, D_MODEL), row), pl.BlockSpec((tm, D_POOL), row),
                pl.BlockSpec((tm, D_ATTN), row)] + [_full(c.shape) for c in consts]
    aliases = {}
    if carried is not None:
        aliases = {len(operands) + k: k for k in range(len(carried))}
        operands += list(carried)
        in_specs += [pl.BlockSpec(memory_space=pl.ANY)] * len(carried)
    return pl.pallas_call(
        functools.partial(_merge_kernel, tm=tm),
        grid=(n // tm,),
        in_specs=in_specs,
        out_specs=[pl.BlockSpec((tm, D_MODEL), out_row), pl.BlockSpec((MOE_REGION, D_MODEL), out_row),
                   pl.BlockSpec((tm, ROUTER_COLS), out_row), pl.BlockSpec((ROUTER_COLS, ROUTER_COLS), out_row)],
        out_shape=[jax.ShapeDtypeStruct((n_tiles_total * tm, D_MODEL), F32),
                   jax.ShapeDtypeStruct((n_tiles_total * MOE_REGION, D_MODEL), BF16),
                   jax.ShapeDtypeStruct((n_tiles_total * tm, ROUTER_COLS), F32),
                   jax.ShapeDtypeStruct((n_tiles_total * ROUTER_COLS, ROUTER_COLS), F32)],
        input_output_aliases=aliases,
        compiler_params=_cparams(("arbitrary",)),
        name="merge_ln1_route_dispatch",
    )(*operands)


def _granule_schedule(cnt_pad, n_steps):
    n_tiles = cnt_pad.shape[0]
    gc = cnt_pad // MOE_GRANULE
    seg = jnp.cumsum(gc, axis=1) - gc
    tot = jnp.sum(gc, axis=0)
    ahead = jnp.cumsum(gc, axis=0) - gc
    slots_e = ((tot + MOE_SLOTS - 1) // MOE_SLOTS) * MOE_SLOTS
    first_slot = jnp.cumsum(slots_e) - slots_e
    p = jnp.arange(n_steps * MOE_SLOTS, dtype=jnp.int32)[None, :]
    in_e = ((p >= first_slot[:, None]) & (p < (first_slot + slots_e)[:, None])).astype(jnp.int32)
    q = p - jnp.sum(in_e * first_slot[:, None], axis=0, keepdims=True)
    valid = q < jnp.sum(in_e * tot[:, None], axis=0, keepdims=True)
    pick = lambda tab: jnp.sum(tab[:, :, None] * in_e[None, :, :], axis=1)
    ahead_p, gc_p, seg_p = pick(ahead), pick(gc), pick(seg)
    in_t = ((q >= ahead_p) & (q < ahead_p + gc_p)).astype(jnp.int32)
    tile_base = (jnp.arange(n_tiles, dtype=jnp.int32) * REGION_GRANULES)[:, None]
    gid = jnp.sum(in_t * (tile_base + seg_p + q - ahead_p), axis=0, keepdims=True)
    gin = jnp.where(valid, gid, 0).astype(jnp.int32).reshape(-1)
    e_id = jnp.sum(in_e * jnp.arange(N_EXPERTS, dtype=jnp.int32)[:, None], axis=0, keepdims=True)
    step_expert = jnp.where(valid, e_id, N_EXPERTS).reshape(n_steps, MOE_SLOTS)[:, 0].astype(jnp.int32)
    s = jnp.arange(REGION_GRANULES, dtype=jnp.int32)[None, None, :]
    in_g = ((s >= seg[:, :, None]) & (s < (seg + gc)[:, :, None])).astype(jnp.int32)
    loc = jnp.sum(in_g * ((first_slot[None, :] + ahead - seg)[:, :, None] + s), axis=1)
    used = jnp.sum(in_g, axis=1) > 0
    loc = jnp.where(used, loc, loc[:, 0:1]).astype(jnp.int32)
    return gin, step_expert, loc.reshape(-1)


def _experts(xs, gin, step_expert, w_gate, w_up, w_down, n_steps):
    rows = MOE_SLOTS * MOE_GRANULE

    def slot_spec(k):
        return pl.BlockSpec((MOE_GRANULE, D_MODEL), lambda t, gin_ref, se_ref: (gin_ref[t * MOE_SLOTS + k], 0))

    w_sel = lambda t, gin_ref, se_ref: (jnp.minimum(se_ref[t], N_EXPERTS - 1), 0, 0)
    grid_spec = pltpu.PrefetchScalarGridSpec(
        num_scalar_prefetch=2,
        grid=(n_steps,),
        in_specs=[slot_spec(k) for k in range(MOE_SLOTS)]
                 + [pl.BlockSpec((None, D_MODEL, D_EXPERT), w_sel), pl.BlockSpec((None, D_MODEL, D_EXPERT), w_sel),
                    pl.BlockSpec((None, D_EXPERT, D_MODEL), w_sel)],
        out_specs=pl.BlockSpec((rows, D_MODEL), lambda t, gin_ref, se_ref: (t, 0)),
        scratch_shapes=[pltpu.VMEM((D_MODEL, D_EXPERT), BF16), pltpu.VMEM((D_MODEL, D_EXPERT), BF16),
                        pltpu.VMEM((D_EXPERT, D_MODEL), BF16)],
    )
    return pl.pallas_call(
        functools.partial(_experts_kernel, n_slots=MOE_SLOTS),
        grid_spec=grid_spec,
        out_shape=jax.ShapeDtypeStruct((n_steps * rows, D_MODEL), BF16),
        compiler_params=_cparams(("arbitrary",)),
        name="expert_mlps",
    )(gin, step_expert, *([xs] * MOE_SLOTS), w_gate, w_up, w_down)


def _final(x1f, route, ys, loc, p2d, wts, tile0):
    tm = MOE_TILE
    n = p2d.shape[0]
    consts = [wts[k] for k in ("ln2_g", "ln2_b", "w_ple_gate", "b_ple_gate", "w_ple", "ln3_g", "ln3_b")]

    def granule_spec(s):
        return pl.BlockSpec((MOE_GRANULE, D_MODEL),
                            lambda i, loc_ref: (loc_ref[(i + tile0) * REGION_GRANULES + s], 0))

    grid_spec = pltpu.PrefetchScalarGridSpec(
        num_scalar_prefetch=1,
        grid=(n // tm,),
        in_specs=[pl.BlockSpec((tm, D_MODEL), lambda i, loc_ref: (i + tile0, 0)),
                  pl.BlockSpec((tm, ROUTER_COLS), lambda i, loc_ref: (i + tile0, 0))]
                 + [granule_spec(s) for s in range(REGION_GRANULES)]
                 + [pl.BlockSpec((tm, D_PLE), lambda i, loc_ref: (i, 0))]
                 + [pl.BlockSpec(c.shape, lambda i, loc_ref: (0, 0)) for c in consts],
        out_specs=pl.BlockSpec((tm, D_MODEL), lambda i, loc_ref: (i, 0)),
    )
    return pl.pallas_call(
        functools.partial(_final_kernel, n_granules=REGION_GRANULES),
        grid_spec=grid_spec,
        out_shape=jax.ShapeDtypeStruct((n, D_MODEL), F32),
        compiler_params=_cparams(("arbitrary",)),
        name="combine_ln2_ple_ln3",
    )(loc, x1f, route, *([ys] * REGION_GRANULES), p2d, *consts)


def _prepare_weights(w_in, b_in, w_pool_grp, s_pool, w_pool_br, w_attn_br, w_o, ln1_g, ln1_b,
                     w_rg, b_rg, w_re, b_re, w_gate, w_up, w_down, ln2_g, ln2_b,
                     w_ple_gate, b_ple_gate, w_ple, ln3_g, ln3_b):
    i = 0
    row = lambda a: a[i].reshape(1, -1).astype(F32)
    wg_bd = jnp.zeros((D_POOL, D_POOL), F32)
    for gi in range(len(POOL_WINDOWS)):
        sl = slice(gi * POOL_GROUP, (gi + 1) * POOL_GROUP)
        wg_bd = wg_bd.at[sl, sl].set(w_pool_grp[i, gi])
    w_r = jnp.zeros((D_MODEL, ROUTER_COLS), F32)
    w_r = w_r.at[:, 0:N_EXPERT_GROUPS].set(w_rg[i])
    w_r = w_r.at[:, N_EXPERT_GROUPS:N_EXPERT_GROUPS + N_EXPERTS].set(w_re[i].reshape(D_MODEL, N_EXPERTS))
    b_r = jnp.zeros((1, ROUTER_COLS), F32)
    b_r = b_r.at[0, 0:N_EXPERT_GROUPS].set(b_rg[i])
    b_r = b_r.at[0, N_EXPERT_GROUPS:N_EXPERT_GROUPS + N_EXPERTS].set(b_re[i].reshape(-1))
    wrt_hi = w_r.T.astype(BF16)
    wrt_lo = (w_r.T - wrt_hi.astype(F32)).astype(BF16)
    tok = jnp.arange(MOE_TILE, dtype=jnp.int32)
    upper = (tok[:, None] < tok[None, :]).astype(BF16)
    rr = jnp.arange(ROUTER_COLS, dtype=jnp.int32)
    lower = (rr[None, :] < rr[:, None]).astype(BF16)
    return dict(
        w_gate_e=w_gate[i].reshape(N_EXPERTS, D_MODEL, D_EXPERT), w_up_e=w_up[i].reshape(N_EXPERTS, D_MODEL, D_EXPERT),
        w_down_e=w_down[i].reshape(N_EXPERTS, D_EXPERT, D_MODEL),
        wrt_hi=wrt_hi, wrt_lo=wrt_lo, b_rt=b_r.reshape(ROUTER_COLS, 1), upper=upper, lower=lower,
        w_qkvu=w_in[i, :, 0:D_QKVU].astype(BF16), b_qkvu=b_in[i, 0:D_QKVU].reshape(1, -1),
        w_gate=w_in[i, :, D_QKVU:].astype(BF16), b_gate=b_in[i, D_QKVU:].reshape(1, -1),
        wg_bd=wg_bd.astype(BF16), s_pool=row(s_pool),
        w_pool_br=w_pool_br[i].astype(BF16), w_attn_br=w_attn_br[i].astype(BF16), w_o=w_o[i].astype(BF16),
        ln1_g=row(ln1_g), ln1_b=row(ln1_b), ln2_g=row(ln2_g), ln2_b=row(ln2_b),
        w_ple_gate=w_ple_gate[i].astype(BF16), b_ple_gate=row(b_ple_gate), w_ple=w_ple[i].astype(BF16),
        ln3_g=row(ln3_g), ln3_b=row(ln3_b))


def kernel(x_prompt, x_sample, p_prompt, p_sample, cache_k, cache_v, state_pool, page_table, w_in, b_in, lam_q1, lam_k1, lam_q2, lam_k2, g_sub, w_pool_grp, s_pool, w_pool_br, w_attn_br, w_o, ln1_g, ln1_b, w_rg, b_rg, w_re, b_re, w_gate, w_up, w_down, ln2_g, ln2_b, w_ple_gate, b_ple_gate, w_ple, ln3_g, ln3_b):
    assert w_in.shape[0] == DEPTH == 1
    bsz, seq, _ = x_prompt.shape
    dbs, dseq, _ = x_sample.shape
    n_past = page_table.shape[1] * PAGE_SIZE
    wts = _prepare_weights(w_in, b_in, w_pool_grp, s_pool, w_pool_br, w_attn_br, w_o, ln1_g, ln1_b,
                           w_rg, b_rg, w_re, b_re, w_gate, w_up, w_down, ln2_g, ln2_b,
                           w_ple_gate, b_ple_gate, w_ple, ln3_g, ln3_b)
    lam_vecs = [a[0].reshape(1, HEAD_DIM).astype(F32) for a in (lam_q1, lam_k1, lam_q2, lam_k2)]
    g_row = g_sub[0].reshape(1, V_DIM).astype(F32)

    xp = x_prompt.reshape(bsz * seq, D_MODEL)
    qt_p, kf_p, vf_p, kb_p, vt_p, yp_p, tail_p = _proj_prompt(
        xp, wts["w_qkvu"], wts["b_qkvu"], wts["wg_bd"], wts["s_pool"], bsz, seq)

    xs = x_sample.reshape(dbs * dseq, D_MODEL)
    state_pad = jnp.pad(state_pool[0], ((0, 0), (POOL_HIST - POOL_STATE, 0), (0, 0)))
    q_s, kf_s, vf_s, yp_s, tail_s = _proj_sample(
        xs, wts["w_qkvu"], wts["b_qkvu"], wts["wg_bd"], wts["s_pool"], state_pad, dbs, dseq, n_past)

    ya_p, ya_s = _attention(qt_p, kb_p, vt_p, q_s, kf_s, vf_s, cache_k, cache_v, page_table, lam_vecs, g_row,
                            bsz, seq, dseq)

    tiles_p = (bsz * seq) // MOE_TILE
    tiles_s = (dbs * dseq) // MOE_TILE
    n_tiles = tiles_p + tiles_s
    shared = _merge(xp, yp_p, ya_p, wts, 0, n_tiles)
    x1f, xsort, route, cnt = _merge(xs, yp_s, ya_s, wts, tiles_p, n_tiles, carried=shared)
    cnt_pad = cnt[:, 0].reshape(n_tiles, ROUTER_COLS)[:, N_EXPERT_GROUPS:N_EXPERT_GROUPS + N_EXPERTS]
    n_steps = (n_tiles * REGION_GRANULES + N_EXPERTS * (MOE_SLOTS - 1) + MOE_SLOTS - 1) // MOE_SLOTS
    gin, step_expert, loc = _granule_schedule(cnt_pad.astype(jnp.int32), n_steps)
    ysort = _experts(xsort, gin, step_expert, wts["w_gate_e"], wts["w_up_e"], wts["w_down_e"], n_steps)
    y_p = _final(x1f, route, ysort, loc, p_prompt[0].reshape(bsz * seq, D_PLE), wts, 0)
    y_s = _final(x1f, route, ysort, loc, p_sample[0].reshape(dbs * dseq, D_PLE), wts, tiles_p)

    drop = POOL_HIST - POOL_STATE
    return (y_p.reshape(bsz, seq, D_MODEL),
            y_s.reshape(dbs, dseq, D_MODEL),
            kf_p.reshape(1, bsz, seq, N_HEADS, V_DIM),
            vf_p.reshape(1, bsz, seq, N_HEADS, V_DIM),
            tail_p[None, :, drop:, :],
            kf_s.reshape(1, dbs, dseq, N_HEADS, V_DIM),
            vf_s.reshape(1, dbs, dseq, N_HEADS, V_DIM),
            tail_s[None, :, drop:, :])
```

```python
import functools
import math

import jax
import jax.numpy as jnp
from jax import lax
from jax.experimental import pallas as pl
from jax.experimental.pallas import tpu as pltpu

F32 = jnp.float32
BF16 = jnp.bfloat16

D_MODEL = 1024
N_HEADS = 4
HEAD_DIM = 64
V_DIM = 2 * HEAD_DIM
D_QK = N_HEADS * 2 * HEAD_DIM
D_ATTN = N_HEADS * V_DIM
D_POOL = 512
POOL_WINDOWS = (2, 4, 8, 16)
POOL_GROUP = D_POOL // len(POOL_WINDOWS)
POOL_STATE = max(POOL_WINDOWS) - 1
POOL_HIST = POOL_STATE + 1
N_EXPERT_GROUPS = 4
EXPERTS_PER_GROUP = 8
N_EXPERTS = N_EXPERT_GROUPS * EXPERTS_PER_GROUP
D_EXPERT = 256
D_PLE = 256
PAGE_SIZE = 128
LN_EPS = 1e-5
DEPTH = 1
DEEPNORM_ALPHA = (2 * DEPTH) ** 0.25
LAM_INIT = 0.8 - 0.6 * math.exp(-0.3 * 0)
D_QKVU = D_POOL + 2 * D_QK + D_ATTN
ROUTER_COLS = 128
TOP_K_IN_GROUP = 2
MOE_TILE = 512
MOE_GRANULE = 16
MOE_REGION = -(-(TOP_K_IN_GROUP * MOE_TILE + N_EXPERTS * (MOE_GRANULE - 1)) // 128) * 128
REGION_GRANULES = MOE_REGION // MOE_GRANULE
MOE_SLOTS = 32
LOG2E = 1.4426950408889634
DENOM_ROWS = 16
V7X_VMEM_LIMIT = 56 * 1024 * 1024
NEG_INF = float("-inf")


def _sigmoid(x):
    return 1.0 / (1.0 + jnp.exp(-x))


def _layer_norm(x, g, b):
    mu = jnp.mean(x, axis=-1, keepdims=True)
    xc = x - mu
    var = jnp.mean(xc * xc, axis=-1, keepdims=True)
    return xc * lax.rsqrt(var + LN_EPS) * g + b


def _lam_value(lq1_ref, lk1_ref, lq2_ref, lk2_ref):
    a = jnp.sum(lq1_ref[...] * lk1_ref[...], axis=1, keepdims=True)
    b = jnp.sum(lq2_ref[...] * lk2_ref[...], axis=1, keepdims=True)
    return jnp.exp(a) - jnp.exp(b) + LAM_INIT


def _window_sums(e):
    outs = []
    for gi, w in enumerate(POOL_WINDOWS):
        s = e[:, gi * POOL_GROUP:(gi + 1) * POOL_GROUP]
        step = 1
        while step < w:
            s = s + pltpu.roll(s, step, axis=0)
            step *= 2
        outs.append(s)
    return outs


def _pool_branch(wins, u, inv_cnts, wg_ref, sp_ref):
    ds = []
    for gi in range(len(POOL_WINDOWS)):
        ds.append(wins[gi] * inv_cnts[gi] - u[:, gi * POOL_GROUP:(gi + 1) * POOL_GROUP])
    d = jnp.concatenate(ds, axis=1).astype(BF16)
    return jnp.dot(d, wg_ref[...], preferred_element_type=F32) * sp_ref[...]


def _store_head_major(ref, x, n_rows):
    for hh in range(N_HEADS):
        ref[pl.ds(hh, n_rows, stride=N_HEADS), :] = x[:, hh * V_DIM:(hh + 1) * V_DIM]


def _proj_prompt_kernel(x_ref, w_ref, b_ref, wg_ref, sp_ref,
                        qt_ref, kf_ref, vf_ref, kb_ref, vt_ref, yp_ref, tail_ref,
                        ext_ref, *, tm, tiles_per_seq):
    t_in_seq = pl.program_id(0) % tiles_per_seq
    z = jnp.dot(x_ref[...].astype(BF16), w_ref[...], preferred_element_type=F32) + b_ref[...]
    u = z[:, 0:D_POOL]
    q = z[:, D_POOL:D_POOL + D_QK]
    k = z[:, D_POOL + D_QK:D_POOL + 2 * D_QK]
    v = z[:, D_POOL + 2 * D_QK:D_QKVU]
    qt_ref[...] = (q * (HEAD_DIM ** -0.5 * LOG2E)).T.astype(BF16)
    vt_ref[...] = v.T.astype(BF16)
    kb_ref[...] = k.astype(BF16)
    _store_head_major(kf_ref, k, tm)
    _store_head_major(vf_ref, v, tm)

    @pl.when(t_in_seq == 0)
    def _():
        ext_ref[0:POOL_HIST, :] = jnp.zeros((POOL_HIST, D_POOL), F32)

    ext_ref[POOL_HIST:POOL_HIST + tm, :] = u
    wins = [s[POOL_HIST:, :] for s in _window_sums(ext_ref[...])]
    pos = t_in_seq * tm + lax.broadcasted_iota(jnp.int32, (tm, 1), 0)
    inv_cnts = [1.0 / jnp.minimum(pos + 1, w).astype(F32) for w in POOL_WINDOWS]
    yp_ref[...] = _pool_branch(wins, u, inv_cnts, wg_ref, sp_ref).astype(BF16)
    tail = u[tm - POOL_HIST:, :]
    ext_ref[0:POOL_HIST, :] = tail
    tail_ref[...] = tail


def _proj_sample_kernel(x_ref, w_ref, b_ref, wg_ref, sp_ref, st_ref,
                        q_ref, kf_ref, vf_ref, yp_ref, tail_ref,
                        ext_ref, *, n_seq, t_new, n_past):
    z = jnp.dot(x_ref[...].astype(BF16), w_ref[...], preferred_element_type=F32) + b_ref[...]
    u = z[:, 0:D_POOL]
    q_ref[...] = z[:, D_POOL:D_POOL + D_QK] * (HEAD_DIM ** -0.5)
    _store_head_major(kf_ref, z[:, D_POOL + D_QK:D_POOL + 2 * D_QK], n_seq * t_new)
    _store_head_major(vf_ref, z[:, D_POOL + 2 * D_QK:D_QKVU], n_seq * t_new)
    rows = POOL_HIST + t_new
    ext_ref[:, 0:POOL_HIST, :] = st_ref[...]
    ext_ref[:, POOL_HIST:rows, :] = u.reshape(n_seq, t_new, D_POOL)
    e3 = ext_ref[...]
    wins = [s.reshape(n_seq, rows, POOL_GROUP)[:, POOL_HIST:, :].reshape(n_seq * t_new, POOL_GROUP)
            for s in _window_sums(e3.reshape(n_seq * rows, D_POOL))]
    inv_cnts = [1.0 / float(min(n_past + 1, w)) for w in POOL_WINDOWS]
    yp_ref[...] = _pool_branch(wins, u, inv_cnts, wg_ref, sp_ref).astype(BF16)
    tail_ref[...] = e3[:, rows - POOL_HIST:, :]


def _attn_prompt_body(slope, lam, qt_ref, k_ref, vt_ref, boff_ref, bdiag_ref, g_ref, o_ref,
                      m_ref, acc_ref, *, tq):
    qi = pl.program_id(2)
    qt = qt_ref[...]
    row = lax.broadcasted_iota(jnp.int32, qt.shape, 0)
    zero = jnp.zeros_like(qt)
    q2t = jnp.concatenate([jnp.where(row < HEAD_DIM, qt, zero), jnp.where(row >= HEAD_DIM, qt, zero)], axis=1)
    ones_rows = jnp.ones((DENOM_ROWS, tq), BF16)
    m_ref[...] = jnp.full(m_ref.shape, NEG_INF, F32)
    acc_ref[...] = jnp.zeros(acc_ref.shape, F32)

    def block(kb, bias_ref, shift):
        k0 = pl.multiple_of(kb * tq, tq)
        kblk = k_ref[pl.ds(k0, tq), :]
        vtblk = vt_ref[:, pl.ds(k0, tq)]
        t = jnp.dot(kblk, q2t, preferred_element_type=F32) + bias_ref[...]
        m_prev = m_ref[...]
        m_new = jnp.maximum(m_prev, jnp.max(t, axis=0, keepdims=True) + shift)
        p = jnp.exp2(t - (m_new - shift)).astype(BF16)
        vt_aug = jnp.concatenate([vtblk, ones_rows], axis=0)
        acc_ref[...] = jnp.exp2(m_prev - m_new) * acc_ref[...] + jnp.dot(vt_aug, p, preferred_element_type=F32)
        m_ref[...] = m_new

    def body(kb, carry):
        block(kb, boff_ref, -(slope * LOG2E) * ((qi - kb) * tq).astype(F32))
        return carry

    lax.fori_loop(0, qi, body, 0)
    block(qi, bdiag_ref, jnp.float32(0.0))

    acc = acc_ref[...]
    on = acc[0:V_DIM, :] / acc[V_DIM:V_DIM + 1, :]
    o = (on[:, 0:tq] - lam * on[:, tq:]).T
    o = o * lax.rsqrt(jnp.mean(o * o, axis=1, keepdims=True) + LN_EPS) * g_ref[...] * (1.0 - LAM_INIT)
    o_ref[...] = o.astype(BF16)


def _attn_sample_body(lam, q_ref, kn_ref, vn_ref, bias_ref, g_ref, k_pages, v_pages, o_ref, *, t_new):
    n_pages = len(k_pages)
    n_past = n_pages * PAGE_SIZE
    q = q_ref[...]
    lane = lax.broadcasted_iota(jnp.int32, (t_new, V_DIM), 1)
    pad = jnp.zeros((PAGE_SIZE - t_new, V_DIM), BF16)
    nt_dims = (((1,), (1,)), ((), ()))

    def head_rows(pages, new_ref, hh):
        past = jnp.concatenate([pg[pl.ds(hh, PAGE_SIZE, stride=N_HEADS), :] for pg in pages], axis=0)
        new = jnp.concatenate([new_ref[pl.ds(hh, t_new, stride=N_HEADS), :].astype(BF16), pad], axis=0)
        return past.astype(BF16), new

    s_rows = []
    for hh in range(N_HEADS):
        qh = q[:, hh * V_DIM:(hh + 1) * V_DIM]
        qh2 = jnp.concatenate([jnp.where(lane < HEAD_DIM, qh, 0.0), jnp.where(lane >= HEAD_DIM, qh, 0.0)],
                              axis=0).astype(BF16)
        k_past, k_new = head_rows(k_pages, kn_ref, hh)
        s_rows.append(jnp.concatenate(
            [lax.dot_general(qh2, k_past, nt_dims, preferred_element_type=F32),
             lax.dot_general(qh2, k_new, nt_dims, preferred_element_type=F32)], axis=1))
    s = jnp.concatenate(s_rows, axis=0) + bias_ref[...]
    m = jnp.max(s, axis=1, keepdims=True)
    p = jnp.exp(s - m)
    p = p / jnp.sum(p, axis=1, keepdims=True)
    outs = []
    for hh in range(N_HEADS):
        r0 = hh * 2 * t_new
        a = (p[r0:r0 + t_new, :] - lam * p[r0 + t_new:r0 + 2 * t_new, :]).astype(BF16)
        v_past, v_new = head_rows(v_pages, vn_ref, hh)
        oh = (jnp.dot(a[:, 0:n_past], v_past, preferred_element_type=F32)
              + jnp.dot(a[:, n_past:], v_new, preferred_element_type=F32))
        oh = oh * lax.rsqrt(jnp.mean(oh * oh, axis=1, keepdims=True) + LN_EPS) * g_ref[...] * (1.0 - LAM_INIT)
        outs.append(oh)
    o_ref[...] = jnp.concatenate(outs, axis=1)


def _attn_kernel(pt_ref, slopes_ref, lq1_ref, lk1_ref, lq2_ref, lk2_ref,
                 qt_ref, k_ref, vt_ref, boff_ref, bdiag_ref, g_ref,
                 qs_ref, kn_ref, vn_ref, bias_s_ref, *rest, n_pages, t_new, tq):
    k_pages = rest[:n_pages]
    v_pages = rest[n_pages:2 * n_pages]
    op_ref, os_ref, m_ref, acc_ref = rest[2 * n_pages:]
    lam = _lam_value(lq1_ref, lk1_ref, lq2_ref, lk2_ref)
    _attn_sample_body(lam, qs_ref, kn_ref, vn_ref, bias_s_ref, g_ref, k_pages, v_pages, os_ref, t_new=t_new)
    _attn_prompt_body(slopes_ref[pl.program_id(1)], lam, qt_ref, k_ref, vt_ref, boff_ref, bdiag_ref, g_ref,
                      op_ref, m_ref, acc_ref, tq=tq)


def _merge_kernel(x_ref, yp_ref, ya_ref, wgate_ref, bgate_ref, wpb_ref, wab_ref, wo_ref,
                  g1_ref, b1_ref, wrt_hi_ref, wrt_lo_ref, brt_ref, upper_ref, lower_ref, *rest, tm):
    x1f_ref, xs_ref, route_ref, cnt_ref = rest[-4:]
    x = x_ref[...]
    zg = jnp.dot(x.astype(BF16), wgate_ref[...], preferred_element_type=F32) + bgate_ref[...]
    gp = _sigmoid(zg[:, 0:D_MODEL])
    ga = _sigmoid(zg[:, D_MODEL:])
    merged = (gp * jnp.dot(yp_ref[...].astype(BF16), wpb_ref[...], preferred_element_type=F32)
              + ga * jnp.dot(ya_ref[...].astype(BF16), wab_ref[...], preferred_element_type=F32))
    x1 = _layer_norm(DEEPNORM_ALPHA * x + jnp.dot(merged.astype(BF16), wo_ref[...], preferred_element_type=F32),
                     g1_ref[...], b1_ref[...])
    x1f_ref[...] = x1
    x1_hi = x1.astype(BF16)
    x1_lo = (x1 - x1_hi.astype(F32)).astype(BF16)
    nt_dims = (((1,), (1,)), ((), ()))
    lg = (lax.dot_general(wrt_hi_ref[...], x1_hi, nt_dims, preferred_element_type=F32)
          + lax.dot_general(wrt_hi_ref[...], x1_lo, nt_dims, preferred_element_type=F32)
          + lax.dot_general(wrt_lo_ref[...], x1_hi, nt_dims, preferred_element_type=F32)) + brt_ref[...]
    row = lax.broadcasted_iota(jnp.int32, lg.shape, 0)
    big = jnp.int32(4 * ROUTER_COLS)
    lgg = jnp.where(row < N_EXPERT_GROUPS, lg, NEG_INF)
    gmax = jnp.max(lgg, axis=0, keepdims=True)
    g_w = 1.0 / jnp.sum(jnp.exp(lgg - gmax), axis=0, keepdims=True)
    g_idx = jnp.min(jnp.where(lgg == gmax, row, big), axis=0, keepdims=True)
    lo_row = N_EXPERT_GROUPS + g_idx * EXPERTS_PER_GROUP
    le = jnp.where((row >= lo_row) & (row < lo_row + EXPERTS_PER_GROUP), lg, NEG_INF)
    v1 = jnp.max(le, axis=0, keepdims=True)
    i1 = jnp.min(jnp.where(le == v1, row, big), axis=0, keepdims=True)
    le2 = jnp.where(row == i1, NEG_INF, le)
    v2 = jnp.max(le2, axis=0, keepdims=True)
    i2 = jnp.min(jnp.where(le2 == v2, row, big), axis=0, keepdims=True)
    e21 = jnp.exp(v2 - v1)
    c1 = g_w / (1.0 + e21)
    c2 = g_w * e21 / (1.0 + e21)
    sel1 = row == i1
    sel2 = row == i2
    member = jnp.where(sel1 | sel2, 1.0, 0.0)
    before = jnp.dot(member.astype(BF16), upper_ref[...], preferred_element_type=F32)
    cnt = jnp.sum(member, axis=1, keepdims=True)
    cnt_pad = jnp.ceil(cnt * (1.0 / MOE_GRANULE)) * MOE_GRANULE
    cnt_b = jnp.broadcast_to(cnt_pad, (ROUTER_COLS, ROUTER_COLS))
    seg_start = jnp.dot(lower_ref[...], cnt_b.astype(BF16), preferred_element_type=F32)
    cnt_ref[...] = cnt_b
    slot = seg_start[:, 0:1] + before
    pos1 = jnp.sum(jnp.where(sel1, slot, 0.0), axis=0, keepdims=True)
    pos2 = jnp.sum(jnp.where(sel2, slot, 0.0), axis=0, keepdims=True)
    srow = lax.broadcasted_iota(jnp.int32, (xs_ref.shape[0], tm), 0).astype(F32)
    perm = jnp.where((srow == pos1) | (srow == pos2), 1.0, 0.0).astype(BF16)
    xs_ref[...] = jnp.dot(perm, x1_hi, preferred_element_type=F32).astype(BF16)
    info = (jnp.where(row == 0, pos1, 0.0) + jnp.where(row == 1, pos2, 0.0)
            + jnp.where(row == 2, c1, 0.0) + jnp.where(row == 3, c2, 0.0))
    route_ref[...] = info.T


def _experts_kernel(gin_ref, sexp_ref, *rest, n_slots):
    x_refs = rest[:n_slots]
    wg_ref, wu_ref, wd_ref, y_ref, wg_b, wu_b, wd_b = rest[n_slots:]
    t = pl.program_id(0)
    expert = sexp_ref[t]
    is_expert = expert < N_EXPERTS

    @pl.when(is_expert & ((t == 0) | (expert != sexp_ref[jnp.maximum(t - 1, 0)])))
    def _():
        wg_b[...] = wg_ref[...].astype(BF16)
        wu_b[...] = wu_ref[...].astype(BF16)
        wd_b[...] = wd_ref[...].astype(BF16)

    @pl.when(is_expert)
    def _():
        half = n_slots // 2
        rows = half * MOE_GRANULE

        def gate_up(refs):
            xg = jnp.concatenate([r[...] for r in refs], axis=0)
            return (jnp.dot(xg, wg_b[...], preferred_element_type=F32),
                    jnp.dot(xg, wu_b[...], preferred_element_type=F32))

        def act_down(gate, up):
            hact = (gate * _sigmoid(gate) * up).astype(BF16)
            return jnp.dot(hact, wd_b[...], preferred_element_type=F32).astype(BF16)

        gate_a, up_a = gate_up(x_refs[:half])
        gate_b, up_b = gate_up(x_refs[half:])
        y_ref[0:rows, :] = act_down(gate_a, up_a)
        y_ref[rows:, :] = act_down(gate_b, up_b)

    @pl.when(jnp.logical_not(is_expert))
    def _():
        y_ref[...] = jnp.zeros(y_ref.shape, BF16)


def _final_kernel(loc_ref, x1f_ref, route_ref, *rest, n_granules):
    g_refs = rest[:n_granules]
    p_ref, g2_ref, b2_ref, wpg_ref, bpg_ref, wple_ref, g3_ref, b3_ref, y_ref = rest[n_granules:]
    ys = jnp.concatenate([r[...] for r in g_refs], axis=0)
    tm = route_ref.shape[0]
    halves = [slice(0, tm // 2), slice(tm // 2, tm)]
    scol = lax.broadcasted_iota(jnp.int32, (tm // 2, ys.shape[0]), 1).astype(F32)

    def combine(rows):
        route = route_ref[rows, :]
        comb = (jnp.where(scol == route[:, 0:1], route[:, 2:3], 0.0)
                + jnp.where(scol == route[:, 1:2], route[:, 3:4], 0.0)).astype(BF16)
        return jnp.dot(comb, ys, preferred_element_type=F32)

    def ple(rows, moe):
        x2 = _layer_norm(DEEPNORM_ALPHA * x1f_ref[rows, :] + moe, g2_ref[...], b2_ref[...])
        z = jnp.dot(x2.astype(BF16), wpg_ref[...], preferred_element_type=F32) + bpg_ref[...]
        pe = jnp.dot(p_ref[rows, :].astype(BF16), wple_ref[...], preferred_element_type=F32)
        return x2, z, pe

    def finish(rows, x2, z, pe):
        y_ref[rows, :] = _layer_norm(DEEPNORM_ALPHA * x2 + _sigmoid(z) * pe, g3_ref[...], b3_ref[...])

    moe_a = combine(halves[0])
    moe_b = combine(halves[1])
    part_a = ple(halves[0], moe_a)
    part_b = ple(halves[1], moe_b)
    finish(halves[0], *part_a)
    finish(halves[1], *part_b)


def _full(shape):
    nd = len(shape)
    return pl.BlockSpec(shape, lambda *_: (0,) * nd)


def _cparams(sem):
    return pltpu.CompilerParams(dimension_semantics=sem, vmem_limit_bytes=V7X_VMEM_LIMIT)


def _proj_prompt(x2d, w_qkvu, b_qkvu, wg_bd, s_pool, n_seq, seq_len, tm=512):
    n = x2d.shape[0]
    tps = seq_len // tm
    row = lambda i: (i, 0)
    col = lambda i: (0, i)
    outs = pl.pallas_call(
        functools.partial(_proj_prompt_kernel, tm=tm, tiles_per_seq=tps),
        grid=(n // tm,),
        in_specs=[pl.BlockSpec((tm, D_MODEL), row), _full(w_qkvu.shape), _full(b_qkvu.shape),
                  _full(wg_bd.shape), _full(s_pool.shape)],
        out_specs=[pl.BlockSpec((D_QK, tm), col), pl.BlockSpec((tm * N_HEADS, V_DIM), row),
                   pl.BlockSpec((tm * N_HEADS, V_DIM), row), pl.BlockSpec((tm, D_QK), row),
                   pl.BlockSpec((D_ATTN, tm), col), pl.BlockSpec((tm, D_POOL), row),
                   pl.BlockSpec((None, POOL_HIST, D_POOL), lambda i: (i // tps, 0, 0))],
        out_shape=[jax.ShapeDtypeStruct((D_QK, n), BF16), jax.ShapeDtypeStruct((n * N_HEADS, V_DIM), F32),
                   jax.ShapeDtypeStruct((n * N_HEADS, V_DIM), F32), jax.ShapeDtypeStruct((n, D_QK), BF16),
                   jax.ShapeDtypeStruct((D_ATTN, n), BF16), jax.ShapeDtypeStruct((n, D_POOL), BF16),
                   jax.ShapeDtypeStruct((n_seq, POOL_HIST, D_POOL), F32)],
        scratch_shapes=[pltpu.VMEM((POOL_HIST + tm, D_POOL), F32)],
        compiler_params=_cparams(("arbitrary",)),
        name="proj_pool_prompt",
    )(x2d, w_qkvu, b_qkvu, wg_bd, s_pool)
    return outs


def _proj_sample(x2d, w_qkvu, b_qkvu, wg_bd, s_pool, state_pad, n_seq, t_new, n_past):
    n = x2d.shape[0]
    rows = POOL_HIST + t_new
    return pl.pallas_call(
        functools.partial(_proj_sample_kernel, n_seq=n_seq, t_new=t_new, n_past=n_past),
        grid=(1,),
        in_specs=[_full(x2d.shape), _full(w_qkvu.shape), _full(b_qkvu.shape), _full(wg_bd.shape),
                  _full(s_pool.shape), _full(state_pad.shape)],
        out_specs=[_full((n, D_QK)), _full((n * N_HEADS, V_DIM)), _full((n * N_HEADS, V_DIM)),
                   _full((n, D_POOL)), _full((n_seq, POOL_HIST, D_POOL))],
        out_shape=[jax.ShapeDtypeStruct((n, D_QK), F32), jax.ShapeDtypeStruct((n * N_HEADS, V_DIM), F32),
                   jax.ShapeDtypeStruct((n * N_HEADS, V_DIM), F32), jax.ShapeDtypeStruct((n, D_POOL), BF16),
                   jax.ShapeDtypeStruct((n_seq, POOL_HIST, D_POOL), F32)],
        scratch_shapes=[pltpu.VMEM((n_seq, rows, D_POOL), F32)],
        compiler_params=_cparams(("arbitrary",)),
        name="proj_pool_sample",
    )(x2d, w_qkvu, b_qkvu, wg_bd, s_pool, state_pad)


def _alibi_slopes():
    return 2.0 ** (-8.0 * jnp.arange(1, N_HEADS + 1, dtype=F32) / N_HEADS)


def _attention(qt, kb, vt, q_s, k_new, v_new, cache_k, cache_v, page_table, lam_vecs, g_sub,
               n_seq, seq_len, t_new, tq=512):
    n = kb.shape[0]
    nq = seq_len // tq
    n_smp, n_pages = page_table.shape
    n_past = n_pages * PAGE_SIZE
    assert n_smp == n_seq * N_HEADS * nq, (n_smp, n_seq, nq)
    slopes = _alibi_slopes()
    rel = (jnp.arange(tq, dtype=jnp.int32)[None, :] - jnp.arange(tq, dtype=jnp.int32)[:, None])
    rel2 = jnp.concatenate([rel, rel], axis=1)
    b_off = -(slopes * LOG2E)[:, None, None] * rel2.astype(F32)[None]
    b_diag = jnp.where(rel2[None] >= 0, b_off, NEG_INF)
    r = jnp.arange(2 * N_HEADS * t_new, dtype=jnp.int32)
    r_head, r_q = r // (2 * t_new), r % t_new
    row_slope = slopes[r_head][:, None]
    kpos = jnp.arange(n_past, dtype=jnp.int32)[None, :]
    b_past = -row_slope * (n_past + r_q[:, None] - kpos).astype(F32)
    j = jnp.arange(PAGE_SIZE, dtype=jnp.int32)[None, :]
    dist_new = r_q[:, None] - j
    b_new = jnp.where((dist_new >= 0) & (j < t_new), -row_slope * dist_new.astype(F32), NEG_INF)
    bias_s = jnp.concatenate([b_past, b_new], axis=1)
    page_rows = PAGE_SIZE * N_HEADS
    ck = cache_k.reshape(-1, V_DIM)
    cv = cache_v.reshape(-1, V_DIM)
    pt = page_table.reshape(-1).astype(jnp.int32)
    step = lambda b, h, i: (b * N_HEADS + h) * nq + i
    const2 = lambda b, h, i, pt_ref: (0, 0)
    vec = pl.BlockSpec((1, HEAD_DIM), const2)
    tok = pl.BlockSpec((t_new, D_QK), lambda b, h, i, pt_ref: (step(b, h, i), 0))
    tok_hm = pl.BlockSpec((t_new * N_HEADS, V_DIM), lambda b, h, i, pt_ref: (step(b, h, i), 0))
    prompt_tile = pl.BlockSpec((tq, V_DIM), lambda b, h, i, pt_ref: (b * nq + i, h))

    def page_spec(pi):
        return pl.BlockSpec((page_rows, V_DIM), lambda b, h, i, pt_ref: (pt_ref[step(b, h, i) * n_pages + pi], 0))

    grid_spec = pltpu.PrefetchScalarGridSpec(
        num_scalar_prefetch=1,
        grid=(n_seq, N_HEADS, nq),
        in_specs=[pl.BlockSpec(memory_space=pltpu.SMEM), vec, vec, vec, vec,
                  pl.BlockSpec((V_DIM, tq), lambda b, h, i, pt_ref: (h, b * nq + i)),
                  pl.BlockSpec((seq_len, V_DIM), lambda b, h, i, pt_ref: (b, h)),
                  pl.BlockSpec((V_DIM, seq_len), lambda b, h, i, pt_ref: (h, b)),
                  pl.BlockSpec((None, tq, 2 * tq), lambda b, h, i, pt_ref: (h, 0, 0)),
                  pl.BlockSpec((None, tq, 2 * tq), lambda b, h, i, pt_ref: (h, 0, 0)),
                  pl.BlockSpec((1, V_DIM), const2),
                  tok, tok_hm, tok_hm, pl.BlockSpec(bias_s.shape, const2)]
                 + [page_spec(pi) for pi in range(n_pages)] * 2,
        out_specs=[prompt_tile, tok],
        scratch_shapes=[pltpu.VMEM((1, 2 * tq), F32), pltpu.VMEM((V_DIM + DENOM_ROWS, 2 * tq), F32)],
    )
    return pl.pallas_call(
        functools.partial(_attn_kernel, n_pages=n_pages, t_new=t_new, tq=tq),
        grid_spec=grid_spec,
        out_shape=[jax.ShapeDtypeStruct((n, D_ATTN), BF16), jax.ShapeDtypeStruct((n_smp * t_new, D_ATTN), F32)],
        compiler_params=_cparams(("arbitrary", "arbitrary", "arbitrary")),
        name="attention",
    )(pt, slopes, *lam_vecs, qt, kb, vt, b_off, b_diag, g_sub, q_s, k_new, v_new, bias_s,
      *([ck] * n_pages), *([cv] * n_pages))


def _merge(x2d, y_pool, y_attn, wts, tile0, n_tiles_total, carried=None):
    tm = MOE_TILE
    n = x2d.shape[0]
    row = lambda i: (i, 0)
    out_row = lambda i: (i + tile0, 0)
    consts = [wts[k] for k in ("w_gate", "b_gate", "w_pool_br", "w_attn_br", "w_o", "ln1_g", "ln1_b",
                               "wrt_hi", "wrt_lo", "b_rt", "upper", "lower")]
    operands = [x2d, y_pool, y_attn, *consts]
    in_specs = [pl.BlockSpec((tm, D_MODEL), row), pl.BlockSpec((tm, D_POOL), row),
                pl.BlockSpec((tm, D_ATTN), row)] + [_full(c.shape) for c in consts]
    aliases = {}
    if carried is not None:
        aliases = {len(operands) + k: k for k in range(len(carried))}
        operands += list(carried)
        in_specs += [pl.BlockSpec(memory_space=pl.ANY)] * len(carried)
    return pl.pallas_call(
        functools.partial(_merge_kernel, tm=tm),
        grid=(n // tm,),
        in_specs=in_specs,
        out_specs=[pl.BlockSpec((tm, D_MODEL), out_row), pl.BlockSpec((MOE_REGION, D_MODEL), out_row),
                   pl.BlockSpec((tm, ROUTER_COLS), out_row), pl.BlockSpec((ROUTER_COLS, ROUTER_COLS), out_row)],
        out_shape=[jax.ShapeDtypeStruct((n_tiles_total * tm, D_MODEL), F32),
                   jax.ShapeDtypeStruct((n_tiles_total * MOE_REGION, D_MODEL), BF16),
                   jax.ShapeDtypeStruct((n_tiles_total * tm, ROUTER_COLS), F32),
                   jax.ShapeDtypeStruct((n_tiles_total * ROUTER_COLS, ROUTER_COLS), F32)],
        input_output_aliases=aliases,
        compiler_params=_cparams(("arbitrary",)),
        name="merge_ln1_route_dispatch",
    )(*operands)


def _granule_schedule(cnt_pad, n_steps):
    n_tiles = cnt_pad.shape[0]
    gc = cnt_pad // MOE_GRANULE
    seg = jnp.cumsum(gc, axis=1) - gc
    tot = jnp.sum(gc, axis=0)
    ahead = jnp.cumsum(gc, axis=0) - gc
    slots_e = ((tot + MOE_SLOTS - 1) // MOE_SLOTS) * MOE_SLOTS
    first_slot = jnp.cumsum(slots_e) - slots_e
    p = jnp.arange(n_steps * MOE_SLOTS, dtype=jnp.int32)[None, :]
    in_e = ((p >= first_slot[:, None]) & (p < (first_slot + slots_e)[:, None])).astype(jnp.int32)
    q = p - jnp.sum(in_e * first_slot[:, None], axis=0, keepdims=True)
    valid = q < jnp.sum(in_e * tot[:, None], axis=0, keepdims=True)
    pick = lambda tab: jnp.sum(tab[:, :, None] * in_e[None, :, :], axis=1)
    ahead_p, gc_p, seg_p = pick(ahead), pick(gc), pick(seg)
    in_t = ((q >= ahead_p) & (q < ahead_p + gc_p)).astype(jnp.int32)
    tile_base = (jnp.arange(n_tiles, dtype=jnp.int32) * REGION_GRANULES)[:, None]
    gid = jnp.sum(in_t * (tile_base + seg_p + q - ahead_p), axis=0, keepdims=True)
    gin = jnp.where(valid, gid, 0).astype(jnp.int32).reshape(-1)
    e_id = jnp.sum(in_e * jnp.arange(N_EXPERTS, dtype=jnp.int32)[:, None], axis=0, keepdims=True)
    step_expert = jnp.where(valid, e_id, N_EXPERTS).reshape(n_steps, MOE_SLOTS)[:, 0].astype(jnp.int32)
    s = jnp.arange(REGION_GRANULES, dtype=jnp.int32)[None, None, :]
    in_g = ((s >= seg[:, :, None]) & (s < (seg + gc)[:, :, None])).astype(jnp.int32)
    loc = jnp.sum(in_g * ((first_slot[None, :] + ahead - seg)[:, :, None] + s), axis=1)
    used = jnp.sum(in_g, axis=1) > 0
    loc = jnp.where(used, loc, loc[:, 0:1]).astype(jnp.int32)
    return gin, step_expert, loc.reshape(-1)


def _experts(xs, gin, step_expert, w_gate, w_up, w_down, n_steps):
    rows = MOE_SLOTS * MOE_GRANULE

    def slot_spec(k):
        return pl.BlockSpec((MOE_GRANULE, D_MODEL), lambda t, gin_ref, se_ref: (gin_ref[t * MOE_SLOTS + k], 0))

    w_sel = lambda t, gin_ref, se_ref: (jnp.minimum(se_ref[t], N_EXPERTS - 1), 0, 0)
    grid_spec = pltpu.PrefetchScalarGridSpec(
        num_scalar_prefetch=2,
        grid=(n_steps,),
        in_specs=[slot_spec(k) for k in range(MOE_SLOTS)]
                 + [pl.BlockSpec((None, D_MODEL, D_EXPERT), w_sel), pl.BlockSpec((None, D_MODEL, D_EXPERT), w_sel),
                    pl.BlockSpec((None, D_EXPERT, D_MODEL), w_sel)],
        out_specs=pl.BlockSpec((rows, D_MODEL), lambda t, gin_ref, se_ref: (t, 0)),
        scratch_shapes=[pltpu.VMEM((D_MODEL, D_EXPERT), BF16), pltpu.VMEM((D_MODEL, D_EXPERT), BF16),
                        pltpu.VMEM((D_EXPERT, D_MODEL), BF16)],
    )
    return pl.pallas_call(
        functools.partial(_experts_kernel, n_slots=MOE_SLOTS),
        grid_spec=grid_spec,
        out_shape=jax.ShapeDtypeStruct((n_steps * rows, D_MODEL), BF16),
        compiler_params=_cparams(("arbitrary",)),
        name="expert_mlps",
    )(gin, step_expert, *([xs] * MOE_SLOTS), w_gate, w_up, w_down)


def _final(x1f, route, ys, loc, p2d, wts, tile0):
    tm = MOE_TILE
    n = p2d.shape[0]
    consts = [wts[k] for k in ("ln2_g", "ln2_b", "w_ple_gate", "b_ple_gate", "w_ple", "ln3_g", "ln3_b")]

    def granule_spec(s):
        return pl.BlockSpec((MOE_GRANULE, D_MODEL),
                            lambda i, loc_ref: (loc_ref[(i + tile0) * REGION_GRANULES + s], 0))

    grid_spec = pltpu.PrefetchScalarGridSpec(
        num_scalar_prefetch=1,
        grid=(n // tm,),
        in_specs=[pl.BlockSpec((tm, D_MODEL), lambda i, loc_ref: (i + tile0, 0)),
                  pl.BlockSpec((tm, ROUTER_COLS), lambda i, loc_ref: (i + tile0, 0))]
                 + [granule_spec(s) for s in range(REGION_GRANULES)]
                 + [pl.BlockSpec((tm, D_PLE), lambda i, loc_ref: (i, 0))]
                 + [pl.BlockSpec(c.shape, lambda i, loc_ref: (0, 0)) for c in consts],
        out_specs=pl.BlockSpec((tm, D_MODEL), lambda i, loc_ref: (i, 0)),
    )
    return pl.pallas_call(
        functools.partial(_final_kernel, n_granules=REGION_GRANULES),
        grid_spec=grid_spec,
        out_shape=jax.ShapeDtypeStruct((n, D_MODEL), F32),
        compiler_params=_cparams(("arbitrary",)),
        name="combine_ln2_ple_ln3",
    )(loc, x1f, route, *([ys] * REGION_GRANULES), p2d, *consts)


def _prepare_weights(w_in, b_in, w_pool_grp, s_pool, w_pool_br, w_attn_br, w_o, ln1_g, ln1_b,
                     w_rg, b_rg, w_re, b_re, w_gate, w_up, w_down, ln2_g, ln2_b,
                     w_ple_gate, b_ple_gate, w_ple, ln3_g, ln3_b):
    i = 0
    row = lambda a: a[i].reshape(1, -1).astype(F32)
    wg_bd = jnp.zeros((D_POOL, D_POOL), F32)
    for gi in range(len(POOL_WINDOWS)):
        sl = slice(gi * POOL_GROUP, (gi + 1) * POOL_GROUP)
        wg_bd = wg_bd.at[sl, sl].set(w_pool_grp[i, gi])
    w_r = jnp.zeros((D_MODEL, ROUTER_COLS), F32)
    w_r = w_r.at[:, 0:N_EXPERT_GROUPS].set(w_rg[i])
    w_r = w_r.at[:, N_EXPERT_GROUPS:N_EXPERT_GROUPS + N_EXPERTS].set(w_re[i].reshape(D_MODEL, N_EXPERTS))
    b_r = jnp.zeros((1, ROUTER_COLS), F32)
    b_r = b_r.at[0, 0:N_EXPERT_GROUPS].set(b_rg[i])
    b_r = b_r.at[0, N_EXPERT_GROUPS:N_EXPERT_GROUPS + N_EXPERTS].set(b_re[i].reshape(-1))
    wrt_hi = w_r.T.astype(BF16)
    wrt_lo = (w_r.T - wrt_hi.astype(F32)).astype(BF16)
    tok = jnp.arange(MOE_TILE, dtype=jnp.int32)
    upper = (tok[:, None] < tok[None, :]).astype(BF16)
    rr = jnp.arange(ROUTER_COLS, dtype=jnp.int32)
    lower = (rr[None, :] < rr[:, None]).astype(BF16)
    return dict(
        w_gate_e=w_gate[i].reshape(N_EXPERTS, D_MODEL, D_EXPERT), w_up_e=w_up[i].reshape(N_EXPERTS, D_MODEL, D_EXPERT),
        w_down_e=w_down[i].reshape(N_EXPERTS, D_EXPERT, D_MODEL),
        wrt_hi=wrt_hi, wrt_lo=wrt_lo, b_rt=b_r.reshape(ROUTER_COLS, 1), upper=upper, lower=lower,
        w_qkvu=w_in[i, :, 0:D_QKVU].astype(BF16), b_qkvu=b_in[i, 0:D_QKVU].reshape(1, -1),
        w_gate=w_in[i, :, D_QKVU:].astype(BF16), b_gate=b_in[i, D_QKVU:].reshape(1, -1),
        wg_bd=wg_bd.astype(BF16), s_pool=row(s_pool),
        w_pool_br=w_pool_br[i].astype(BF16), w_attn_br=w_attn_br[i].astype(BF16), w_o=w_o[i].astype(BF16),
        ln1_g=row(ln1_g), ln1_b=row(ln1_b), ln2_g=row(ln2_g), ln2_b=row(ln2_b),
        w_ple_gate=w_ple_gate[i].astype(BF16), b_ple_gate=row(b_ple_gate), w_ple=w_ple[i].astype(BF16),
        ln3_g=row(ln3_g), ln3_b=row(ln3_b))


def kernel(x_prompt, x_sample, p_prompt, p_sample, cache_k, cache_v, state_pool, page_table, w_in, b_in, lam_q1, lam_k1, lam_q2, lam_k2, g_sub, w_pool_grp, s_pool, w_pool_br, w_attn_br, w_o, ln1_g, ln1_b, w_rg, b_rg, w_re, b_re, w_gate, w_up, w_down, ln2_g, ln2_b, w_ple_gate, b_ple_gate, w_ple, ln3_g, ln3_b):
    assert w_in.shape[0] == DEPTH == 1
    bsz, seq, _ = x_prompt.shape
    dbs, dseq, _ = x_sample.shape
    n_past = page_table.shape[1] * PAGE_SIZE
    wts = _prepare_weights(w_in, b_in, w_pool_grp, s_pool, w_pool_br, w_attn_br, w_o, ln1_g, ln1_b,
                           w_rg, b_rg, w_re, b_re, w_gate, w_up, w_down, ln2_g, ln2_b,
                           w_ple_gate, b_ple_gate, w_ple, ln3_g, ln3_b)
    lam_vecs = [a[0].reshape(1, HEAD_DIM).astype(F32) for a in (lam_q1, lam_k1, lam_q2, lam_k2)]
    g_row = g_sub[0].reshape(1, V_DIM).astype(F32)

    xp = x_prompt.reshape(bsz * seq, D_MODEL)
    qt_p, kf_p, vf_p, kb_p, vt_p, yp_p, tail_p = _proj_prompt(
        xp, wts["w_qkvu"], wts["b_qkvu"], wts["wg_bd"], wts["s_pool"], bsz, seq)

    xs = x_sample.reshape(dbs * dseq, D_MODEL)
    state_pad = jnp.pad(state_pool[0], ((0, 0), (POOL_HIST - POOL_STATE, 0), (0, 0)))
    q_s, kf_s, vf_s, yp_s, tail_s = _proj_sample(
        xs, wts["w_qkvu"], wts["b_qkvu"], wts["wg_bd"], wts["s_pool"], state_pad, dbs, dseq, n_past)

    ya_p, ya_s = _attention(qt_p, kb_p, vt_p, q_s, kf_s, vf_s, cache_k, cache_v, page_table, lam_vecs, g_row,
                            bsz, seq, dseq)

    tiles_p = (bsz * seq) // MOE_TILE
    tiles_s = (dbs * dseq) // MOE_TILE
    n_tiles = tiles_p + tiles_s
    shared = _merge(xp, yp_p, ya_p, wts, 0, n_tiles)
    x1f, xsort, route, cnt = _merge(xs, yp_s, ya_s, wts, tiles_p, n_tiles, carried=shared)
    cnt_pad = cnt[:, 0].reshape(n_tiles, ROUTER_COLS)[:, N_EXPERT_GROUPS:N_EXPERT_GROUPS + N_EXPERTS]
    n_steps = (n_tiles * REGION_GRANULES + N_EXPERTS * (MOE_SLOTS - 1) + MOE_SLOTS - 1) // MOE_SLOTS
    gin, step_expert, loc = _granule_schedule(cnt_pad.astype(jnp.int32), n_steps)
    ysort = _experts(xsort, gin, step_expert, wts["w_gate_e"], wts["w_up_e"], wts["w_down_e"], n_steps)
    y_p = _final(x1f, route, ysort, loc, p_prompt[0].reshape(bsz * seq, D_PLE), wts, 0)
    y_s = _final(x1f, route, ysort, loc, p_sample[0].reshape(dbs * dseq, D_PLE), wts, tiles_p)

    drop = POOL_HIST - POOL_STATE
    return (y_p.reshape(bsz, seq, D_MODEL),
            y_s.reshape(dbs, dseq, D_MODEL),
            kf_p.reshape(1, bsz, seq, N_HEADS, V_DIM),
            vf_p.reshape(1, bsz, seq, N_HEADS, V_DIM),
            tail_p[None, :, drop:, :],
            kf_s.reshape(1, dbs, dseq, N_HEADS, V_DIM),
            vf_s.reshape(1, dbs, dseq, N_HEADS, V_DIM),
            tail_s[None, :, drop:, :])
```

```python
import functools
import math

import jax
import jax.numpy as jnp
from jax import lax
from jax.experimental import pallas as pl
from jax.experimental.pallas import tpu as pltpu

F32 = jnp.float32
BF16 = jnp.bfloat16

D_MODEL = 1024
N_HEADS = 4
HEAD_DIM = 64
V_DIM = 2 * HEAD_DIM
D_QK = N_HEADS * 2 * HEAD_DIM
D_ATTN = N_HEADS * V_DIM
D_POOL = 512
POOL_WINDOWS = (2, 4, 8, 16)
POOL_GROUP = D_POOL // len(POOL_WINDOWS)
POOL_STATE = max(POOL_WINDOWS) - 1
POOL_HIST = POOL_STATE + 1
N_EXPERT_GROUPS = 4
EXPERTS_PER_GROUP = 8
N_EXPERTS = N_EXPERT_GROUPS * EXPERTS_PER_GROUP
D_EXPERT = 256
D_PLE = 256
PAGE_SIZE = 128
LN_EPS = 1e-5
DEPTH = 1
DEEPNORM_ALPHA = (2 * DEPTH) ** 0.25
LAM_INIT = 0.8 - 0.6 * math.exp(-0.3 * 0)
D_QKVU = D_POOL + 2 * D_QK + D_ATTN
ROUTER_COLS = 128
TOP_K_IN_GROUP = 2
MOE_TILE = 512
MOE_GRANULE = 16
MOE_REGION = -(-(TOP_K_IN_GROUP * MOE_TILE + N_EXPERTS * (MOE_GRANULE - 1)) // 128) * 128
REGION_GRANULES = MOE_REGION // MOE_GRANULE
MOE_SLOTS = 32
LOG2E = 1.4426950408889634
DENOM_ROWS = 16
V7X_VMEM_LIMIT = 56 * 1024 * 1024
NEG_INF = float("-inf")


def _sigmoid(x):
    return 1.0 / (1.0 + jnp.exp(-x))


def _layer_norm(x, g, b):
    mu = jnp.mean(x, axis=-1, keepdims=True)
    xc = x - mu
    var = jnp.mean(xc * xc, axis=-1, keepdims=True)
    return xc * lax.rsqrt(var + LN_EPS) * g + b


def _lam_value(lq1_ref, lk1_ref, lq2_ref, lk2_ref):
    a = jnp.sum(lq1_ref[...] * lk1_ref[...], axis=1, keepdims=True)
    b = jnp.sum(lq2_ref[...] * lk2_ref[...], axis=1, keepdims=True)
    return jnp.exp(a) - jnp.exp(b) + LAM_INIT


def _window_sums(e):
    outs = []
    for gi, w in enumerate(POOL_WINDOWS):
        s = e[:, gi * POOL_GROUP:(gi + 1) * POOL_GROUP]
        step = 1
        while step < w:
            s = s + pltpu.roll(s, step, axis=0)
            step *= 2
        outs.append(s)
    return outs


def _pool_branch(wins, u, inv_cnts, wg_ref, sp_ref):
    ds = []
    for gi in range(len(POOL_WINDOWS)):
        ds.append(wins[gi] * inv_cnts[gi] - u[:, gi * POOL_GROUP:(gi + 1) * POOL_GROUP])
    d = jnp.concatenate(ds, axis=1).astype(BF16)
    return jnp.dot(d, wg_ref[...], preferred_element_type=F32) * sp_ref[...]


def _store_head_major(ref, x, n_rows):
    for hh in range(N_HEADS):
        ref[pl.ds(hh, n_rows, stride=N_HEADS), :] = x[:, hh * V_DIM:(hh + 1) * V_DIM]


def _proj_prompt_kernel(x_ref, w_ref, b_ref, wg_ref, sp_ref,
                        qt_ref, kf_ref, vf_ref, kb_ref, vt_ref, yp_ref, tail_ref,
                        ext_ref, *, tm, tiles_per_seq):
    t_in_seq = pl.program_id(0) % tiles_per_seq
    z = jnp.dot(x_ref[...].astype(BF16), w_ref[...], preferred_element_type=F32) + b_ref[...]
    u = z[:, 0:D_POOL]
    q = z[:, D_POOL:D_POOL + D_QK]
    k = z[:, D_POOL + D_QK:D_POOL + 2 * D_QK]
    v = z[:, D_POOL + 2 * D_QK:D_QKVU]
    qt_ref[...] = (q * (HEAD_DIM ** -0.5 * LOG2E)).T.astype(BF16)
    vt_ref[...] = v.T.astype(BF16)
    kb_ref[...] = k.astype(BF16)
    _store_head_major(kf_ref, k, tm)
    _store_head_major(vf_ref, v, tm)

    @pl.when(t_in_seq == 0)
    def _():
        ext_ref[0:POOL_HIST, :] = jnp.zeros((POOL_HIST, D_POOL), F32)

    ext_ref[POOL_HIST:POOL_HIST + tm, :] = u
    wins = [s[POOL_HIST:, :] for s in _window_sums(ext_ref[...])]
    pos = t_in_seq * tm + lax.broadcasted_iota(jnp.int32, (tm, 1), 0)
    inv_cnts = [1.0 / jnp.minimum(pos + 1, w).astype(F32) for w in POOL_WINDOWS]
    yp_ref[...] = _pool_branch(wins, u, inv_cnts, wg_ref, sp_ref).astype(BF16)
    tail = u[tm - POOL_HIST:, :]
    ext_ref[0:POOL_HIST, :] = tail
    tail_ref[...] = tail


def _proj_sample_kernel(x_ref, w_ref, b_ref, wg_ref, sp_ref, st_ref,
                        q_ref, kf_ref, vf_ref, yp_ref, tail_ref,
                        ext_ref, *, n_seq, t_new, n_past):
    z = jnp.dot(x_ref[...].astype(BF16), w_ref[...], preferred_element_type=F32) + b_ref[...]
    u = z[:, 0:D_POOL]
    q_ref[...] = z[:, D_POOL:D_POOL + D_QK] * (HEAD_DIM ** -0.5)
    _store_head_major(kf_ref, z[:, D_POOL + D_QK:D_POOL + 2 * D_QK], n_seq * t_new)
    _store_head_major(vf_ref, z[:, D_POOL + 2 * D_QK:D_QKVU], n_seq * t_new)
    rows = POOL_HIST + t_new
    ext_ref[:, 0:POOL_HIST, :] = st_ref[...]
    ext_ref[:, POOL_HIST:rows, :] = u.reshape(n_seq, t_new, D_POOL)
    e3 = ext_ref[...]
    wins = [s.reshape(n_seq, rows, POOL_GROUP)[:, POOL_HIST:, :].reshape(n_seq * t_new, POOL_GROUP)
            for s in _window_sums(e3.reshape(n_seq * rows, D_POOL))]
    inv_cnts = [1.0 / float(min(n_past + 1, w)) for w in POOL_WINDOWS]
    yp_ref[...] = _pool_branch(wins, u, inv_cnts, wg_ref, sp_ref).astype(BF16)
    tail_ref[...] = e3[:, rows - POOL_HIST:, :]


def _attn_prompt_body(slope, lam, qt_ref, k_ref, vt_ref, boff_ref, bdiag_ref, g_ref, o_ref,
                      m_ref, acc_ref, *, tq):
    qi = pl.program_id(2)
    qt = qt_ref[...]
    row = lax.broadcasted_iota(jnp.int32, qt.shape, 0)
    zero = jnp.zeros_like(qt)
    q2t = jnp.concatenate([jnp.where(row < HEAD_DIM, qt, zero), jnp.where(row >= HEAD_DIM, qt, zero)], axis=1)
    half = tq // 2

    def update(k_start, n_keys, lanes, bias, shift, first):
        kblk = k_ref[pl.ds(k_start, n_keys), :]
        vt_aug = jnp.concatenate([vt_ref[:, pl.ds(k_start, n_keys)], jnp.ones((DENOM_ROWS, n_keys), BF16)], axis=0)
        t = jnp.dot(kblk, q2t[:, lanes], preferred_element_type=F32) + bias
        m_new = jnp.max(t, axis=0, keepdims=True) + shift
        if not first:
            m_prev = m_ref[:, lanes]
            m_new = jnp.maximum(m_prev, m_new)
        pv = jnp.dot(vt_aug, jnp.exp2(t - (m_new - shift)).astype(BF16), preferred_element_type=F32)
        acc_ref[:, lanes] = pv if first else jnp.exp2(m_prev - m_new) * acc_ref[:, lanes] + pv
        m_ref[:, lanes] = m_new

    k_diag = pl.multiple_of(qi * tq, tq)
    zero_shift = jnp.float32(0.0)
    update(k_diag, half, slice(0, 2 * tq), bdiag_ref[0:half, :], zero_shift, True)
    for lanes in (slice(half, tq), slice(tq + half, 2 * tq)):
        update(pl.multiple_of(k_diag + half, half), half, lanes, bdiag_ref[half:tq, lanes], zero_shift, False)

    def body(kb, carry):
        shift = -(slope * LOG2E) * ((qi - kb) * tq).astype(F32)
        update(pl.multiple_of(kb * tq, tq), tq, slice(0, 2 * tq), boff_ref[...], shift, False)
        return carry

    lax.fori_loop(0, qi, body, 0)

    acc = acc_ref[...]
    on = acc[0:V_DIM, :] / acc[V_DIM:V_DIM + 1, :]
    o = (on[:, 0:tq] - lam * on[:, tq:]).T
    o = o * lax.rsqrt(jnp.mean(o * o, axis=1, keepdims=True) + LN_EPS) * g_ref[...] * (1.0 - LAM_INIT)
    o_ref[...] = o.astype(BF16)


def _attn_sample_body(lam, q_ref, kn_ref, vn_ref, bias_ref, g_ref, k_pages, v_pages, o_ref, *, t_new):
    n_pages = len(k_pages)
    n_past = n_pages * PAGE_SIZE
    q = q_ref[...]
    lane = lax.broadcasted_iota(jnp.int32, (t_new, V_DIM), 1)
    pad = jnp.zeros((PAGE_SIZE - t_new, V_DIM), BF16)
    nt_dims = (((1,), (1,)), ((), ()))

    def head_rows(pages, new_ref, hh):
        past = jnp.concatenate([pg[pl.ds(hh, PAGE_SIZE, stride=N_HEADS), :] for pg in pages], axis=0)
        new = jnp.concatenate([new_ref[pl.ds(hh, t_new, stride=N_HEADS), :].astype(BF16), pad], axis=0)
        return past.astype(BF16), new

    s_rows = []
    for hh in range(N_HEADS):
        qh = q[:, hh * V_DIM:(hh + 1) * V_DIM]
        qh2 = jnp.concatenate([jnp.where(lane < HEAD_DIM, qh, 0.0), jnp.where(lane >= HEAD_DIM, qh, 0.0)],
                              axis=0).astype(BF16)
        k_past, k_new = head_rows(k_pages, kn_ref, hh)
        s_rows.append(jnp.concatenate(
            [lax.dot_general(qh2, k_past, nt_dims, preferred_element_type=F32),
             lax.dot_general(qh2, k_new, nt_dims, preferred_element_type=F32)], axis=1))
    s = jnp.concatenate(s_rows, axis=0) + bias_ref[...]
    m = jnp.max(s, axis=1, keepdims=True)
    p = jnp.exp(s - m)
    p = p / jnp.sum(p, axis=1, keepdims=True)
    outs = []
    for h0 in range(0, N_HEADS, 2):
        a2 = jnp.concatenate(
            [p[hh * 2 * t_new:hh * 2 * t_new + t_new, :] - lam * p[hh * 2 * t_new + t_new:(hh + 1) * 2 * t_new, :]
             for hh in (h0, h0 + 1)], axis=0).astype(BF16)
        (vp0, vn0), (vp1, vn1) = head_rows(v_pages, vn_ref, h0), head_rows(v_pages, vn_ref, h0 + 1)
        o2 = (jnp.dot(a2[:, 0:n_past], jnp.concatenate([vp0, vp1], axis=1), preferred_element_type=F32)
              + jnp.dot(a2[:, n_past:], jnp.concatenate([vn0, vn1], axis=1), preferred_element_type=F32))
        for k in range(2):
            oh = o2[k * t_new:(k + 1) * t_new, k * V_DIM:(k + 1) * V_DIM]
            oh = oh * lax.rsqrt(jnp.mean(oh * oh, axis=1, keepdims=True) + LN_EPS) * g_ref[...] * (1.0 - LAM_INIT)
            outs.append(oh)
    o_ref[...] = jnp.concatenate(outs, axis=1)


def _attn_kernel(pt_ref, slopes_ref, lq1_ref, lk1_ref, lq2_ref, lk2_ref,
                 qt_ref, k_ref, vt_ref, boff_ref, bdiag_ref, g_ref,
                 qs_ref, kn_ref, vn_ref, bias_s_ref, *rest, n_pages, t_new, tq):
    k_pages = rest[:n_pages]
    v_pages = rest[n_pages:2 * n_pages]
    op_ref, os_ref, m_ref, acc_ref = rest[2 * n_pages:]
    lam = _lam_value(lq1_ref, lk1_ref, lq2_ref, lk2_ref)
    _attn_sample_body(lam, qs_ref, kn_ref, vn_ref, bias_s_ref, g_ref, k_pages, v_pages, os_ref, t_new=t_new)
    _attn_prompt_body(slopes_ref[pl.program_id(1)], lam, qt_ref, k_ref, vt_ref, boff_ref, bdiag_ref, g_ref,
                      op_ref, m_ref, acc_ref, tq=tq)


def _merge_kernel(x_ref, yp_ref, ya_ref, wgate_ref, bgate_ref, wpb_ref, wab_ref, wo_ref,
                  g1_ref, b1_ref, wrt_hi_ref, wrt_lo_ref, brt_ref, upper_ref, lower_ref, *rest, tm):
    x1f_ref, xs_ref, route_ref, cnt_ref = rest[-4:]
    x = x_ref[...]
    zg = jnp.dot(x.astype(BF16), wgate_ref[...], preferred_element_type=F32) + bgate_ref[...]
    gp = _sigmoid(zg[:, 0:D_MODEL])
    ga = _sigmoid(zg[:, D_MODEL:])
    merged = (gp * jnp.dot(yp_ref[...].astype(BF16), wpb_ref[...], preferred_element_type=F32)
              + ga * jnp.dot(ya_ref[...].astype(BF16), wab_ref[...], preferred_element_type=F32))
    x1 = _layer_norm(DEEPNORM_ALPHA * x + jnp.dot(merged.astype(BF16), wo_ref[...], preferred_element_type=F32),
                     g1_ref[...], b1_ref[...])
    x1f_ref[...] = x1
    x1_hi = x1.astype(BF16)
    x1_lo = (x1 - x1_hi.astype(F32)).astype(BF16)
    nt_dims = (((1,), (1,)), ((), ()))
    lg = (lax.dot_general(wrt_hi_ref[...], x1_hi, nt_dims, preferred_element_type=F32)
          + lax.dot_general(wrt_hi_ref[...], x1_lo, nt_dims, preferred_element_type=F32)
          + lax.dot_general(wrt_lo_ref[...], x1_hi, nt_dims, preferred_element_type=F32)) + brt_ref[...]
    row = lax.broadcasted_iota(jnp.int32, lg.shape, 0)
    big = jnp.int32(4 * ROUTER_COLS)
    lgg = jnp.where(row < N_EXPERT_GROUPS, lg, NEG_INF)
    gmax = jnp.max(lgg, axis=0, keepdims=True)
    g_w = 1.0 / jnp.sum(jnp.exp(lgg - gmax), axis=0, keepdims=True)
    g_idx = jnp.min(jnp.where(lgg == gmax, row, big), axis=0, keepdims=True)
    lo_row = N_EXPERT_GROUPS + g_idx * EXPERTS_PER_GROUP
    le = jnp.where((row >= lo_row) & (row < lo_row + EXPERTS_PER_GROUP), lg, NEG_INF)
    v1 = jnp.max(le, axis=0, keepdims=True)
    i1 = jnp.min(jnp.where(le == v1, row, big), axis=0, keepdims=True)
    le2 = jnp.where(row == i1, NEG_INF, le)
    v2 = jnp.max(le2, axis=0, keepdims=True)
    i2 = jnp.min(jnp.where(le2 == v2, row, big), axis=0, keepdims=True)
    e21 = jnp.exp(v2 - v1)
    c1 = g_w / (1.0 + e21)
    c2 = g_w * e21 / (1.0 + e21)
    sel1 = row == i1
    sel2 = row == i2
    member = jnp.where(sel1 | sel2, 1.0, 0.0)
    before = jnp.dot(member.astype(BF16), upper_ref[...], preferred_element_type=F32)
    cnt = jnp.sum(member, axis=1, keepdims=True)
    cnt_pad = jnp.ceil(cnt * (1.0 / MOE_GRANULE)) * MOE_GRANULE
    cnt_b = jnp.broadcast_to(cnt_pad, (ROUTER_COLS, ROUTER_COLS))
    seg_start = jnp.dot(lower_ref[...], cnt_b.astype(BF16), preferred_element_type=F32)
    cnt_ref[...] = cnt_b
    slot = seg_start[:, 0:1] + before
    pos1 = jnp.sum(jnp.where(sel1, slot, 0.0), axis=0, keepdims=True)
    pos2 = jnp.sum(jnp.where(sel2, slot, 0.0), axis=0, keepdims=True)
    srow = lax.broadcasted_iota(jnp.int32, (xs_ref.shape[0], tm), 0).astype(F32)
    perm = jnp.where((srow == pos1) | (srow == pos2), 1.0, 0.0).astype(BF16)
    xs_ref[...] = jnp.dot(perm, x1_hi, preferred_element_type=F32).astype(BF16)
    info = (jnp.where(row == 0, pos1, 0.0) + jnp.where(row == 1, pos2, 0.0)
            + jnp.where(row == 2, c1, 0.0) + jnp.where(row == 3, c2, 0.0))
    route_ref[...] = info.T


def _experts_kernel(gin_ref, sexp_ref, *rest, n_slots):
    x_refs = rest[:n_slots]
    wg_ref, wu_ref, wd_ref, y_ref, wg_b, wu_b, wd_b = rest[n_slots:]
    t = pl.program_id(0)
    expert = sexp_ref[t]
    is_expert = expert < N_EXPERTS

    @pl.when(is_expert & ((t == 0) | (expert != sexp_ref[jnp.maximum(t - 1, 0)])))
    def _():
        wg_b[...] = wg_ref[...].astype(BF16)
        wu_b[...] = wu_ref[...].astype(BF16)
        wd_b[...] = wd_ref[...].astype(BF16)

    @pl.when(is_expert)
    def _():
        half = n_slots // 2
        rows = half * MOE_GRANULE

        def gate_up(refs):
            xg = jnp.concatenate([r[...] for r in refs], axis=0)
            return (jnp.dot(xg, wg_b[...], preferred_element_type=F32),
                    jnp.dot(xg, wu_b[...], preferred_element_type=F32))

        def act_down(gate, up):
            hact = (gate * _sigmoid(gate) * up).astype(BF16)
            return jnp.dot(hact, wd_b[...], preferred_element_type=F32).astype(BF16)

        gate_a, up_a = gate_up(x_refs[:half])
        gate_b, up_b = gate_up(x_refs[half:])
        y_ref[0:rows, :] = act_down(gate_a, up_a)
        y_ref[rows:, :] = act_down(gate_b, up_b)

    @pl.when(jnp.logical_not(is_expert))
    def _():
        y_ref[...] = jnp.zeros(y_ref.shape, BF16)


def _final_kernel(loc_ref, x1f_ref, route_ref, *rest, n_granules):
    g_refs = rest[:n_granules]
    p_ref, g2_ref, b2_ref, wpg_ref, bpg_ref, wple_ref, g3_ref, b3_ref, y_ref = rest[n_granules:]
    ys = jnp.concatenate([r[...] for r in g_refs], axis=0)
    tm = route_ref.shape[0]
    halves = [slice(0, tm // 2), slice(tm // 2, tm)]
    scol = lax.broadcasted_iota(jnp.int32, (tm // 2, ys.shape[0]), 1).astype(F32)

    def combine(rows):
        route = route_ref[rows, :]
        comb = (jnp.where(scol == route[:, 0:1], route[:, 2:3], 0.0)
                + jnp.where(scol == route[:, 1:2], route[:, 3:4], 0.0)).astype(BF16)
        return jnp.dot(comb, ys, preferred_element_type=F32)

    def ple(rows, moe):
        x2 = _layer_norm(DEEPNORM_ALPHA * x1f_ref[rows, :] + moe, g2_ref[...], b2_ref[...])
        z = jnp.dot(x2.astype(BF16), wpg_ref[...], preferred_element_type=F32) + bpg_ref[...]
        pe = jnp.dot(p_ref[rows, :].astype(BF16), wple_ref[...], preferred_element_type=F32)
        return x2, z, pe

    def finish(rows, x2, z, pe):
        y_ref[rows, :] = _layer_norm(DEEPNORM_ALPHA * x2 + _sigmoid(z) * pe, g3_ref[...], b3_ref[...])

    moe_a = combine(halves[0])
    moe_b = combine(halves[1])
    part_a = ple(halves[0], moe_a)
    part_b = ple(halves[1], moe_b)
    finish(halves[0], *part_a)
    finish(halves[1], *part_b)


def _full(shape):
    nd = len(shape)
    return pl.BlockSpec(shape, lambda *_: (0,) * nd)


def _cparams(sem):
    return pltpu.CompilerParams(dimension_semantics=sem, vmem_limit_bytes=V7X_VMEM_LIMIT)


def _proj_prompt(x2d, w_qkvu, b_qkvu, wg_bd, s_pool, n_seq, seq_len, tm=512):
    n = x2d.shape[0]
    tps = seq_len // tm
    row = lambda i: (i, 0)
    col = lambda i: (0, i)
    outs = pl.pallas_call(
        functools.partial(_proj_prompt_kernel, tm=tm, tiles_per_seq=tps),
        grid=(n // tm,),
        in_specs=[pl.BlockSpec((tm, D_MODEL), row), _full(w_qkvu.shape), _full(b_qkvu.shape),
                  _full(wg_bd.shape), _full(s_pool.shape)],
        out_specs=[pl.BlockSpec((D_QK, tm), col), pl.BlockSpec((tm * N_HEADS, V_DIM), row),
                   pl.BlockSpec((tm * N_HEADS, V_DIM), row), pl.BlockSpec((tm, D_QK), row),
                   pl.BlockSpec((D_ATTN, tm), col), pl.BlockSpec((tm, D_POOL), row),
                   pl.BlockSpec((None, POOL_HIST, D_POOL), lambda i: (i // tps, 0, 0))],
        out_shape=[jax.ShapeDtypeStruct((D_QK, n), BF16), jax.ShapeDtypeStruct((n * N_HEADS, V_DIM), F32),
                   jax.ShapeDtypeStruct((n * N_HEADS, V_DIM), F32), jax.ShapeDtypeStruct((n, D_QK), BF16),
                   jax.ShapeDtypeStruct((D_ATTN, n), BF16), jax.ShapeDtypeStruct((n, D_POOL), BF16),
                   jax.ShapeDtypeStruct((n_seq, POOL_HIST, D_POOL), F32)],
        scratch_shapes=[pltpu.VMEM((POOL_HIST + tm, D_POOL), F32)],
        compiler_params=_cparams(("arbitrary",)),
        name="proj_pool_prompt",
    )(x2d, w_qkvu, b_qkvu, wg_bd, s_pool)
    return outs


def _proj_sample(x2d, w_qkvu, b_qkvu, wg_bd, s_pool, state_pad, n_seq, t_new, n_past):
    n = x2d.shape[0]
    rows = POOL_HIST + t_new
    return pl.pallas_call(
        functools.partial(_proj_sample_kernel, n_seq=n_seq, t_new=t_new, n_past=n_past),
        grid=(1,),
        in_specs=[_full(x2d.shape), _full(w_qkvu.shape), _full(b_qkvu.shape), _full(wg_bd.shape),
                  _full(s_pool.shape), _full(state_pad.shape)],
        out_specs=[_full((n, D_QK)), _full((n * N_HEADS, V_DIM)), _full((n * N_HEADS, V_DIM)),
                   _full((n, D_POOL)), _full((n_seq, POOL_HIST, D_POOL))],
        out_shape=[jax.ShapeDtypeStruct((n, D_QK), F32), jax.ShapeDtypeStruct((n * N_HEADS, V_DIM), F32),
                   jax.ShapeDtypeStruct((n * N_HEADS, V_DIM), F32), jax.ShapeDtypeStruct((n, D_POOL), BF16),
                   jax.ShapeDtypeStruct((n_seq, POOL_HIST, D_POOL), F32)],
        scratch_shapes=[pltpu.VMEM((n_seq, rows, D_POOL), F32)],
        compiler_params=_cparams(("arbitrary",)),
        name="proj_pool_sample",
    )(x2d, w_qkvu, b_qkvu, wg_bd, s_pool, state_pad)


def _alibi_slopes():
    return 2.0 ** (-8.0 * jnp.arange(1, N_HEADS + 1, dtype=F32) / N_HEADS)


def _attention(qt, kb, vt, q_s, k_new, v_new, cache_k, cache_v, page_table, lam_vecs, g_sub,
               n_seq, seq_len, t_new, tq=512):
    n = kb.shape[0]
    nq = seq_len // tq
    n_smp, n_pages = page_table.shape
    n_past = n_pages * PAGE_SIZE
    assert n_smp == n_seq * N_HEADS * nq, (n_smp, n_seq, nq)
    slopes = _alibi_slopes()
    rel = (jnp.arange(tq, dtype=jnp.int32)[None, :] - jnp.arange(tq, dtype=jnp.int32)[:, None])
    rel2 = jnp.concatenate([rel, rel], axis=1)
    b_off = -(slopes * LOG2E)[:, None, None] * rel2.astype(F32)[None]
    b_diag = jnp.where(rel2[None] >= 0, b_off, NEG_INF)
    r = jnp.arange(2 * N_HEADS * t_new, dtype=jnp.int32)
    r_head, r_q = r // (2 * t_new), r % t_new
    row_slope = slopes[r_head][:, None]
    kpos = jnp.arange(n_past, dtype=jnp.int32)[None, :]
    b_past = -row_slope * (n_past + r_q[:, None] - kpos).astype(F32)
    j = jnp.arange(PAGE_SIZE, dtype=jnp.int32)[None, :]
    dist_new = r_q[:, None] - j
    b_new = jnp.where((dist_new >= 0) & (j < t_new), -row_slope * dist_new.astype(F32), NEG_INF)
    bias_s = jnp.concatenate([b_past, b_new], axis=1)
    page_rows = PAGE_SIZE * N_HEADS
    ck = cache_k.reshape(-1, V_DIM)
    cv = cache_v.reshape(-1, V_DIM)
    pt = page_table.reshape(-1).astype(jnp.int32)
    step = lambda b, h, i: (b * N_HEADS + h) * nq + i
    const2 = lambda b, h, i, pt_ref: (0, 0)
    vec = pl.BlockSpec((1, HEAD_DIM), const2)
    tok = pl.BlockSpec((t_new, D_QK), lambda b, h, i, pt_ref: (step(b, h, i), 0))
    tok_hm = pl.BlockSpec((t_new * N_HEADS, V_DIM), lambda b, h, i, pt_ref: (step(b, h, i), 0))
    prompt_tile = pl.BlockSpec((tq, V_DIM), lambda b, h, i, pt_ref: (b * nq + i, h))

    def page_spec(pi):
        return pl.BlockSpec((page_rows, V_DIM), lambda b, h, i, pt_ref: (pt_ref[step(b, h, i) * n_pages + pi], 0))

    grid_spec = pltpu.PrefetchScalarGridSpec(
        num_scalar_prefetch=1,
        grid=(n_seq, N_HEADS, nq),
        in_specs=[pl.BlockSpec(memory_space=pltpu.SMEM), vec, vec, vec, vec,
                  pl.BlockSpec((V_DIM, tq), lambda b, h, i, pt_ref: (h, b * nq + i)),
                  pl.BlockSpec((seq_len, V_DIM), lambda b, h, i, pt_ref: (b, h)),
                  pl.BlockSpec((V_DIM, seq_len), lambda b, h, i, pt_ref: (h, b)),
                  pl.BlockSpec((None, tq, 2 * tq), lambda b, h, i, pt_ref: (h, 0, 0)),
                  pl.BlockSpec((None, tq, 2 * tq), lambda b, h, i, pt_ref: (h, 0, 0)),
                  pl.BlockSpec((1, V_DIM), const2),
                  tok, tok_hm, tok_hm, pl.BlockSpec(bias_s.shape, const2)]
                 + [page_spec(pi) for pi in range(n_pages)] * 2,
        out_specs=[prompt_tile, tok],
        scratch_shapes=[pltpu.VMEM((1, 2 * tq), F32), pltpu.VMEM((V_DIM + DENOM_ROWS, 2 * tq), F32)],
    )
    return pl.pallas_call(
        functools.partial(_attn_kernel, n_pages=n_pages, t_new=t_new, tq=tq),
        grid_spec=grid_spec,
        out_shape=[jax.ShapeDtypeStruct((n, D_ATTN), BF16), jax.ShapeDtypeStruct((n_smp * t_new, D_ATTN), F32)],
        compiler_params=_cparams(("arbitrary", "arbitrary", "arbitrary")),
        name="attention",
    )(pt, slopes, *lam_vecs, qt, kb, vt, b_off, b_diag, g_sub, q_s, k_new, v_new, bias_s,
      *([ck] * n_pages), *([cv] * n_pages))


def _merge(x2d, y_pool, y_attn, wts, tile0, n_tiles_total, carried=None):
    tm = MOE_TILE
    n = x2d.shape[0]
    row = lambda i: (i, 0)
    out_row = lambda i: (i + tile0, 0)
    consts = [wts[k] for k in ("w_gate", "b_gate", "w_pool_br", "w_attn_br", "w_o", "ln1_g", "ln1_b",
                               "wrt_hi", "wrt_lo", "b_rt", "upper", "lower")]
    operands = [x2d, y_pool, y_attn, *consts]
    in_specs = [pl.BlockSpec((tm, D_MODEL), row), pl.BlockSpec((tm, D_POOL), row),
                pl.BlockSpec((tm, D_ATTN), row)] + [_full(c.shape) for c in consts]
    aliases = {}
    if carried is not None:
        aliases = {len(operands) + k: k for k in range(len(carried))}
        operands += list(carried)
        in_specs += [pl.BlockSpec(memory_space=pl.ANY)] * len(carried)
    return pl.pallas_call(
        functools.partial(_merge_kernel, tm=tm),
        grid=(n // tm,),
        in_specs=in_specs,
        out_specs=[pl.BlockSpec((tm, D_MODEL), out_row), pl.BlockSpec((MOE_REGION, D_MODEL), out_row),
                   pl.BlockSpec((tm, ROUTER_COLS), out_row), pl.BlockSpec((ROUTER_COLS, ROUTER_COLS), out_row)],
        out_shape=[jax.ShapeDtypeStruct((n_tiles_total * tm, D_MODEL), F32),
                   jax.ShapeDtypeStruct((n_tiles_total * MOE_REGION, D_MODEL), BF16),
                   jax.ShapeDtypeStruct((n_tiles_total * tm, ROUTER_COLS), F32),
                   jax.ShapeDtypeStruct((n_tiles_total * ROUTER_COLS, ROUTER_COLS), F32)],
        input_output_aliases=aliases,
        compiler_params=_cparams(("arbitrary",)),
        name="merge_ln1_route_dispatch",
    )(*operands)


def _granule_schedule(cnt_pad, n_steps):
    n_tiles = cnt_pad.shape[0]
    gc = cnt_pad // MOE_GRANULE
    seg = jnp.cumsum(gc, axis=1) - gc
    tot = jnp.sum(gc, axis=0)
    ahead = jnp.cumsum(gc, axis=0) - gc
    slots_e = ((tot + MOE_SLOTS - 1) // MOE_SLOTS) * MOE_SLOTS
    first_slot = jnp.cumsum(slots_e) - slots_e
    p0 = (jnp.arange(n_steps, dtype=jnp.int32) * MOE_SLOTS)[None, :]
    in_e = ((p0 >= first_slot[:, None]) & (p0 < (first_slot + slots_e)[:, None])).astype(jnp.int32)
    q0 = p0 - jnp.sum(in_e * first_slot[:, None], axis=0, keepdims=True)
    tot_s = jnp.sum(in_e * tot[:, None], axis=0, keepdims=True)
    e_id = jnp.sum(in_e * jnp.arange(N_EXPERTS, dtype=jnp.int32)[:, None], axis=0)
    step_expert = jnp.where(q0[0] < tot_s[0], e_id, N_EXPERTS).astype(jnp.int32)
    pick = lambda tab: jnp.sum(tab[:, :, None] * in_e[None, :, :], axis=1)[:, :, None]
    ahead_s, gc_s, seg_s = pick(ahead), pick(gc), pick(seg)
    q = q0[0][None, :, None] + jnp.arange(MOE_SLOTS, dtype=jnp.int32)[None, None, :]
    in_t = ((q >= ahead_s) & (q < ahead_s + gc_s)).astype(jnp.int32)
    tile_base = (jnp.arange(n_tiles, dtype=jnp.int32) * REGION_GRANULES)[:, None, None]
    gid = jnp.sum(in_t * (tile_base + seg_s + q - ahead_s), axis=0)
    gin = jnp.where(q[0] < tot_s[0][:, None], gid, 0).astype(jnp.int32).reshape(-1)
    s = jnp.arange(REGION_GRANULES, dtype=jnp.int32)[None, None, :]
    in_g = ((s >= seg[:, :, None]) & (s < (seg + gc)[:, :, None])).astype(jnp.int32)
    loc = jnp.sum(in_g * ((first_slot[None, :] + ahead - seg)[:, :, None] + s), axis=1)
    used = jnp.sum(in_g, axis=1) > 0
    loc = jnp.where(used, loc, loc[:, 0:1]).astype(jnp.int32)
    return gin, step_expert, loc.reshape(-1)


def _experts(xs, gin, step_expert, w_gate, w_up, w_down, n_steps):
    rows = MOE_SLOTS * MOE_GRANULE

    def slot_spec(k):
        return pl.BlockSpec((MOE_GRANULE, D_MODEL), lambda t, gin_ref, se_ref: (gin_ref[t * MOE_SLOTS + k], 0))

    w_sel = lambda t, gin_ref, se_ref: (jnp.minimum(se_ref[t], N_EXPERTS - 1), 0, 0)
    grid_spec = pltpu.PrefetchScalarGridSpec(
        num_scalar_prefetch=2,
        grid=(n_steps,),
        in_specs=[slot_spec(k) for k in range(MOE_SLOTS)]
                 + [pl.BlockSpec((None, D_MODEL, D_EXPERT), w_sel), pl.BlockSpec((None, D_MODEL, D_EXPERT), w_sel),
                    pl.BlockSpec((None, D_EXPERT, D_MODEL), w_sel)],
        out_specs=pl.BlockSpec((rows, D_MODEL), lambda t, gin_ref, se_ref: (t, 0)),
        scratch_shapes=[pltpu.VMEM((D_MODEL, D_EXPERT), BF16), pltpu.VMEM((D_MODEL, D_EXPERT), BF16),
                        pltpu.VMEM((D_EXPERT, D_MODEL), BF16)],
    )
    return pl.pallas_call(
        functools.partial(_experts_kernel, n_slots=MOE_SLOTS),
        grid_spec=grid_spec,
        out_shape=jax.ShapeDtypeStruct((n_steps * rows, D_MODEL), BF16),
        compiler_params=_cparams(("arbitrary",)),
        name="expert_mlps",
    )(gin, step_expert, *([xs] * MOE_SLOTS), w_gate, w_up, w_down)


def _final(x1f, route, ys, loc, p2d, wts, tile0):
    tm = MOE_TILE
    n = p2d.shape[0]
    consts = [wts[k] for k in ("ln2_g", "ln2_b", "w_ple_gate", "b_ple_gate", "w_ple", "ln3_g", "ln3_b")]

    def granule_spec(s):
        return pl.BlockSpec((MOE_GRANULE, D_MODEL),
                            lambda i, loc_ref: (loc_ref[(i + tile0) * REGION_GRANULES + s], 0))

    grid_spec = pltpu.PrefetchScalarGridSpec(
        num_scalar_prefetch=1,
        grid=(n // tm,),
        in_specs=[pl.BlockSpec((tm, D_MODEL), lambda i, loc_ref: (i + tile0, 0)),
                  pl.BlockSpec((tm, ROUTER_COLS), lambda i, loc_ref: (i + tile0, 0))]
                 + [granule_spec(s) for s in range(REGION_GRANULES)]
                 + [pl.BlockSpec((tm, D_PLE), lambda i, loc_ref: (i, 0))]
                 + [pl.BlockSpec(c.shape, lambda i, loc_ref: (0, 0)) for c in consts],
        out_specs=pl.BlockSpec((tm, D_MODEL), lambda i, loc_ref: (i, 0)),
    )
    return pl.pallas_call(
        functools.partial(_final_kernel, n_granules=REGION_GRANULES),
        grid_spec=grid_spec,
        out_shape=jax.ShapeDtypeStruct((n, D_MODEL), F32),
        compiler_params=_cparams(("arbitrary",)),
        name="combine_ln2_ple_ln3",
    )(loc, x1f, route, *([ys] * REGION_GRANULES), p2d, *consts)


def _prepare_weights(w_in, b_in, w_pool_grp, s_pool, w_pool_br, w_attn_br, w_o, ln1_g, ln1_b,
                     w_rg, b_rg, w_re, b_re, w_gate, w_up, w_down, ln2_g, ln2_b,
                     w_ple_gate, b_ple_gate, w_ple, ln3_g, ln3_b):
    i = 0
    row = lambda a: a[i].reshape(1, -1).astype(F32)
    wg_bd = jnp.zeros((D_POOL, D_POOL), F32)
    for gi in range(len(POOL_WINDOWS)):
        sl = slice(gi * POOL_GROUP, (gi + 1) * POOL_GROUP)
        wg_bd = wg_bd.at[sl, sl].set(w_pool_grp[i, gi])
    w_r = jnp.zeros((D_MODEL, ROUTER_COLS), F32)
    w_r = w_r.at[:, 0:N_EXPERT_GROUPS].set(w_rg[i])
    w_r = w_r.at[:, N_EXPERT_GROUPS:N_EXPERT_GROUPS + N_EXPERTS].set(w_re[i].reshape(D_MODEL, N_EXPERTS))
    b_r = jnp.zeros((1, ROUTER_COLS), F32)
    b_r = b_r.at[0, 0:N_EXPERT_GROUPS].set(b_rg[i])
    b_r = b_r.at[0, N_EXPERT_GROUPS:N_EXPERT_GROUPS + N_EXPERTS].set(b_re[i].reshape(-1))
    wrt_hi = w_r.T.astype(BF16)
    wrt_lo = (w_r.T - wrt_hi.astype(F32)).astype(BF16)
    tok = jnp.arange(MOE_TILE, dtype=jnp.int32)
    upper = (tok[:, None] < tok[None, :]).astype(BF16)
    rr = jnp.arange(ROUTER_COLS, dtype=jnp.int32)
    lower = (rr[None, :] < rr[:, None]).astype(BF16)
    return dict(
        w_gate_e=w_gate[i].reshape(N_EXPERTS, D_MODEL, D_EXPERT), w_up_e=w_up[i].reshape(N_EXPERTS, D_MODEL, D_EXPERT),
        w_down_e=w_down[i].reshape(N_EXPERTS, D_EXPERT, D_MODEL),
        wrt_hi=wrt_hi, wrt_lo=wrt_lo, b_rt=b_r.reshape(ROUTER_COLS, 1), upper=upper, lower=lower,
        w_qkvu=w_in[i, :, 0:D_QKVU].astype(BF16), b_qkvu=b_in[i, 0:D_QKVU].reshape(1, -1),
        w_gate=w_in[i, :, D_QKVU:].astype(BF16), b_gate=b_in[i, D_QKVU:].reshape(1, -1),
        wg_bd=wg_bd.astype(BF16), s_pool=row(s_pool),
        w_pool_br=w_pool_br[i].astype(BF16), w_attn_br=w_attn_br[i].astype(BF16), w_o=w_o[i].astype(BF16),
        ln1_g=row(ln1_g), ln1_b=row(ln1_b), ln2_g=row(ln2_g), ln2_b=row(ln2_b),
        w_ple_gate=w_ple_gate[i].astype(BF16), b_ple_gate=row(b_ple_gate), w_ple=w_ple[i].astype(BF16),
        ln3_g=row(ln3_g), ln3_b=row(ln3_b))


def kernel(x_prompt, x_sample, p_prompt, p_sample, cache_k, cache_v, state_pool, page_table, w_in, b_in, lam_q1, lam_k1, lam_q2, lam_k2, g_sub, w_pool_grp, s_pool, w_pool_br, w_attn_br, w_o, ln1_g, ln1_b, w_rg, b_rg, w_re, b_re, w_gate, w_up, w_down, ln2_g, ln2_b, w_ple_gate, b_ple_gate, w_ple, ln3_g, ln3_b):
    assert w_in.shape[0] == DEPTH == 1
    bsz, seq, _ = x_prompt.shape
    dbs, dseq, _ = x_sample.shape
    n_past = page_table.shape[1] * PAGE_SIZE
    wts = _prepare_weights(w_in, b_in, w_pool_grp, s_pool, w_pool_br, w_attn_br, w_o, ln1_g, ln1_b,
                           w_rg, b_rg, w_re, b_re, w_gate, w_up, w_down, ln2_g, ln2_b,
                           w_ple_gate, b_ple_gate, w_ple, ln3_g, ln3_b)
    lam_vecs = [a[0].reshape(1, HEAD_DIM).astype(F32) for a in (lam_q1, lam_k1, lam_q2, lam_k2)]
    g_row = g_sub[0].reshape(1, V_DIM).astype(F32)

    xp = x_prompt.reshape(bsz * seq, D_MODEL)
    qt_p, kf_p, vf_p, kb_p, vt_p, yp_p, tail_p = _proj_prompt(
        xp, wts["w_qkvu"], wts["b_qkvu"], wts["wg_bd"], wts["s_pool"], bsz, seq)

    xs = x_sample.reshape(dbs * dseq, D_MODEL)
    state_pad = jnp.pad(state_pool[0], ((0, 0), (POOL_HIST - POOL_STATE, 0), (0, 0)))
    q_s, kf_s, vf_s, yp_s, tail_s = _proj_sample(
        xs, wts["w_qkvu"], wts["b_qkvu"], wts["wg_bd"], wts["s_pool"], state_pad, dbs, dseq, n_past)

    ya_p, ya_s = _attention(qt_p, kb_p, vt_p, q_s, kf_s, vf_s, cache_k, cache_v, page_table, lam_vecs, g_row,
                            bsz, seq, dseq)

    tiles_p = (bsz * seq) // MOE_TILE
    tiles_s = (dbs * dseq) // MOE_TILE
    n_tiles = tiles_p + tiles_s
    shared = _merge(xp, yp_p, ya_p, wts, 0, n_tiles)
    x1f, xsort, route, cnt = _merge(xs, yp_s, ya_s, wts, tiles_p, n_tiles, carried=shared)
    cnt_pad = cnt[:, 0].reshape(n_tiles, ROUTER_COLS)[:, N_EXPERT_GROUPS:N_EXPERT_GROUPS + N_EXPERTS]
    n_steps = (n_tiles * REGION_GRANULES + N_EXPERTS * (MOE_SLOTS - 1) + MOE_SLOTS - 1) // MOE_SLOTS
    gin, step_expert, loc = _granule_schedule(cnt_pad.astype(jnp.int32), n_steps)
    ysort = _experts(xsort, gin, step_expert, wts["w_gate_e"], wts["w_up_e"], wts["w_down_e"], n_steps)
    y_p = _final(x1f, route, ysort, loc, p_prompt[0].reshape(bsz * seq, D_PLE), wts, 0)
    y_s = _final(x1f, route, ysort, loc, p_sample[0].reshape(dbs * dseq, D_PLE), wts, tiles_p)

    drop = POOL_HIST - POOL_STATE
    return (y_p.reshape(bsz, seq, D_MODEL),
            y_s.reshape(dbs, dseq, D_MODEL),
            kf_p.reshape(1, bsz, seq, N_HEADS, V_DIM),
            vf_p.reshape(1, bsz, seq, N_HEADS, V_DIM),
            tail_p[None, :, drop:, :],
            kf_s.reshape(1, dbs, dseq, N_HEADS, V_DIM),
            vf_s.reshape(1, dbs, dseq, N_HEADS, V_DIM),
            tail_s[None, :, drop:, :])
```

```python
import functools
import math

import jax
import jax.numpy as jnp
from jax import lax
from jax.experimental import pallas as pl
from jax.experimental.pallas import tpu as pltpu

F32 = jnp.float32
BF16 = jnp.bfloat16

D_MODEL = 1024
N_HEADS = 4
HEAD_DIM = 64
V_DIM = 2 * HEAD_DIM
D_QK = N_HEADS * 2 * HEAD_DIM
D_ATTN = N_HEADS * V_DIM
D_POOL = 512
POOL_WINDOWS = (2, 4, 8, 16)
POOL_GROUP = D_POOL // len(POOL_WINDOWS)
POOL_STATE = max(POOL_WINDOWS) - 1
POOL_HIST = POOL_STATE + 1
N_EXPERT_GROUPS = 4
EXPERTS_PER_GROUP = 8
N_EXPERTS = N_EXPERT_GROUPS * EXPERTS_PER_GROUP
D_EXPERT = 256
D_PLE = 256
PAGE_SIZE = 128
LN_EPS = 1e-5
DEPTH = 1
DEEPNORM_ALPHA = (2 * DEPTH) ** 0.25
LAM_INIT = 0.8 - 0.6 * math.exp(-0.3 * 0)
D_QKVU = D_POOL + 2 * D_QK + D_ATTN
ROUTER_COLS = 128
TOP_K_IN_GROUP = 2
MOE_TILE = 512
MOE_GRANULE = 16
MOE_REGION = -(-(TOP_K_IN_GROUP * MOE_TILE + N_EXPERTS * (MOE_GRANULE - 1)) // 128) * 128
REGION_GRANULES = MOE_REGION // MOE_GRANULE
MOE_SLOTS = 32
LOG2E = 1.4426950408889634
DENOM_ROWS = 16
SAMPLES_PER_STEP = 2
SCORES_AHEAD = 2
V7X_VMEM_LIMIT = 56 * 1024 * 1024
NEG_INF = float("-inf")


def _sigmoid(x):
    return 1.0 / (1.0 + jnp.exp(-x))


def _layer_norm(x, g, b):
    mu = jnp.mean(x, axis=-1, keepdims=True)
    xc = x - mu
    var = jnp.mean(xc * xc, axis=-1, keepdims=True)
    return xc * lax.rsqrt(var + LN_EPS) * g + b


def _lam_value(lq1_ref, lk1_ref, lq2_ref, lk2_ref):
    a = jnp.sum(lq1_ref[...] * lk1_ref[...], axis=1, keepdims=True)
    b = jnp.sum(lq2_ref[...] * lk2_ref[...], axis=1, keepdims=True)
    return jnp.exp(a) - jnp.exp(b) + LAM_INIT


def _window_sums(e):
    outs = []
    for gi, w in enumerate(POOL_WINDOWS):
        s = e[:, gi * POOL_GROUP:(gi + 1) * POOL_GROUP]
        step = 1
        while step < w:
            s = s + pltpu.roll(s, step, axis=0)
            step *= 2
        outs.append(s)
    return outs


def _pool_branch(wins, u, inv_cnts, wg_ref, sp_ref):
    ds = []
    for gi in range(len(POOL_WINDOWS)):
        ds.append(wins[gi] * inv_cnts[gi] - u[:, gi * POOL_GROUP:(gi + 1) * POOL_GROUP])
    d = jnp.concatenate(ds, axis=1).astype(BF16)
    return jnp.dot(d, wg_ref[...], preferred_element_type=F32) * sp_ref[...]


def _store_head_major(ref, x, n_rows):
    for hh in range(N_HEADS):
        ref[pl.ds(hh, n_rows, stride=N_HEADS), :] = x[:, hh * V_DIM:(hh + 1) * V_DIM]


def _proj_prompt_kernel(x_ref, w_ref, b_ref, wg_ref, sp_ref,
                        qt_ref, kf_ref, vf_ref, kb_ref, vt_ref, yp_ref, tail_ref,
                        ext_ref, *, tm, tiles_per_seq):
    t_in_seq = pl.program_id(0) % tiles_per_seq
    z = jnp.dot(x_ref[...].astype(BF16), w_ref[...], preferred_element_type=F32) + b_ref[...]
    u = z[:, 0:D_POOL]
    q = z[:, D_POOL:D_POOL + D_QK]
    k = z[:, D_POOL + D_QK:D_POOL + 2 * D_QK]
    v = z[:, D_POOL + 2 * D_QK:D_QKVU]
    qt_ref[...] = (q * (HEAD_DIM ** -0.5 * LOG2E)).T.astype(BF16)
    vt_ref[...] = v.T.astype(BF16)
    kb_ref[...] = k.astype(BF16)
    _store_head_major(kf_ref, k, tm)
    _store_head_major(vf_ref, v, tm)

    @pl.when(t_in_seq == 0)
    def _():
        ext_ref[0:POOL_HIST, :] = jnp.zeros((POOL_HIST, D_POOL), F32)

    ext_ref[POOL_HIST:POOL_HIST + tm, :] = u
    wins = [s[POOL_HIST:, :] for s in _window_sums(ext_ref[...])]
    pos = t_in_seq * tm + lax.broadcasted_iota(jnp.int32, (tm, 1), 0)
    inv_cnts = [1.0 / jnp.minimum(pos + 1, w).astype(F32) for w in POOL_WINDOWS]
    yp_ref[...] = _pool_branch(wins, u, inv_cnts, wg_ref, sp_ref).astype(BF16)
    tail = u[tm - POOL_HIST:, :]
    ext_ref[0:POOL_HIST, :] = tail
    tail_ref[...] = tail


def _proj_sample_kernel(x_ref, w_ref, b_ref, wg_ref, sp_ref, st_ref,
                        q_ref, kf_ref, vf_ref, yp_ref, tail_ref,
                        ext_ref, *, n_seq, t_new, n_past):
    z = jnp.dot(x_ref[...].astype(BF16), w_ref[...], preferred_element_type=F32) + b_ref[...]
    u = z[:, 0:D_POOL]
    q_ref[...] = z[:, D_POOL:D_POOL + D_QK] * (HEAD_DIM ** -0.5)
    _store_head_major(kf_ref, z[:, D_POOL + D_QK:D_POOL + 2 * D_QK], n_seq * t_new)
    _store_head_major(vf_ref, z[:, D_POOL + 2 * D_QK:D_QKVU], n_seq * t_new)
    rows = POOL_HIST + t_new
    ext_ref[:, 0:POOL_HIST, :] = st_ref[...]
    ext_ref[:, POOL_HIST:rows, :] = u.reshape(n_seq, t_new, D_POOL)
    e3 = ext_ref[...]
    wins = [s.reshape(n_seq, rows, POOL_GROUP)[:, POOL_HIST:, :].reshape(n_seq * t_new, POOL_GROUP)
            for s in _window_sums(e3.reshape(n_seq * rows, D_POOL))]
    inv_cnts = [1.0 / float(min(n_past + 1, w)) for w in POOL_WINDOWS]
    yp_ref[...] = _pool_branch(wins, u, inv_cnts, wg_ref, sp_ref).astype(BF16)
    tail_ref[...] = e3[:, rows - POOL_HIST:, :]


def _attn_prompt_pair(slope, lam, qt_refs, k_ref, vt_ref, boff_ref, bdiag_ref, g_ref, o_ref,
                      q2t_ref, m_ref, acc_ref, *, tq, nq):
    pair = pl.program_id(2)
    tiles = (nq - 1 - pair, pair)
    half = tq // 2
    all_lanes = slice(0, 2 * tq)
    late_lanes = (slice(half, tq), slice(tq + half, 2 * tq))
    for slot, qt_ref in enumerate(qt_refs):
        qt = qt_ref[...]
        row = lax.broadcasted_iota(jnp.int32, qt.shape, 0)
        zero = jnp.zeros_like(qt)
        q2t_ref[slot] = jnp.concatenate([jnp.where(row < HEAD_DIM, qt, zero), jnp.where(row >= HEAD_DIM, qt, zero)],
                                        axis=1)

    def both_maps(bias):
        return jnp.concatenate([bias, bias], axis=1)

    def scores(slot, k_start, n_keys, lanes, bias):
        kblk = k_ref[pl.ds(k_start, n_keys), :]
        return jnp.dot(kblk, q2t_ref[slot, :, lanes], preferred_element_type=F32) + bias

    def absorb(t, slot, k_start, n_keys, lanes, shift, first):
        vt_aug = jnp.concatenate([vt_ref[:, pl.ds(k_start, n_keys)], jnp.ones((DENOM_ROWS, n_keys), BF16)], axis=0)
        m_new = jnp.max(t, axis=0, keepdims=True) + shift
        if not first:
            m_prev = m_ref[slot, :, lanes]
            m_new = jnp.maximum(m_prev, m_new)
        pv = jnp.dot(vt_aug, jnp.exp2(t - (m_new - shift)).astype(BF16), preferred_element_type=F32)
        acc_ref[slot, :, lanes] = pv if first else jnp.exp2(m_prev - m_new) * acc_ref[slot, :, lanes] + pv
        m_ref[slot, :, lanes] = m_new

    zero_shift = jnp.float32(0.0)
    tasks = []
    for slot in range(2):
        k_diag = pl.multiple_of(tiles[slot] * tq, tq)
        k_diag2 = pl.multiple_of(k_diag + half, half)
        tasks.append((slot, k_diag, half, all_lanes, lambda: both_maps(bdiag_ref[0:half, :]), zero_shift, True))
        for lanes in late_lanes:
            tasks.append((slot, k_diag2, half, lanes, lambda: bdiag_ref[half:tq, half:tq], zero_shift, False))
    for j in range(nq - 1):
        in_second = j >= tiles[0]
        kb = jnp.where(in_second, j - tiles[0], j)
        q_tile = jnp.where(in_second, tiles[1], tiles[0])
        shift = -(slope * LOG2E) * ((q_tile - kb) * tq).astype(F32)
        tasks.append((in_second.astype(jnp.int32), pl.multiple_of(kb * tq, tq), tq, all_lanes,
                      lambda: both_maps(boff_ref[...]), shift, False))
    queue = []
    for slot, k_start, n_keys, lanes, bias_fn, shift, first in tasks:
        queue.append((scores(slot, k_start, n_keys, lanes, bias_fn()), slot, k_start, n_keys, lanes, shift, first))
        if len(queue) > SCORES_AHEAD:
            absorb(*queue.pop(0))
    while queue:
        absorb(*queue.pop(0))

    for slot in range(2):
        acc = acc_ref[slot]
        on = acc[0:V_DIM, :] / acc[V_DIM:V_DIM + 1, :]
        o = (on[:, 0:tq] - lam * on[:, tq:]).T
        o = o * lax.rsqrt(jnp.mean(o * o, axis=1, keepdims=True) + LN_EPS) * g_ref[...] * (1.0 - LAM_INIT)
        o_ref[slot * tq:(slot + 1) * tq, :] = o.astype(BF16)


def _attn_sample_body(lam, sj, q_ref, kn_ref, vn_ref, bias_ref, g_ref, k_pages, v_pages, o_ref, *, t_new):
    n_pages = len(k_pages)
    n_past = n_pages * PAGE_SIZE
    rows = slice(sj * t_new, (sj + 1) * t_new)
    q = q_ref[rows, :]
    lane = lax.broadcasted_iota(jnp.int32, (t_new, V_DIM), 1)
    pad = jnp.zeros((PAGE_SIZE - t_new, V_DIM), BF16)
    nt_dims = (((1,), (1,)), ((), ()))

    def head_rows(pages, new_ref, hh):
        past = jnp.concatenate([pg[pl.ds(hh, PAGE_SIZE, stride=N_HEADS), :] for pg in pages], axis=0)
        new = new_ref[pl.ds(sj * t_new * N_HEADS + hh, t_new, stride=N_HEADS), :]
        new = jnp.concatenate([new.astype(BF16), pad], axis=0)
        return past.astype(BF16), new

    s_rows = []
    for hh in range(N_HEADS):
        qh = q[:, hh * V_DIM:(hh + 1) * V_DIM]
        qh2 = jnp.concatenate([jnp.where(lane < HEAD_DIM, qh, 0.0), jnp.where(lane >= HEAD_DIM, qh, 0.0)],
                              axis=0).astype(BF16)
        k_past, k_new = head_rows(k_pages, kn_ref, hh)
        s_rows.append(jnp.concatenate(
            [lax.dot_general(qh2, k_past, nt_dims, preferred_element_type=F32),
             lax.dot_general(qh2, k_new, nt_dims, preferred_element_type=F32)], axis=1))
    s = jnp.concatenate(s_rows, axis=0) + bias_ref[...]
    m = jnp.max(s, axis=1, keepdims=True)
    p = jnp.exp(s - m)
    p = p / jnp.sum(p, axis=1, keepdims=True)
    outs = []
    for h0 in range(0, N_HEADS, 2):
        a2 = jnp.concatenate(
            [p[hh * 2 * t_new:hh * 2 * t_new + t_new, :] - lam * p[hh * 2 * t_new + t_new:(hh + 1) * 2 * t_new, :]
             for hh in (h0, h0 + 1)], axis=0).astype(BF16)
        (vp0, vn0), (vp1, vn1) = head_rows(v_pages, vn_ref, h0), head_rows(v_pages, vn_ref, h0 + 1)
        o2 = (jnp.dot(a2[:, 0:n_past], jnp.concatenate([vp0, vp1], axis=1), preferred_element_type=F32)
              + jnp.dot(a2[:, n_past:], jnp.concatenate([vn0, vn1], axis=1), preferred_element_type=F32))
        for k in range(2):
            oh = o2[k * t_new:(k + 1) * t_new, k * V_DIM:(k + 1) * V_DIM]
            oh = oh * lax.rsqrt(jnp.mean(oh * oh, axis=1, keepdims=True) + LN_EPS) * g_ref[...] * (1.0 - LAM_INIT)
            outs.append(oh)
    o_ref[rows, :] = jnp.concatenate(outs, axis=1)


def _attn_kernel(pt_ref, slopes_ref, lq1_ref, lk1_ref, lq2_ref, lk2_ref,
                 qta_ref, qtb_ref, k_ref, vt_ref, boff_ref, bdiag_ref, g_ref, bias_s_ref, *rest,
                 n_pages, t_new, tq, nq):
    qs_ref, kn_ref, vn_ref = rest[:3]
    pages = rest[3:3 + SAMPLES_PER_STEP * 2 * n_pages]
    op_ref, os_ref, q2t_ref, m_ref, acc_ref = rest[3 + SAMPLES_PER_STEP * 2 * n_pages:]
    lam = _lam_value(lq1_ref, lk1_ref, lq2_ref, lk2_ref)
    for sj in range(SAMPLES_PER_STEP):
        pg = pages[sj * 2 * n_pages:(sj + 1) * 2 * n_pages]
        _attn_sample_body(lam, sj, qs_ref, kn_ref, vn_ref, bias_s_ref, g_ref, pg[:n_pages], pg[n_pages:], os_ref,
                          t_new=t_new)
    _attn_prompt_pair(slopes_ref[pl.program_id(1)], lam, (qta_ref, qtb_ref), k_ref, vt_ref, boff_ref, bdiag_ref,
                      g_ref, op_ref, q2t_ref, m_ref, acc_ref, tq=tq, nq=nq)


def _merge_kernel(x_ref, yp_ref, ya_ref, wgate_ref, bgate_ref, wpb_ref, wab_ref, wo_ref,
                  g1_ref, b1_ref, wrt_hi_ref, wrt_lo_ref, brt_ref, upper_ref, lower_ref, *rest, tm):
    x1f_ref, xs_ref, route_ref, cnt_ref = rest[-4:]
    x = x_ref[...]
    zg = jnp.dot(x.astype(BF16), wgate_ref[...], preferred_element_type=F32) + bgate_ref[...]
    gp = _sigmoid(zg[:, 0:D_MODEL])
    ga = _sigmoid(zg[:, D_MODEL:])
    merged = (gp * jnp.dot(yp_ref[...].astype(BF16), wpb_ref[...], preferred_element_type=F32)
              + ga * jnp.dot(ya_ref[...].astype(BF16), wab_ref[...], preferred_element_type=F32))
    x1 = _layer_norm(DEEPNORM_ALPHA * x + jnp.dot(merged.astype(BF16), wo_ref[...], preferred_element_type=F32),
                     g1_ref[...], b1_ref[...])
    x1f_ref[...] = x1
    x1_hi = x1.astype(BF16)
    x1_lo = (x1 - x1_hi.astype(F32)).astype(BF16)
    nt_dims = (((1,), (1,)), ((), ()))
    lg = (lax.dot_general(wrt_hi_ref[...], x1_hi, nt_dims, preferred_element_type=F32)
          + lax.dot_general(wrt_hi_ref[...], x1_lo, nt_dims, preferred_element_type=F32)
          + lax.dot_general(wrt_lo_ref[...], x1_hi, nt_dims, preferred_element_type=F32)) + brt_ref[...]
    row = lax.broadcasted_iota(jnp.int32, lg.shape, 0)
    big = jnp.int32(4 * ROUTER_COLS)
    lgg = jnp.where(row < N_EXPERT_GROUPS, lg, NEG_INF)
    gmax = jnp.max(lgg, axis=0, keepdims=True)
    g_w = 1.0 / jnp.sum(jnp.exp(lgg - gmax), axis=0, keepdims=True)
    g_idx = jnp.min(jnp.where(lgg == gmax, row, big), axis=0, keepdims=True)
    lo_row = N_EXPERT_GROUPS + g_idx * EXPERTS_PER_GROUP
    le = jnp.where((row >= lo_row) & (row < lo_row + EXPERTS_PER_GROUP), lg, NEG_INF)
    v1 = jnp.max(le, axis=0, keepdims=True)
    i1 = jnp.min(jnp.where(le == v1, row, big), axis=0, keepdims=True)
    le2 = jnp.where(row == i1, NEG_INF, le)
    v2 = jnp.max(le2, axis=0, keepdims=True)
    i2 = jnp.min(jnp.where(le2 == v2, row, big), axis=0, keepdims=True)
    e21 = jnp.exp(v2 - v1)
    c1 = g_w / (1.0 + e21)
    c2 = g_w * e21 / (1.0 + e21)
    sel1 = row == i1
    sel2 = row == i2
    member = jnp.where(sel1 | sel2, 1.0, 0.0)
    before = jnp.dot(member.astype(BF16), upper_ref[...], preferred_element_type=F32)
    cnt = jnp.sum(member, axis=1, keepdims=True)
    cnt_pad = jnp.ceil(cnt * (1.0 / MOE_GRANULE)) * MOE_GRANULE
    cnt_b = jnp.broadcast_to(cnt_pad, (ROUTER_COLS, ROUTER_COLS))
    seg_start = jnp.dot(lower_ref[...], cnt_b.astype(BF16), preferred_element_type=F32)
    cnt_ref[...] = cnt_b
    slot = seg_start[:, 0:1] + before
    pos1 = jnp.sum(jnp.where(sel1, slot, 0.0), axis=0, keepdims=True)
    pos2 = jnp.sum(jnp.where(sel2, slot, 0.0), axis=0, keepdims=True)
    srow = lax.broadcasted_iota(jnp.int32, (xs_ref.shape[0], tm), 0).astype(F32)
    perm = jnp.where((srow == pos1) | (srow == pos2), 1.0, 0.0).astype(BF16)
    xs_ref[...] = jnp.dot(perm, x1_hi, preferred_element_type=F32).astype(BF16)
    info = (jnp.where(row == 0, pos1, 0.0) + jnp.where(row == 1, pos2, 0.0)
            + jnp.where(row == 2, c1, 0.0) + jnp.where(row == 3, c2, 0.0))
    route_ref[...] = info.T


def _experts_kernel(gin_ref, sexp_ref, *rest, n_slots):
    x_refs = rest[:n_slots]
    wg_ref, wu_ref, wd_ref, y_ref, wg_b, wu_b, wd_b = rest[n_slots:]
    t = pl.program_id(0)
    expert = sexp_ref[t]
    is_expert = expert < N_EXPERTS

    @pl.when(is_expert & ((t == 0) | (expert != sexp_ref[jnp.maximum(t - 1, 0)])))
    def _():
        wg_b[...] = wg_ref[...].astype(BF16)
        wu_b[...] = wu_ref[...].astype(BF16)
        wd_b[...] = wd_ref[...].astype(BF16)

    @pl.when(is_expert)
    def _():
        half = n_slots // 2
        rows = half * MOE_GRANULE

        def gate_up(refs):
            xg = jnp.concatenate([r[...] for r in refs], axis=0)
            return (jnp.dot(xg, wg_b[...], preferred_element_type=F32),
                    jnp.dot(xg, wu_b[...], preferred_element_type=F32))

        def act_down(gate, up):
            hact = (gate * _sigmoid(gate) * up).astype(BF16)
            return jnp.dot(hact, wd_b[...], preferred_element_type=F32).astype(BF16)

        gate_a, up_a = gate_up(x_refs[:half])
        gate_b, up_b = gate_up(x_refs[half:])
        y_ref[0:rows, :] = act_down(gate_a, up_a)
        y_ref[rows:, :] = act_down(gate_b, up_b)

    @pl.when(jnp.logical_not(is_expert))
    def _():
        y_ref[...] = jnp.zeros(y_ref.shape, BF16)


def _final_kernel(loc_ref, x1f_ref, route_ref, *rest, n_granules):
    g_refs = rest[:n_granules]
    p_ref, g2_ref, b2_ref, wpg_ref, bpg_ref, wple_ref, g3_ref, b3_ref, y_ref = rest[n_granules:]
    ys = jnp.concatenate([r[...] for r in g_refs], axis=0)
    tm = route_ref.shape[0]
    halves = [slice(0, tm // 2), slice(tm // 2, tm)]
    scol = lax.broadcasted_iota(jnp.int32, (tm // 2, ys.shape[0]), 1).astype(F32)

    def combine(rows):
        route = route_ref[rows, :]
        comb = (jnp.where(scol == route[:, 0:1], route[:, 2:3], 0.0)
                + jnp.where(scol == route[:, 1:2], route[:, 3:4], 0.0)).astype(BF16)
        return jnp.dot(comb, ys, preferred_element_type=F32)

    def ple(rows, moe):
        x2 = _layer_norm(DEEPNORM_ALPHA * x1f_ref[rows, :] + moe, g2_ref[...], b2_ref[...])
        z = jnp.dot(x2.astype(BF16), wpg_ref[...], preferred_element_type=F32) + bpg_ref[...]
        pe = jnp.dot(p_ref[rows, :].astype(BF16), wple_ref[...], preferred_element_type=F32)
        return x2, z, pe

    def finish(rows, x2, z, pe):
        y_ref[rows, :] = _layer_norm(DEEPNORM_ALPHA * x2 + _sigmoid(z) * pe, g3_ref[...], b3_ref[...])

    moe_a = combine(halves[0])
    moe_b = combine(halves[1])
    part_a = ple(halves[0], moe_a)
    part_b = ple(halves[1], moe_b)
    finish(halves[0], *part_a)
    finish(halves[1], *part_b)


def _full(shape):
    nd = len(shape)
    return pl.BlockSpec(shape, lambda *_: (0,) * nd)


def _cparams(sem):
    return pltpu.CompilerParams(dimension_semantics=sem, vmem_limit_bytes=V7X_VMEM_LIMIT)


def _proj_prompt(x2d, w_qkvu, b_qkvu, wg_bd, s_pool, n_seq, seq_len, tm=512):
    n = x2d.shape[0]
    tps = seq_len // tm
    row = lambda i: (i, 0)
    col = lambda i: (0, i)
    outs = pl.pallas_call(
        functools.partial(_proj_prompt_kernel, tm=tm, tiles_per_seq=tps),
        grid=(n // tm,),
        in_specs=[pl.BlockSpec((tm, D_MODEL), row), _full(w_qkvu.shape), _full(b_qkvu.shape),
                  _full(wg_bd.shape), _full(s_pool.shape)],
        out_specs=[pl.BlockSpec((D_QK, tm), col), pl.BlockSpec((tm * N_HEADS, V_DIM), row),
                   pl.BlockSpec((tm * N_HEADS, V_DIM), row), pl.BlockSpec((tm, D_QK), row),
                   pl.BlockSpec((D_ATTN, tm), col), pl.BlockSpec((tm, D_POOL), row),
                   pl.BlockSpec((None, POOL_HIST, D_POOL), lambda i: (i // tps, 0, 0))],
        out_shape=[jax.ShapeDtypeStruct((D_QK, n), BF16), jax.ShapeDtypeStruct((n * N_HEADS, V_DIM), F32),
                   jax.ShapeDtypeStruct((n * N_HEADS, V_DIM), F32), jax.ShapeDtypeStruct((n, D_QK), BF16),
                   jax.ShapeDtypeStruct((D_ATTN, n), BF16), jax.ShapeDtypeStruct((n, D_POOL), BF16),
                   jax.ShapeDtypeStruct((n_seq, POOL_HIST, D_POOL), F32)],
        scratch_shapes=[pltpu.VMEM((POOL_HIST + tm, D_POOL), F32)],
        compiler_params=_cparams(("arbitrary",)),
        name="proj_pool_prompt",
    )(x2d, w_qkvu, b_qkvu, wg_bd, s_pool)
    return outs


def _proj_sample(x2d, w_qkvu, b_qkvu, wg_bd, s_pool, state_pad, n_seq, t_new, n_past):
    n = x2d.shape[0]
    rows = POOL_HIST + t_new
    return pl.pallas_call(
        functools.partial(_proj_sample_kernel, n_seq=n_seq, t_new=t_new, n_past=n_past),
        grid=(1,),
        in_specs=[_full(x2d.shape), _full(w_qkvu.shape), _full(b_qkvu.shape), _full(wg_bd.shape),
                  _full(s_pool.shape), _full(state_pad.shape)],
        out_specs=[_full((n, D_QK)), _full((n * N_HEADS, V_DIM)), _full((n * N_HEADS, V_DIM)),
                   _full((n, D_POOL)), _full((n_seq, POOL_HIST, D_POOL))],
        out_shape=[jax.ShapeDtypeStruct((n, D_QK), F32), jax.ShapeDtypeStruct((n * N_HEADS, V_DIM), F32),
                   jax.ShapeDtypeStruct((n * N_HEADS, V_DIM), F32), jax.ShapeDtypeStruct((n, D_POOL), BF16),
                   jax.ShapeDtypeStruct((n_seq, POOL_HIST, D_POOL), F32)],
        scratch_shapes=[pltpu.VMEM((n_seq, rows, D_POOL), F32)],
        compiler_params=_cparams(("arbitrary",)),
        name="proj_pool_sample",
    )(x2d, w_qkvu, b_qkvu, wg_bd, s_pool, state_pad)


def _alibi_slopes():
    return 2.0 ** (-8.0 * jnp.arange(1, N_HEADS + 1, dtype=F32) / N_HEADS)


def _attention(qt, kb, vt, q_s, k_new, v_new, cache_k, cache_v, page_table, lam_vecs, g_sub,
               n_seq, seq_len, t_new, tq=512):
    n = kb.shape[0]
    nq = seq_len // tq
    n_pairs = nq // 2
    n_smp, n_pages = page_table.shape
    n_past = n_pages * PAGE_SIZE
    assert nq % 2 == 0 and n_smp == SAMPLES_PER_STEP * n_seq * N_HEADS * n_pairs, (n_smp, n_seq, nq)
    slopes = _alibi_slopes()
    rel = (jnp.arange(tq, dtype=jnp.int32)[None, :] - jnp.arange(tq, dtype=jnp.int32)[:, None])
    b_off = -(slopes * LOG2E)[:, None, None] * rel.astype(F32)[None]
    b_diag = jnp.where(rel[None] >= 0, b_off, NEG_INF)
    r = jnp.arange(2 * N_HEADS * t_new, dtype=jnp.int32)
    r_head, r_q = r // (2 * t_new), r % t_new
    row_slope = slopes[r_head][:, None]
    kpos = jnp.arange(n_past, dtype=jnp.int32)[None, :]
    b_past = -row_slope * (n_past + r_q[:, None] - kpos).astype(F32)
    j = jnp.arange(PAGE_SIZE, dtype=jnp.int32)[None, :]
    dist_new = r_q[:, None] - j
    b_new = jnp.where((dist_new >= 0) & (j < t_new), -row_slope * dist_new.astype(F32), NEG_INF)
    bias_s = jnp.concatenate([b_past, b_new], axis=1)
    page_rows = PAGE_SIZE * N_HEADS
    ck = cache_k.reshape(-1, V_DIM)
    cv = cache_v.reshape(-1, V_DIM)
    pt = page_table.reshape(-1).astype(jnp.int32)
    step = lambda b, h, i: (b * N_HEADS + h) * n_pairs + i
    const2 = lambda b, h, i, pt_ref: (0, 0)
    vec = pl.BlockSpec((1, HEAD_DIM), const2)
    smp_rows = SAMPLES_PER_STEP * t_new
    tok = pl.BlockSpec((smp_rows, D_QK), lambda b, h, i, pt_ref: (step(b, h, i), 0))
    tok_hm = pl.BlockSpec((smp_rows * N_HEADS, V_DIM), lambda b, h, i, pt_ref: (step(b, h, i), 0))
    table = pl.BlockSpec((None, tq, tq), lambda b, h, i, pt_ref: (h, 0, 0))

    def page_spec(sj, pi):
        return pl.BlockSpec((page_rows, V_DIM), lambda b, h, i, pt_ref: (
            pt_ref[(step(b, h, i) * SAMPLES_PER_STEP + sj) * n_pages + pi], 0))

    pages = []
    for sj in range(SAMPLES_PER_STEP):
        pages += [(page_spec(sj, pi), ck) for pi in range(n_pages)] + [(page_spec(sj, pi), cv) for pi in range(n_pages)]
    grid_spec = pltpu.PrefetchScalarGridSpec(
        num_scalar_prefetch=1,
        grid=(n_seq, N_HEADS, n_pairs),
        in_specs=[pl.BlockSpec(memory_space=pltpu.SMEM), vec, vec, vec, vec,
                  pl.BlockSpec((V_DIM, tq), lambda b, h, i, pt_ref: (h, b * nq + nq - 1 - i)),
                  pl.BlockSpec((V_DIM, tq), lambda b, h, i, pt_ref: (h, b * nq + i)),
                  pl.BlockSpec((seq_len, V_DIM), lambda b, h, i, pt_ref: (b, h)),
                  pl.BlockSpec((V_DIM, seq_len), lambda b, h, i, pt_ref: (h, b)),
                  table, table, pl.BlockSpec((1, V_DIM), const2), pl.BlockSpec(bias_s.shape, const2),
                  tok, tok_hm, tok_hm] + [spec for spec, _ in pages],
        out_specs=[pl.BlockSpec((2 * tq, V_DIM), lambda b, h, i, pt_ref: (b * n_pairs + i, h)), tok],
        scratch_shapes=[pltpu.VMEM((2, V_DIM, 2 * tq), BF16), pltpu.VMEM((2, 1, 2 * tq), F32),
                        pltpu.VMEM((2, V_DIM + DENOM_ROWS, 2 * tq), F32)],
    )
    return pl.pallas_call(
        functools.partial(_attn_kernel, n_pages=n_pages, t_new=t_new, tq=tq, nq=nq),
        grid_spec=grid_spec,
        out_shape=[jax.ShapeDtypeStruct((n, D_ATTN), BF16), jax.ShapeDtypeStruct((n_smp * t_new, D_ATTN), F32)],
        compiler_params=_cparams(("arbitrary", "arbitrary", "arbitrary")),
        name="attention",
    )(pt, slopes, *lam_vecs, qt, qt, kb, vt, b_off, b_diag, g_sub, bias_s, q_s, k_new, v_new,
      *[arr for _, arr in pages])


def _pair_major_tile(i, nq):
    b, qi = i // nq, i % nq
    return b * nq + jnp.where(qi >= nq // 2, 2 * (nq - 1 - qi), 2 * qi + 1)


def _merge(x2d, y_pool, y_attn, wts, tile0, n_tiles_total, carried=None, attn_tile=lambda i: i):
    tm = MOE_TILE
    n = x2d.shape[0]
    row = lambda i: (i, 0)
    out_row = lambda i: (i + tile0, 0)
    consts = [wts[k] for k in ("w_gate", "b_gate", "w_pool_br", "w_attn_br", "w_o", "ln1_g", "ln1_b",
                               "wrt_hi", "wrt_lo", "b_rt", "upper", "lower")]
    operands = [x2d, y_pool, y_attn, *consts]
    in_specs = [pl.BlockSpec((tm, D_MODEL), row), pl.BlockSpec((tm, D_POOL), row),
                pl.BlockSpec((tm, D_ATTN), lambda i: (attn_tile(i), 0))] + [_full(c.shape) for c in consts]
    aliases = {}
    if carried is not None:
        aliases = {len(operands) + k: k for k in range(len(carried))}
        operands += list(carried)
        in_specs += [pl.BlockSpec(memory_space=pl.ANY)] * len(carried)
    return pl.pallas_call(
        functools.partial(_merge_kernel, tm=tm),
        grid=(n // tm,),
        in_specs=in_specs,
        out_specs=[pl.BlockSpec((tm, D_MODEL), out_row), pl.BlockSpec((MOE_REGION, D_MODEL), out_row),
                   pl.BlockSpec((tm, ROUTER_COLS), out_row), pl.BlockSpec((ROUTER_COLS, ROUTER_COLS), out_row)],
        out_shape=[jax.ShapeDtypeStruct((n_tiles_total * tm, D_MODEL), F32),
                   jax.ShapeDtypeStruct((n_tiles_total * MOE_REGION, D_MODEL), BF16),
                   jax.ShapeDtypeStruct((n_tiles_total * tm, ROUTER_COLS), F32),
                   jax.ShapeDtypeStruct((n_tiles_total * ROUTER_COLS, ROUTER_COLS), F32)],
        input_output_aliases=aliases,
        compiler_params=_cparams(("arbitrary",)),
        name="merge_ln1_route_dispatch",
    )(*operands)


def _granule_schedule(cnt_pad, n_steps):
    n_tiles = cnt_pad.shape[0]
    gc = cnt_pad // MOE_GRANULE
    seg = jnp.cumsum(gc, axis=1) - gc
    tot = jnp.sum(gc, axis=0)
    ahead = jnp.cumsum(gc, axis=0) - gc
    slots_e = ((tot + MOE_SLOTS - 1) // MOE_SLOTS) * MOE_SLOTS
    first_slot = jnp.cumsum(slots_e) - slots_e
    p0 = (jnp.arange(n_steps, dtype=jnp.int32) * MOE_SLOTS)[None, :]
    in_e = ((p0 >= first_slot[:, None]) & (p0 < (first_slot + slots_e)[:, None])).astype(jnp.int32)
    q0 = p0 - jnp.sum(in_e * first_slot[:, None], axis=0, keepdims=True)
    tot_s = jnp.sum(in_e * tot[:, None], axis=0, keepdims=True)
    e_id = jnp.sum(in_e * jnp.arange(N_EXPERTS, dtype=jnp.int32)[:, None], axis=0)
    step_expert = jnp.where(q0[0] < tot_s[0], e_id, N_EXPERTS).astype(jnp.int32)
    pick = lambda tab: jnp.sum(tab[:, :, None] * in_e[None, :, :], axis=1)[:, :, None]
    ahead_s, gc_s, seg_s = pick(ahead), pick(gc), pick(seg)
    q = q0[0][None, :, None] + jnp.arange(MOE_SLOTS, dtype=jnp.int32)[None, None, :]
    in_t = ((q >= ahead_s) & (q < ahead_s + gc_s)).astype(jnp.int32)
    tile_base = (jnp.arange(n_tiles, dtype=jnp.int32) * REGION_GRANULES)[:, None, None]
    gid = jnp.sum(in_t * (tile_base + seg_s + q - ahead_s), axis=0)
    gin = jnp.where(q[0] < tot_s[0][:, None], gid, 0).astype(jnp.int32).reshape(-1)
    s = jnp.arange(REGION_GRANULES, dtype=jnp.int32)[None, None, :]
    in_g = ((s >= seg[:, :, None]) & (s < (seg + gc)[:, :, None])).astype(jnp.int32)
    loc = jnp.sum(in_g * ((first_slot[None, :] + ahead - seg)[:, :, None] + s), axis=1)
    used = jnp.sum(in_g, axis=1) > 0
    loc = jnp.where(used, loc, loc[:, 0:1]).astype(jnp.int32)
    return gin, step_expert, loc.reshape(-1)


def _experts(xs, gin, step_expert, w_gate, w_up, w_down, n_steps):
    rows = MOE_SLOTS * MOE_GRANULE

    def slot_spec(k):
        return pl.BlockSpec((MOE_GRANULE, D_MODEL), lambda t, gin_ref, se_ref: (gin_ref[t * MOE_SLOTS + k], 0))

    w_sel = lambda t, gin_ref, se_ref: (jnp.minimum(se_ref[t], N_EXPERTS - 1), 0, 0)
    grid_spec = pltpu.PrefetchScalarGridSpec(
        num_scalar_prefetch=2,
        grid=(n_steps,),
        in_specs=[slot_spec(k) for k in range(MOE_SLOTS)]
                 + [pl.BlockSpec((None, D_MODEL, D_EXPERT), w_sel), pl.BlockSpec((None, D_MODEL, D_EXPERT), w_sel),
                    pl.BlockSpec((None, D_EXPERT, D_MODEL), w_sel)],
        out_specs=pl.BlockSpec((rows, D_MODEL), lambda t, gin_ref, se_ref: (t, 0)),
        scratch_shapes=[pltpu.VMEM((D_MODEL, D_EXPERT), BF16), pltpu.VMEM((D_MODEL, D_EXPERT), BF16),
                        pltpu.VMEM((D_EXPERT, D_MODEL), BF16)],
    )
    return pl.pallas_call(
        functools.partial(_experts_kernel, n_slots=MOE_SLOTS),
        grid_spec=grid_spec,
        out_shape=jax.ShapeDtypeStruct((n_steps * rows, D_MODEL), BF16),
        compiler_params=_cparams(("arbitrary",)),
        name="expert_mlps",
    )(gin, step_expert, *([xs] * MOE_SLOTS), w_gate, w_up, w_down)


def _final(x1f, route, ys, loc, p2d, wts, tile0):
    tm = MOE_TILE
    n = p2d.shape[0]
    consts = [wts[k] for k in ("ln2_g", "ln2_b", "w_ple_gate", "b_ple_gate", "w_ple", "ln3_g", "ln3_b")]

    def granule_spec(s):
        return pl.BlockSpec((MOE_GRANULE, D_MODEL),
                            lambda i, loc_ref: (loc_ref[(i + tile0) * REGION_GRANULES + s], 0))

    grid_spec = pltpu.PrefetchScalarGridSpec(
        num_scalar_prefetch=1,
        grid=(n // tm,),
        in_specs=[pl.BlockSpec((tm, D_MODEL), lambda i, loc_ref: (i + tile0, 0)),
                  pl.BlockSpec((tm, ROUTER_COLS), lambda i, loc_ref: (i + tile0, 0))]
                 + [granule_spec(s) for s in range(REGION_GRANULES)]
                 + [pl.BlockSpec((tm, D_PLE), lambda i, loc_ref: (i, 0))]
                 + [pl.BlockSpec(c.shape, lambda i, loc_ref: (0, 0)) for c in consts],
        out_specs=pl.BlockSpec((tm, D_MODEL), lambda i, loc_ref: (i, 0)),
    )
    return pl.pallas_call(
        functools.partial(_final_kernel, n_granules=REGION_GRANULES),
        grid_spec=grid_spec,
        out_shape=jax.ShapeDtypeStruct((n, D_MODEL), F32),
        compiler_params=_cparams(("arbitrary",)),
        name="combine_ln2_ple_ln3",
    )(loc, x1f, route, *([ys] * REGION_GRANULES), p2d, *consts)


def _prepare_weights(w_in, b_in, w_pool_grp, s_pool, w_pool_br, w_attn_br, w_o, ln1_g, ln1_b,
                     w_rg, b_rg, w_re, b_re, w_gate, w_up, w_down, ln2_g, ln2_b,
                     w_ple_gate, b_ple_gate, w_ple, ln3_g, ln3_b):
    i = 0
    row = lambda a: a[i].reshape(1, -1).astype(F32)
    wg_bd = jnp.zeros((D_POOL, D_POOL), F32)
    for gi in range(len(POOL_WINDOWS)):
        sl = slice(gi * POOL_GROUP, (gi + 1) * POOL_GROUP)
        wg_bd = wg_bd.at[sl, sl].set(w_pool_grp[i, gi])
    w_r = jnp.zeros((D_MODEL, ROUTER_COLS), F32)
    w_r = w_r.at[:, 0:N_EXPERT_GROUPS].set(w_rg[i])
    w_r = w_r.at[:, N_EXPERT_GROUPS:N_EXPERT_GROUPS + N_EXPERTS].set(w_re[i].reshape(D_MODEL, N_EXPERTS))
    b_r = jnp.zeros((1, ROUTER_COLS), F32)
    b_r = b_r.at[0, 0:N_EXPERT_GROUPS].set(b_rg[i])
    b_r = b_r.at[0, N_EXPERT_GROUPS:N_EXPERT_GROUPS + N_EXPERTS].set(b_re[i].reshape(-1))
    wrt_hi = w_r.T.astype(BF16)
    wrt_lo = (w_r.T - wrt_hi.astype(F32)).astype(BF16)
    tok = jnp.arange(MOE_TILE, dtype=jnp.int32)
    upper = (tok[:, None] < tok[None, :]).astype(BF16)
    rr = jnp.arange(ROUTER_COLS, dtype=jnp.int32)
    lower = (rr[None, :] < rr[:, None]).astype(BF16)
    return dict(
        w_gate_e=w_gate[i].reshape(N_EXPERTS, D_MODEL, D_EXPERT), w_up_e=w_up[i].reshape(N_EXPERTS, D_MODEL, D_EXPERT),
        w_down_e=w_down[i].reshape(N_EXPERTS, D_EXPERT, D_MODEL),
        wrt_hi=wrt_hi, wrt_lo=wrt_lo, b_rt=b_r.reshape(ROUTER_COLS, 1), upper=upper, lower=lower,
        w_qkvu=w_in[i, :, 0:D_QKVU].astype(BF16), b_qkvu=b_in[i, 0:D_QKVU].reshape(1, -1),
        w_gate=w_in[i, :, D_QKVU:].astype(BF16), b_gate=b_in[i, D_QKVU:].reshape(1, -1),
        wg_bd=wg_bd.astype(BF16), s_pool=row(s_pool),
        w_pool_br=w_pool_br[i].astype(BF16), w_attn_br=w_attn_br[i].astype(BF16), w_o=w_o[i].astype(BF16),
        ln1_g=row(ln1_g), ln1_b=row(ln1_b), ln2_g=row(ln2_g), ln2_b=row(ln2_b),
        w_ple_gate=w_ple_gate[i].astype(BF16), b_ple_gate=row(b_ple_gate), w_ple=w_ple[i].astype(BF16),
        ln3_g=row(ln3_g), ln3_b=row(ln3_b))


def kernel(x_prompt, x_sample, p_prompt, p_sample, cache_k, cache_v, state_pool, page_table, w_in, b_in, lam_q1, lam_k1, lam_q2, lam_k2, g_sub, w_pool_grp, s_pool, w_pool_br, w_attn_br, w_o, ln1_g, ln1_b, w_rg, b_rg, w_re, b_re, w_gate, w_up, w_down, ln2_g, ln2_b, w_ple_gate, b_ple_gate, w_ple, ln3_g, ln3_b):
    assert w_in.shape[0] == DEPTH == 1
    bsz, seq, _ = x_prompt.shape
    dbs, dseq, _ = x_sample.shape
    n_past = page_table.shape[1] * PAGE_SIZE
    wts = _prepare_weights(w_in, b_in, w_pool_grp, s_pool, w_pool_br, w_attn_br, w_o, ln1_g, ln1_b,
                           w_rg, b_rg, w_re, b_re, w_gate, w_up, w_down, ln2_g, ln2_b,
                           w_ple_gate, b_ple_gate, w_ple, ln3_g, ln3_b)
    lam_vecs = [a[0].reshape(1, HEAD_DIM).astype(F32) for a in (lam_q1, lam_k1, lam_q2, lam_k2)]
    g_row = g_sub[0].reshape(1, V_DIM).astype(F32)

    xp = x_prompt.reshape(bsz * seq, D_MODEL)
    qt_p, kf_p, vf_p, kb_p, vt_p, yp_p, tail_p = _proj_prompt(
        xp, wts["w_qkvu"], wts["b_qkvu"], wts["wg_bd"], wts["s_pool"], bsz, seq)

    xs = x_sample.reshape(dbs * dseq, D_MODEL)
    state_pad = jnp.pad(state_pool[0], ((0, 0), (POOL_HIST - POOL_STATE, 0), (0, 0)))
    q_s, kf_s, vf_s, yp_s, tail_s = _proj_sample(
        xs, wts["w_qkvu"], wts["b_qkvu"], wts["wg_bd"], wts["s_pool"], state_pad, dbs, dseq, n_past)

    ya_p, ya_s = _attention(qt_p, kb_p, vt_p, q_s, kf_s, vf_s, cache_k, cache_v, page_table, lam_vecs, g_row,
                            bsz, seq, dseq, tq=MOE_TILE)

    tiles_p = (bsz * seq) // MOE_TILE
    tiles_s = (dbs * dseq) // MOE_TILE
    n_tiles = tiles_p + tiles_s
    shared = _merge(xp, yp_p, ya_p, wts, 0, n_tiles, attn_tile=lambda i: _pair_major_tile(i, seq // MOE_TILE))
    x1f, xsort, route, cnt = _merge(xs, yp_s, ya_s, wts, tiles_p, n_tiles, carried=shared)
    cnt_pad = cnt[:, 0].reshape(n_tiles, ROUTER_COLS)[:, N_EXPERT_GROUPS:N_EXPERT_GROUPS + N_EXPERTS]
    n_steps = (n_tiles * REGION_GRANULES + N_EXPERTS * (MOE_SLOTS - 1) + MOE_SLOTS - 1) // MOE_SLOTS
    gin, step_expert, loc = _granule_schedule(cnt_pad.astype(jnp.int32), n_steps)
    ysort = _experts(xsort, gin, step_expert, wts["w_gate_e"], wts["w_up_e"], wts["w_down_e"], n_steps)
    y_p = _final(x1f, route, ysort, loc, p_prompt[0].reshape(bsz * seq, D_PLE), wts, 0)
    y_s = _final(x1f, route, ysort, loc, p_sample[0].reshape(dbs * dseq, D_PLE), wts, tiles_p)

    drop = POOL_HIST - POOL_STATE
    return (y_p.reshape(bsz, seq, D_MODEL),
            y_s.reshape(dbs, dseq, D_MODEL),
            kf_p.reshape(1, bsz, seq, N_HEADS, V_DIM),
            vf_p.reshape(1, bsz, seq, N_HEADS, V_DIM),
            tail_p[None, :, drop:, :],
            kf_s.reshape(1, dbs, dseq, N_HEADS, V_DIM),
            vf_s.reshape(1, dbs, dseq, N_HEADS, V_DIM),
            tail_s[None, :, drop:, :])
```

```python
import functools
import math

import jax
import jax.numpy as jnp
from jax import lax
from jax.experimental import pallas as pl
from jax.experimental.pallas import tpu as pltpu

F32 = jnp.float32
BF16 = jnp.bfloat16

D_MODEL = 1024
N_HEADS = 4
HEAD_DIM = 64
V_DIM = 2 * HEAD_DIM
D_QK = N_HEADS * 2 * HEAD_DIM
D_ATTN = N_HEADS * V_DIM
D_POOL = 512
POOL_WINDOWS = (2, 4, 8, 16)
POOL_GROUP = D_POOL // len(POOL_WINDOWS)
POOL_STATE = max(POOL_WINDOWS) - 1
POOL_HIST = POOL_STATE + 1
N_EXPERT_GROUPS = 4
EXPERTS_PER_GROUP = 8
N_EXPERTS = N_EXPERT_GROUPS * EXPERTS_PER_GROUP
D_EXPERT = 256
D_PLE = 256
PAGE_SIZE = 128
LN_EPS = 1e-5
DEPTH = 1
DEEPNORM_ALPHA = (2 * DEPTH) ** 0.25
LAM_INIT = 0.8 - 0.6 * math.exp(-0.3 * 0)
D_QKVU = D_POOL + 2 * D_QK + D_ATTN
ROUTER_COLS = 128
TOP_K_IN_GROUP = 2
MOE_TILE = 512
MOE_GRANULE = 16
MOE_REGION = -(-(TOP_K_IN_GROUP * MOE_TILE + N_EXPERTS * (MOE_GRANULE - 1)) // 128) * 128
REGION_GRANULES = MOE_REGION // MOE_GRANULE
MOE_SLOTS = 32
LOG2E = 1.4426950408889634
DENOM_ROWS = 16
SAMPLES_PER_STEP = 2
SCORES_AHEAD = 2
V7X_VMEM_LIMIT = 56 * 1024 * 1024
NEG_INF = float("-inf")


def _sigmoid(x):
    return 1.0 / (1.0 + jnp.exp(-x))


def _layer_norm(x, g, b):
    mu = jnp.mean(x, axis=-1, keepdims=True)
    xc = x - mu
    var = jnp.mean(xc * xc, axis=-1, keepdims=True)
    return xc * lax.rsqrt(var + LN_EPS) * g + b


def _lam_value(lq1_ref, lk1_ref, lq2_ref, lk2_ref):
    a = jnp.sum(lq1_ref[...] * lk1_ref[...], axis=1, keepdims=True)
    b = jnp.sum(lq2_ref[...] * lk2_ref[...], axis=1, keepdims=True)
    return jnp.exp(a) - jnp.exp(b) + LAM_INIT


def _window_sums(e):
    outs = []
    for gi, w in enumerate(POOL_WINDOWS):
        s = e[:, gi * POOL_GROUP:(gi + 1) * POOL_GROUP]
        step = 1
        while step < w:
            s = s + pltpu.roll(s, step, axis=0)
            step *= 2
        outs.append(s)
    return outs


def _pool_branch(wins, u, inv_cnts, wg_ref, sp_ref):
    ds = []
    for gi in range(len(POOL_WINDOWS)):
        ds.append(wins[gi] * inv_cnts[gi] - u[:, gi * POOL_GROUP:(gi + 1) * POOL_GROUP])
    d = jnp.concatenate(ds, axis=1).astype(BF16)
    return jnp.dot(d, wg_ref[...], preferred_element_type=F32) * sp_ref[...]


def _store_head_major(ref, x, n_rows):
    for hh in range(N_HEADS):
        ref[pl.ds(hh, n_rows, stride=N_HEADS), :] = x[:, hh * V_DIM:(hh + 1) * V_DIM]


def _proj_prompt_kernel(x_ref, w_ref, b_ref, wg_ref, sp_ref,
                        qt_ref, kf_ref, vf_ref, kb_ref, vt_ref, yp_ref, tail_ref,
                        ext_ref, *, tm, tiles_per_seq):
    t_in_seq = pl.program_id(0) % tiles_per_seq
    z = jnp.dot(x_ref[...].astype(BF16), w_ref[...], preferred_element_type=F32) + b_ref[...]
    u = z[:, 0:D_POOL]
    q = z[:, D_POOL:D_POOL + D_QK]
    k = z[:, D_POOL + D_QK:D_POOL + 2 * D_QK]
    v = z[:, D_POOL + 2 * D_QK:D_QKVU]
    qt_ref[...] = (q * (HEAD_DIM ** -0.5 * LOG2E)).T.astype(BF16)
    vt_ref[...] = v.T.astype(BF16)
    kb_ref[...] = k.astype(BF16)
    _store_head_major(kf_ref, k, tm)
    _store_head_major(vf_ref, v, tm)

    @pl.when(t_in_seq == 0)
    def _():
        ext_ref[0:POOL_HIST, :] = jnp.zeros((POOL_HIST, D_POOL), F32)

    ext_ref[POOL_HIST:POOL_HIST + tm, :] = u
    wins = [s[POOL_HIST:, :] for s in _window_sums(ext_ref[...])]
    pos = t_in_seq * tm + lax.broadcasted_iota(jnp.int32, (tm, 1), 0)
    inv_cnts = [1.0 / jnp.minimum(pos + 1, w).astype(F32) for w in POOL_WINDOWS]
    yp_ref[...] = _pool_branch(wins, u, inv_cnts, wg_ref, sp_ref).astype(BF16)
    tail = u[tm - POOL_HIST:, :]
    ext_ref[0:POOL_HIST, :] = tail
    tail_ref[...] = tail


def _proj_sample_kernel(x_ref, w_ref, b_ref, wg_ref, sp_ref, st_ref,
                        q_ref, kf_ref, vf_ref, yp_ref, tail_ref,
                        ext_ref, *, n_seq, t_new, n_past):
    z = jnp.dot(x_ref[...].astype(BF16), w_ref[...], preferred_element_type=F32) + b_ref[...]
    u = z[:, 0:D_POOL]
    q_ref[...] = z[:, D_POOL:D_POOL + D_QK] * (HEAD_DIM ** -0.5)
    _store_head_major(kf_ref, z[:, D_POOL + D_QK:D_POOL + 2 * D_QK], n_seq * t_new)
    _store_head_major(vf_ref, z[:, D_POOL + 2 * D_QK:D_QKVU], n_seq * t_new)
    rows = POOL_HIST + t_new
    ext_ref[:, 0:POOL_HIST, :] = st_ref[...]
    ext_ref[:, POOL_HIST:rows, :] = u.reshape(n_seq, t_new, D_POOL)
    e3 = ext_ref[...]
    wins = [s.reshape(n_seq, rows, POOL_GROUP)[:, POOL_HIST:, :].reshape(n_seq * t_new, POOL_GROUP)
            for s in _window_sums(e3.reshape(n_seq * rows, D_POOL))]
    inv_cnts = [1.0 / float(min(n_past + 1, w)) for w in POOL_WINDOWS]
    yp_ref[...] = _pool_branch(wins, u, inv_cnts, wg_ref, sp_ref).astype(BF16)
    tail_ref[...] = e3[:, rows - POOL_HIST:, :]


def _attn_prompt_pair(slope, lam, qt_refs, k_ref, vt_ref, boff_ref, bdiag_ref, g_ref, o_ref,
                      q2t_ref, m_ref, acc_ref, *, tq, nq):
    pair = pl.program_id(2)
    tiles = (nq - 1 - pair, pair)
    half = tq // 2
    all_lanes = slice(0, 2 * tq)
    late_lanes = (slice(half, tq), slice(tq + half, 2 * tq))
    for slot, qt_ref in enumerate(qt_refs):
        qt = qt_ref[...]
        row = lax.broadcasted_iota(jnp.int32, qt.shape, 0)
        zero = jnp.zeros_like(qt)
        q2t_ref[slot] = jnp.concatenate([jnp.where(row < HEAD_DIM, qt, zero), jnp.where(row >= HEAD_DIM, qt, zero)],
                                        axis=1)

    def both_maps(bias):
        return jnp.concatenate([bias, bias], axis=1)

    def scores(slot, k_start, n_keys, lanes, bias):
        kblk = k_ref[pl.ds(k_start, n_keys), :]
        return jnp.dot(kblk, q2t_ref[slot, :, lanes], preferred_element_type=F32) + bias

    def absorb(t, slot, k_start, n_keys, lanes, shift, first):
        vt_aug = jnp.concatenate([vt_ref[:, pl.ds(k_start, n_keys)], jnp.ones((DENOM_ROWS, n_keys), BF16)], axis=0)
        m_new = jnp.max(t, axis=0, keepdims=True) + shift
        if not first:
            m_prev = m_ref[slot, :, lanes]
            m_new = jnp.maximum(m_prev, m_new)
        pv = jnp.dot(vt_aug, jnp.exp2(t - (m_new - shift)).astype(BF16), preferred_element_type=F32)
        acc_ref[slot, :, lanes] = pv if first else jnp.exp2(m_prev - m_new) * acc_ref[slot, :, lanes] + pv
        m_ref[slot, :, lanes] = m_new

    zero_shift = jnp.float32(0.0)
    tasks = []
    for slot in range(2):
        k_diag = pl.multiple_of(tiles[slot] * tq, tq)
        k_diag2 = pl.multiple_of(k_diag + half, half)
        tasks.append((slot, k_diag, half, all_lanes, lambda: both_maps(bdiag_ref[0:half, :]), zero_shift, True))
        for lanes in late_lanes:
            tasks.append((slot, k_diag2, half, lanes, lambda: bdiag_ref[half:tq, half:tq], zero_shift, False))
    for j in range(nq - 1):
        in_second = j >= tiles[0]
        kb = jnp.where(in_second, j - tiles[0], j)
        q_tile = jnp.where(in_second, tiles[1], tiles[0])
        shift = -(slope * LOG2E) * ((q_tile - kb) * tq).astype(F32)
        tasks.append((in_second.astype(jnp.int32), pl.multiple_of(kb * tq, tq), tq, all_lanes,
                      lambda: both_maps(boff_ref[...]), shift, False))
    queue = []
    for slot, k_start, n_keys, lanes, bias_fn, shift, first in tasks:
        queue.append((scores(slot, k_start, n_keys, lanes, bias_fn()), slot, k_start, n_keys, lanes, shift, first))
        if len(queue) > SCORES_AHEAD:
            absorb(*queue.pop(0))
    while queue:
        absorb(*queue.pop(0))

    for slot in range(2):
        acc = acc_ref[slot]
        on = acc[0:V_DIM, :] / acc[V_DIM:V_DIM + 1, :]
        o = (on[:, 0:tq] - lam * on[:, tq:]).T
        o = o * lax.rsqrt(jnp.mean(o * o, axis=1, keepdims=True) + LN_EPS) * g_ref[...] * (1.0 - LAM_INIT)
        o_ref[slot * tq:(slot + 1) * tq, :] = o.astype(BF16)


def _attn_sample_body(lam, sj, q_ref, kn_ref, vn_ref, bias_ref, g_ref, k_pages, v_pages, *, t_new):
    n_pages = len(k_pages)
    n_past = n_pages * PAGE_SIZE
    rows = slice(sj * t_new, (sj + 1) * t_new)
    q = q_ref[rows, :]
    lane = lax.broadcasted_iota(jnp.int32, (t_new, V_DIM), 1)
    pad = jnp.zeros((PAGE_SIZE - t_new, V_DIM), BF16)
    nt_dims = (((1,), (1,)), ((), ()))

    def head_rows(pages, new_ref, hh):
        past = jnp.concatenate([pg[pl.ds(hh, PAGE_SIZE, stride=N_HEADS), :] for pg in pages], axis=0)
        new = new_ref[pl.ds(sj * t_new * N_HEADS + hh, t_new, stride=N_HEADS), :]
        new = jnp.concatenate([new.astype(BF16), pad], axis=0)
        return past.astype(BF16), new

    s_rows = []
    for hh in range(N_HEADS):
        qh = q[:, hh * V_DIM:(hh + 1) * V_DIM]
        qh2 = jnp.concatenate([jnp.where(lane < HEAD_DIM, qh, 0.0), jnp.where(lane >= HEAD_DIM, qh, 0.0)],
                              axis=0).astype(BF16)
        k_past, k_new = head_rows(k_pages, kn_ref, hh)
        s_rows.append(jnp.concatenate(
            [lax.dot_general(qh2, k_past, nt_dims, preferred_element_type=F32),
             lax.dot_general(qh2, k_new, nt_dims, preferred_element_type=F32)], axis=1))
    s = jnp.concatenate(s_rows, axis=0) + bias_ref[...]
    m = jnp.max(s, axis=1, keepdims=True)
    p = jnp.exp(s - m)
    p = p / jnp.sum(p, axis=1, keepdims=True)
    outs = []
    for h0 in range(0, N_HEADS, 2):
        a2 = jnp.concatenate(
            [p[hh * 2 * t_new:hh * 2 * t_new + t_new, :] - lam * p[hh * 2 * t_new + t_new:(hh + 1) * 2 * t_new, :]
             for hh in (h0, h0 + 1)], axis=0).astype(BF16)
        (vp0, vn0), (vp1, vn1) = head_rows(v_pages, vn_ref, h0), head_rows(v_pages, vn_ref, h0 + 1)
        o2 = (jnp.dot(a2[:, 0:n_past], jnp.concatenate([vp0, vp1], axis=1), preferred_element_type=F32)
              + jnp.dot(a2[:, n_past:], jnp.concatenate([vn0, vn1], axis=1), preferred_element_type=F32))
        for k in range(2):
            oh = o2[k * t_new:(k + 1) * t_new, k * V_DIM:(k + 1) * V_DIM]
            oh = oh * lax.rsqrt(jnp.mean(oh * oh, axis=1, keepdims=True) + LN_EPS) * g_ref[...] * (1.0 - LAM_INIT)
            outs.append(oh)
    return jnp.concatenate(outs, axis=1)


def _attn_kernel(pt_ref, slopes_ref, lq1_ref, lk1_ref, lq2_ref, lk2_ref,
                 qta_ref, qtb_ref, k_ref, vt_ref, boff_ref, bdiag_ref, g_ref, bias_s_ref, *rest,
                 n_pages, t_new, tq, nq):
    qs_ref, kn_ref, vn_ref = rest[:3]
    pages = rest[3:3 + SAMPLES_PER_STEP * 2 * n_pages]
    op_ref, os_ref, q2t_ref, m_ref, acc_ref = rest[3 + SAMPLES_PER_STEP * 2 * n_pages:]
    lam = _lam_value(lq1_ref, lk1_ref, lq2_ref, lk2_ref)
    outs = []
    for sj in range(SAMPLES_PER_STEP):
        pg = pages[sj * 2 * n_pages:(sj + 1) * 2 * n_pages]
        outs.append(_attn_sample_body(lam, sj, qs_ref, kn_ref, vn_ref, bias_s_ref, g_ref, pg[:n_pages],
                                      pg[n_pages:], t_new=t_new))
    os_ref[...] = jnp.concatenate(outs, axis=0).astype(BF16)
    _attn_prompt_pair(slopes_ref[pl.program_id(1)], lam, (qta_ref, qtb_ref), k_ref, vt_ref, boff_ref, bdiag_ref,
                      g_ref, op_ref, q2t_ref, m_ref, acc_ref, tq=tq, nq=nq)


def _merge_kernel(xp_ref, ypp_ref, yap_ref, xs_in_ref, yps_ref, yas_ref,
                  wgate_ref, bgate_ref, wpb_ref, wab_ref, wo_ref,
                  g1_ref, b1_ref, wrt_hi_ref, wrt_lo_ref, brt_ref, upper_ref, lower_ref,
                  x1f_ref, xs_ref, route_ref, cnt_ref, *, tm, prompt_tiles):
    is_sample = pl.program_id(0) >= prompt_tiles
    x = jnp.where(is_sample, xs_in_ref[...], xp_ref[...])
    y_pool = jnp.where(is_sample, yps_ref[...], ypp_ref[...])
    y_attn = jnp.where(is_sample, yas_ref[...], yap_ref[...])
    zg = jnp.dot(x.astype(BF16), wgate_ref[...], preferred_element_type=F32) + bgate_ref[...]
    gp = _sigmoid(zg[:, 0:D_MODEL])
    ga = _sigmoid(zg[:, D_MODEL:])
    merged = (gp * jnp.dot(y_pool, wpb_ref[...], preferred_element_type=F32)
              + ga * jnp.dot(y_attn, wab_ref[...], preferred_element_type=F32))
    x1 = _layer_norm(DEEPNORM_ALPHA * x + jnp.dot(merged.astype(BF16), wo_ref[...], preferred_element_type=F32),
                     g1_ref[...], b1_ref[...])
    x1f_ref[...] = x1
    x1_hi = x1.astype(BF16)
    x1_lo = (x1 - x1_hi.astype(F32)).astype(BF16)
    nt_dims = (((1,), (1,)), ((), ()))
    lg = (lax.dot_general(wrt_hi_ref[...], x1_hi, nt_dims, preferred_element_type=F32)
          + lax.dot_general(wrt_hi_ref[...], x1_lo, nt_dims, preferred_element_type=F32)
          + lax.dot_general(wrt_lo_ref[...], x1_hi, nt_dims, preferred_element_type=F32)) + brt_ref[...]
    row = lax.broadcasted_iota(jnp.int32, lg.shape, 0)
    big = jnp.int32(4 * ROUTER_COLS)
    lgg = jnp.where(row < N_EXPERT_GROUPS, lg, NEG_INF)
    gmax = jnp.max(lgg, axis=0, keepdims=True)
    g_w = 1.0 / jnp.sum(jnp.exp(lgg - gmax), axis=0, keepdims=True)
    g_idx = jnp.min(jnp.where(lgg == gmax, row, big), axis=0, keepdims=True)
    lo_row = N_EXPERT_GROUPS + g_idx * EXPERTS_PER_GROUP
    le = jnp.where((row >= lo_row) & (row < lo_row + EXPERTS_PER_GROUP), lg, NEG_INF)
    v1 = jnp.max(le, axis=0, keepdims=True)
    i1 = jnp.min(jnp.where(le == v1, row, big), axis=0, keepdims=True)
    le2 = jnp.where(row == i1, NEG_INF, le)
    v2 = jnp.max(le2, axis=0, keepdims=True)
    i2 = jnp.min(jnp.where(le2 == v2, row, big), axis=0, keepdims=True)
    e21 = jnp.exp(v2 - v1)
    c1 = g_w / (1.0 + e21)
    c2 = g_w * e21 / (1.0 + e21)
    sel1 = row == i1
    sel2 = row == i2
    member = jnp.where(sel1 | sel2, 1.0, 0.0)
    before = jnp.dot(member.astype(BF16), upper_ref[...], preferred_element_type=F32)
    cnt = jnp.sum(member, axis=1, keepdims=True)
    cnt_pad = jnp.ceil(cnt * (1.0 / MOE_GRANULE)) * MOE_GRANULE
    cnt_b = jnp.broadcast_to(cnt_pad, (ROUTER_COLS, ROUTER_COLS))
    seg_start = jnp.dot(lower_ref[...], cnt_b.astype(BF16), preferred_element_type=F32)
    cnt_ref[...] = cnt_b
    slot = seg_start[:, 0:1] + before
    pos1 = jnp.sum(jnp.where(sel1, slot, 0.0), axis=0, keepdims=True)
    pos2 = jnp.sum(jnp.where(sel2, slot, 0.0), axis=0, keepdims=True)
    srow = lax.broadcasted_iota(jnp.int32, (xs_ref.shape[0], tm), 0).astype(F32)
    perm = jnp.where((srow == pos1) | (srow == pos2), 1.0, 0.0).astype(BF16)
    xs_ref[...] = jnp.dot(perm, x1_hi, preferred_element_type=F32).astype(BF16)
    info = (jnp.where(row == 0, pos1, 0.0) + jnp.where(row == 1, pos2, 0.0)
            + jnp.where(row == 2, c1, 0.0) + jnp.where(row == 3, c2, 0.0))
    route_ref[...] = info.T


def _experts_kernel(gin_ref, sexp_ref, xs_hbm, wg_ref, wu_ref, wd_ref, y_ref,
                    xbuf, sems, wg_b, wu_b, wd_b, *, n_slots, n_steps):
    t = pl.program_id(0)
    expert = sexp_ref[t]
    is_expert = expert < N_EXPERTS
    cur = t % 2

    def gather(step, slot):
        for k in range(n_slots):
            src = pl.multiple_of(gin_ref[step * n_slots + k] * MOE_GRANULE, MOE_GRANULE)
            pltpu.make_async_copy(xs_hbm.at[pl.ds(src, MOE_GRANULE), :],
                                  xbuf.at[slot, pl.ds(k * MOE_GRANULE, MOE_GRANULE), :], sems.at[slot]).start()

    @pl.when((t == 0) & is_expert)
    def _():
        gather(0, 0)

    nxt = jnp.minimum(t + 1, n_steps - 1)

    @pl.when((t + 1 < n_steps) & (sexp_ref[nxt] < N_EXPERTS))
    def _():
        gather(nxt, 1 - cur)

    @pl.when(is_expert & ((t == 0) | (expert != sexp_ref[jnp.maximum(t - 1, 0)])))
    def _():
        wg_b[...] = wg_ref[...].astype(BF16)
        wu_b[...] = wu_ref[...].astype(BF16)
        wd_b[...] = wd_ref[...].astype(BF16)

    @pl.when(is_expert)
    def _():
        pltpu.make_async_copy(xbuf.at[cur], xbuf.at[cur], sems.at[cur]).wait()
        rows = (n_slots // 2) * MOE_GRANULE

        def gate_up(row0):
            xg = xbuf[cur, row0:row0 + rows, :]
            return (jnp.dot(xg, wg_b[...], preferred_element_type=F32),
                    jnp.dot(xg, wu_b[...], preferred_element_type=F32))

        def act_down(gate, up):
            hact = (gate * _sigmoid(gate) * up).astype(BF16)
            return jnp.dot(hact, wd_b[...], preferred_element_type=F32).astype(BF16)

        gate_a, up_a = gate_up(0)
        gate_b, up_b = gate_up(rows)
        y_ref[0:rows, :] = act_down(gate_a, up_a)
        y_ref[rows:, :] = act_down(gate_b, up_b)

    @pl.when(jnp.logical_not(is_expert))
    def _():
        y_ref[...] = jnp.zeros(y_ref.shape, BF16)


def _final_kernel(loc_ref, x1f_ref, route_ref, *rest, n_granules):
    g_refs = rest[:n_granules]
    p_ref, g2_ref, b2_ref, wpg_ref, bpg_ref, wple_ref, g3_ref, b3_ref, y_ref = rest[n_granules:]
    ys = jnp.concatenate([r[...] for r in g_refs], axis=0)
    tm = route_ref.shape[0]
    halves = [slice(0, tm // 2), slice(tm // 2, tm)]
    scol = lax.broadcasted_iota(jnp.int32, (tm // 2, ys.shape[0]), 1).astype(F32)

    def combine(rows):
        route = route_ref[rows, :]
        comb = (jnp.where(scol == route[:, 0:1], route[:, 2:3], 0.0)
                + jnp.where(scol == route[:, 1:2], route[:, 3:4], 0.0)).astype(BF16)
        return jnp.dot(comb, ys, preferred_element_type=F32)

    def ple(rows, moe):
        x2 = _layer_norm(DEEPNORM_ALPHA * x1f_ref[rows, :] + moe, g2_ref[...], b2_ref[...])
        z = jnp.dot(x2.astype(BF16), wpg_ref[...], preferred_element_type=F32) + bpg_ref[...]
        pe = jnp.dot(p_ref[rows, :].astype(BF16), wple_ref[...], preferred_element_type=F32)
        return x2, z, pe

    def finish(rows, x2, z, pe):
        y_ref[rows, :] = _layer_norm(DEEPNORM_ALPHA * x2 + _sigmoid(z) * pe, g3_ref[...], b3_ref[...])

    moe_a = combine(halves[0])
    moe_b = combine(halves[1])
    part_a = ple(halves[0], moe_a)
    part_b = ple(halves[1], moe_b)
    finish(halves[0], *part_a)
    finish(halves[1], *part_b)


def _full(shape):
    nd = len(shape)
    return pl.BlockSpec(shape, lambda *_: (0,) * nd)


def _cparams(sem):
    return pltpu.CompilerParams(dimension_semantics=sem, vmem_limit_bytes=V7X_VMEM_LIMIT)


def _proj_prompt(x2d, w_qkvu, b_qkvu, wg_bd, s_pool, n_seq, seq_len, tm=512):
    n = x2d.shape[0]
    tps = seq_len // tm
    row = lambda i: (i, 0)
    col = lambda i: (0, i)
    outs = pl.pallas_call(
        functools.partial(_proj_prompt_kernel, tm=tm, tiles_per_seq=tps),
        grid=(n // tm,),
        in_specs=[pl.BlockSpec((tm, D_MODEL), row), _full(w_qkvu.shape), _full(b_qkvu.shape),
                  _full(wg_bd.shape), _full(s_pool.shape)],
        out_specs=[pl.BlockSpec((D_QK, tm), col), pl.BlockSpec((tm * N_HEADS, V_DIM), row),
                   pl.BlockSpec((tm * N_HEADS, V_DIM), row), pl.BlockSpec((tm, D_QK), row),
                   pl.BlockSpec((D_ATTN, tm), col), pl.BlockSpec((tm, D_POOL), row),
                   pl.BlockSpec((None, POOL_HIST, D_POOL), lambda i: (i // tps, 0, 0))],
        out_shape=[jax.ShapeDtypeStruct((D_QK, n), BF16), jax.ShapeDtypeStruct((n * N_HEADS, V_DIM), F32),
                   jax.ShapeDtypeStruct((n * N_HEADS, V_DIM), F32), jax.ShapeDtypeStruct((n, D_QK), BF16),
                   jax.ShapeDtypeStruct((D_ATTN, n), BF16), jax.ShapeDtypeStruct((n, D_POOL), BF16),
                   jax.ShapeDtypeStruct((n_seq, POOL_HIST, D_POOL), F32)],
        scratch_shapes=[pltpu.VMEM((POOL_HIST + tm, D_POOL), F32)],
        compiler_params=_cparams(("arbitrary",)),
        name="proj_pool_prompt",
    )(x2d, w_qkvu, b_qkvu, wg_bd, s_pool)
    return outs


def _proj_sample(x2d, w_qkvu, b_qkvu, wg_bd, s_pool, state_pad, n_seq, t_new, n_past):
    n = x2d.shape[0]
    rows = POOL_HIST + t_new
    return pl.pallas_call(
        functools.partial(_proj_sample_kernel, n_seq=n_seq, t_new=t_new, n_past=n_past),
        grid=(1,),
        in_specs=[_full(x2d.shape), _full(w_qkvu.shape), _full(b_qkvu.shape), _full(wg_bd.shape),
                  _full(s_pool.shape), _full(state_pad.shape)],
        out_specs=[_full((n, D_QK)), _full((n * N_HEADS, V_DIM)), _full((n * N_HEADS, V_DIM)),
                   _full((n, D_POOL)), _full((n_seq, POOL_HIST, D_POOL))],
        out_shape=[jax.ShapeDtypeStruct((n, D_QK), F32), jax.ShapeDtypeStruct((n * N_HEADS, V_DIM), F32),
                   jax.ShapeDtypeStruct((n * N_HEADS, V_DIM), F32), jax.ShapeDtypeStruct((n, D_POOL), BF16),
                   jax.ShapeDtypeStruct((n_seq, POOL_HIST, D_POOL), F32)],
        scratch_shapes=[pltpu.VMEM((n_seq, rows, D_POOL), F32)],
        compiler_params=_cparams(("arbitrary",)),
        name="proj_pool_sample",
    )(x2d, w_qkvu, b_qkvu, wg_bd, s_pool, state_pad)


def _alibi_slopes():
    return 2.0 ** (-8.0 * jnp.arange(1, N_HEADS + 1, dtype=F32) / N_HEADS)


def _attention(qt, kb, vt, q_s, k_new, v_new, cache_k, cache_v, page_table, lam_vecs, g_sub,
               n_seq, seq_len, t_new, tq=512):
    n = kb.shape[0]
    nq = seq_len // tq
    n_pairs = nq // 2
    n_smp, n_pages = page_table.shape
    n_past = n_pages * PAGE_SIZE
    assert nq % 2 == 0 and n_smp == SAMPLES_PER_STEP * n_seq * N_HEADS * n_pairs, (n_smp, n_seq, nq)
    slopes = _alibi_slopes()
    rel = (jnp.arange(tq, dtype=jnp.int32)[None, :] - jnp.arange(tq, dtype=jnp.int32)[:, None])
    b_off = -(slopes * LOG2E)[:, None, None] * rel.astype(F32)[None]
    b_diag = jnp.where(rel[None] >= 0, b_off, NEG_INF)
    r = jnp.arange(2 * N_HEADS * t_new, dtype=jnp.int32)
    r_head, r_q = r // (2 * t_new), r % t_new
    row_slope = slopes[r_head][:, None]
    kpos = jnp.arange(n_past, dtype=jnp.int32)[None, :]
    b_past = -row_slope * (n_past + r_q[:, None] - kpos).astype(F32)
    j = jnp.arange(PAGE_SIZE, dtype=jnp.int32)[None, :]
    dist_new = r_q[:, None] - j
    b_new = jnp.where((dist_new >= 0) & (j < t_new), -row_slope * dist_new.astype(F32), NEG_INF)
    bias_s = jnp.concatenate([b_past, b_new], axis=1)
    page_rows = PAGE_SIZE * N_HEADS
    ck = cache_k.reshape(-1, V_DIM)
    cv = cache_v.reshape(-1, V_DIM)
    pt = page_table.reshape(-1).astype(jnp.int32)
    step = lambda b, h, i: (b * N_HEADS + h) * n_pairs + i
    const2 = lambda b, h, i, pt_ref: (0, 0)
    vec = pl.BlockSpec((1, HEAD_DIM), const2)
    smp_rows = SAMPLES_PER_STEP * t_new
    tok = pl.BlockSpec((smp_rows, D_QK), lambda b, h, i, pt_ref: (step(b, h, i), 0))
    tok_hm = pl.BlockSpec((smp_rows * N_HEADS, V_DIM), lambda b, h, i, pt_ref: (step(b, h, i), 0))
    table = pl.BlockSpec((None, tq, tq), lambda b, h, i, pt_ref: (h, 0, 0))

    def page_spec(sj, pi):
        return pl.BlockSpec((page_rows, V_DIM), lambda b, h, i, pt_ref: (
            pt_ref[(step(b, h, i) * SAMPLES_PER_STEP + sj) * n_pages + pi], 0))

    pages = []
    for sj in range(SAMPLES_PER_STEP):
        pages += [(page_spec(sj, pi), ck) for pi in range(n_pages)] + [(page_spec(sj, pi), cv) for pi in range(n_pages)]
    grid_spec = pltpu.PrefetchScalarGridSpec(
        num_scalar_prefetch=1,
        grid=(n_seq, N_HEADS, n_pairs),
        in_specs=[pl.BlockSpec(memory_space=pltpu.SMEM), vec, vec, vec, vec,
                  pl.BlockSpec((V_DIM, tq), lambda b, h, i, pt_ref: (h, b * nq + nq - 1 - i)),
                  pl.BlockSpec((V_DIM, tq), lambda b, h, i, pt_ref: (h, b * nq + i)),
                  pl.BlockSpec((seq_len, V_DIM), lambda b, h, i, pt_ref: (b, h)),
                  pl.BlockSpec((V_DIM, seq_len), lambda b, h, i, pt_ref: (h, b)),
                  table, table, pl.BlockSpec((1, V_DIM), const2), pl.BlockSpec(bias_s.shape, const2),
                  tok, tok_hm, tok_hm] + [spec for spec, _ in pages],
        out_specs=[pl.BlockSpec((2 * tq, V_DIM), lambda b, h, i, pt_ref: (b * n_pairs + i, h)), tok],
        scratch_shapes=[pltpu.VMEM((2, V_DIM, 2 * tq), BF16), pltpu.VMEM((2, 1, 2 * tq), F32),
                        pltpu.VMEM((2, V_DIM + DENOM_ROWS, 2 * tq), F32)],
    )
    return pl.pallas_call(
        functools.partial(_attn_kernel, n_pages=n_pages, t_new=t_new, tq=tq, nq=nq),
        grid_spec=grid_spec,
        out_shape=[jax.ShapeDtypeStruct((n, D_ATTN), BF16), jax.ShapeDtypeStruct((n_smp * t_new, D_ATTN), BF16)],
        compiler_params=_cparams(("arbitrary", "arbitrary", "arbitrary")),
        name="attention",
    )(pt, slopes, *lam_vecs, qt, qt, kb, vt, b_off, b_diag, g_sub, bias_s, q_s, k_new, v_new,
      *[arr for _, arr in pages])


def _pair_major_tile(i, nq):
    b, qi = i // nq, i % nq
    return b * nq + jnp.where(qi >= nq // 2, 2 * (nq - 1 - qi), 2 * qi + 1)


def _merge(prompt, sample, wts, attn_tile):
    tm = MOE_TILE
    tiles_p = prompt[0].shape[0] // tm
    n_tiles = tiles_p + sample[0].shape[0] // tm
    p_row = lambda i: (jnp.minimum(i, tiles_p - 1), 0)
    s_row = lambda i: (jnp.maximum(i - tiles_p, 0), 0)
    row = lambda i: (i, 0)
    consts = [wts[k] for k in ("w_gate", "b_gate", "w_pool_br", "w_attn_br", "w_o", "ln1_g", "ln1_b",
                               "wrt_hi", "wrt_lo", "b_rt", "upper", "lower")]
    return pl.pallas_call(
        functools.partial(_merge_kernel, tm=tm, prompt_tiles=tiles_p),
        grid=(n_tiles,),
        in_specs=[pl.BlockSpec((tm, D_MODEL), p_row), pl.BlockSpec((tm, D_POOL), p_row),
                  pl.BlockSpec((tm, D_ATTN), lambda i: (attn_tile(jnp.minimum(i, tiles_p - 1)), 0)),
                  pl.BlockSpec((tm, D_MODEL), s_row), pl.BlockSpec((tm, D_POOL), s_row),
                  pl.BlockSpec((tm, D_ATTN), s_row)] + [_full(c.shape) for c in consts],
        out_specs=[pl.BlockSpec((tm, D_MODEL), row), pl.BlockSpec((MOE_REGION, D_MODEL), row),
                   pl.BlockSpec((tm, ROUTER_COLS), row), pl.BlockSpec((ROUTER_COLS, ROUTER_COLS), row)],
        out_shape=[jax.ShapeDtypeStruct((n_tiles * tm, D_MODEL), F32),
                   jax.ShapeDtypeStruct((n_tiles * MOE_REGION, D_MODEL), BF16),
                   jax.ShapeDtypeStruct((n_tiles * tm, ROUTER_COLS), F32),
                   jax.ShapeDtypeStruct((n_tiles * ROUTER_COLS, ROUTER_COLS), F32)],
        compiler_params=_cparams(("arbitrary",)),
        name="merge_ln1_route_dispatch",
    )(*prompt, *sample, *consts)


def _granule_schedule(cnt_pad, n_steps):
    n_tiles = cnt_pad.shape[0]
    gc = cnt_pad // MOE_GRANULE
    seg = jnp.cumsum(gc, axis=1) - gc
    tot = jnp.sum(gc, axis=0)
    ahead = jnp.cumsum(gc, axis=0) - gc
    slots_e = ((tot + MOE_SLOTS - 1) // MOE_SLOTS) * MOE_SLOTS
    first_slot = jnp.cumsum(slots_e) - slots_e
    p0 = (jnp.arange(n_steps, dtype=jnp.int32) * MOE_SLOTS)[None, :]
    in_e = ((p0 >= first_slot[:, None]) & (p0 < (first_slot + slots_e)[:, None])).astype(jnp.int32)
    q0 = p0 - jnp.sum(in_e * first_slot[:, None], axis=0, keepdims=True)
    tot_s = jnp.sum(in_e * tot[:, None], axis=0, keepdims=True)
    e_id = jnp.sum(in_e * jnp.arange(N_EXPERTS, dtype=jnp.int32)[:, None], axis=0)
    step_expert = jnp.where(q0[0] < tot_s[0], e_id, N_EXPERTS).astype(jnp.int32)
    pick = lambda tab: jnp.sum(tab[:, :, None] * in_e[None, :, :], axis=1)[:, :, None]
    ahead_s, gc_s, seg_s = pick(ahead), pick(gc), pick(seg)
    q = q0[0][None, :, None] + jnp.arange(MOE_SLOTS, dtype=jnp.int32)[None, None, :]
    in_t = ((q >= ahead_s) & (q < ahead_s + gc_s)).astype(jnp.int32)
    tile_base = (jnp.arange(n_tiles, dtype=jnp.int32) * REGION_GRANULES)[:, None, None]
    gid = jnp.sum(in_t * (tile_base + seg_s + q - ahead_s), axis=0)
    gin = jnp.where(q[0] < tot_s[0][:, None], gid, 0).astype(jnp.int32).reshape(-1)
    s = jnp.arange(REGION_GRANULES, dtype=jnp.int32)[None, None, :]
    in_g = ((s >= seg[:, :, None]) & (s < (seg + gc)[:, :, None])).astype(jnp.int32)
    loc = jnp.sum(in_g * ((first_slot[None, :] + ahead - seg)[:, :, None] + s), axis=1)
    used = jnp.sum(in_g, axis=1) > 0
    loc = jnp.where(used, loc, loc[:, 0:1]).astype(jnp.int32)
    return gin, step_expert, loc.reshape(-1)


def _experts(xs, gin, step_expert, w_gate, w_up, w_down, n_steps):
    rows = MOE_SLOTS * MOE_GRANULE
    w_sel = lambda t, gin_ref, se_ref: (jnp.minimum(se_ref[t], N_EXPERTS - 1), 0, 0)
    grid_spec = pltpu.PrefetchScalarGridSpec(
        num_scalar_prefetch=2,
        grid=(n_steps,),
        in_specs=[pl.BlockSpec(memory_space=pl.ANY),
                  pl.BlockSpec((None, D_MODEL, D_EXPERT), w_sel), pl.BlockSpec((None, D_MODEL, D_EXPERT), w_sel),
                  pl.BlockSpec((None, D_EXPERT, D_MODEL), w_sel)],
        out_specs=pl.BlockSpec((rows, D_MODEL), lambda t, gin_ref, se_ref: (t, 0)),
        scratch_shapes=[pltpu.VMEM((2, rows, D_MODEL), BF16), pltpu.SemaphoreType.DMA((2,)),
                        pltpu.VMEM((D_MODEL, D_EXPERT), BF16), pltpu.VMEM((D_MODEL, D_EXPERT), BF16),
                        pltpu.VMEM((D_EXPERT, D_MODEL), BF16)],
    )
    return pl.pallas_call(
        functools.partial(_experts_kernel, n_slots=MOE_SLOTS, n_steps=n_steps),
        grid_spec=grid_spec,
        out_shape=jax.ShapeDtypeStruct((n_steps * rows, D_MODEL), BF16),
        compiler_params=_cparams(("arbitrary",)),
        name="expert_mlps",
    )(gin, step_expert, xs, w_gate, w_up, w_down)


def _final(x1f, route, ys, loc, p2d, wts, tile0):
    tm = MOE_TILE
    n = p2d.shape[0]
    consts = [wts[k] for k in ("ln2_g", "ln2_b", "w_ple_gate", "b_ple_gate", "w_ple", "ln3_g", "ln3_b")]

    def granule_spec(s):
        return pl.BlockSpec((MOE_GRANULE, D_MODEL),
                            lambda i, loc_ref: (loc_ref[(i + tile0) * REGION_GRANULES + s], 0))

    grid_spec = pltpu.PrefetchScalarGridSpec(
        num_scalar_prefetch=1,
        grid=(n // tm,),
        in_specs=[pl.BlockSpec((tm, D_MODEL), lambda i, loc_ref: (i + tile0, 0)),
                  pl.BlockSpec((tm, ROUTER_COLS), lambda i, loc_ref: (i + tile0, 0))]
                 + [granule_spec(s) for s in range(REGION_GRANULES)]
                 + [pl.BlockSpec((tm, D_PLE), lambda i, loc_ref: (i, 0))]
                 + [pl.BlockSpec(c.shape, lambda i, loc_ref: (0, 0)) for c in consts],
        out_specs=pl.BlockSpec((tm, D_MODEL), lambda i, loc_ref: (i, 0)),
    )
    return pl.pallas_call(
        functools.partial(_final_kernel, n_granules=REGION_GRANULES),
        grid_spec=grid_spec,
        out_shape=jax.ShapeDtypeStruct((n, D_MODEL), F32),
        compiler_params=_cparams(("arbitrary",)),
        name="combine_ln2_ple_ln3",
    )(loc, x1f, route, *([ys] * REGION_GRANULES), p2d, *consts)


def _prepare_weights(w_in, b_in, w_pool_grp, s_pool, w_pool_br, w_attn_br, w_o, ln1_g, ln1_b,
                     w_rg, b_rg, w_re, b_re, w_gate, w_up, w_down, ln2_g, ln2_b,
                     w_ple_gate, b_ple_gate, w_ple, ln3_g, ln3_b):
    i = 0
    row = lambda a: a[i].reshape(1, -1).astype(F32)
    wg_bd = jnp.zeros((D_POOL, D_POOL), F32)
    for gi in range(len(POOL_WINDOWS)):
        sl = slice(gi * POOL_GROUP, (gi + 1) * POOL_GROUP)
        wg_bd = wg_bd.at[sl, sl].set(w_pool_grp[i, gi])
    w_r = jnp.zeros((D_MODEL, ROUTER_COLS), F32)
    w_r = w_r.at[:, 0:N_EXPERT_GROUPS].set(w_rg[i])
    w_r = w_r.at[:, N_EXPERT_GROUPS:N_EXPERT_GROUPS + N_EXPERTS].set(w_re[i].reshape(D_MODEL, N_EXPERTS))
    b_r = jnp.zeros((1, ROUTER_COLS), F32)
    b_r = b_r.at[0, 0:N_EXPERT_GROUPS].set(b_rg[i])
    b_r = b_r.at[0, N_EXPERT_GROUPS:N_EXPERT_GROUPS + N_EXPERTS].set(b_re[i].reshape(-1))
    wrt_hi = w_r.T.astype(BF16)
    wrt_lo = (w_r.T - wrt_hi.astype(F32)).astype(BF16)
    tok = jnp.arange(MOE_TILE, dtype=jnp.int32)
    upper = (tok[:, None] < tok[None, :]).astype(BF16)
    rr = jnp.arange(ROUTER_COLS, dtype=jnp.int32)
    lower = (rr[None, :] < rr[:, None]).astype(BF16)
    return dict(
        w_gate_e=w_gate[i].reshape(N_EXPERTS, D_MODEL, D_EXPERT), w_up_e=w_up[i].reshape(N_EXPERTS, D_MODEL, D_EXPERT),
        w_down_e=w_down[i].reshape(N_EXPERTS, D_EXPERT, D_MODEL),
        wrt_hi=wrt_hi, wrt_lo=wrt_lo, b_rt=b_r.reshape(ROUTER_COLS, 1), upper=upper, lower=lower,
        w_qkvu=w_in[i, :, 0:D_QKVU].astype(BF16), b_qkvu=b_in[i, 0:D_QKVU].reshape(1, -1),
        w_gate=w_in[i, :, D_QKVU:].astype(BF16), b_gate=b_in[i, D_QKVU:].reshape(1, -1),
        wg_bd=wg_bd.astype(BF16), s_pool=row(s_pool),
        w_pool_br=w_pool_br[i].astype(BF16), w_attn_br=w_attn_br[i].astype(BF16), w_o=w_o[i].astype(BF16),
        ln1_g=row(ln1_g), ln1_b=row(ln1_b), ln2_g=row(ln2_g), ln2_b=row(ln2_b),
        w_ple_gate=w_ple_gate[i].astype(BF16), b_ple_gate=row(b_ple_gate), w_ple=w_ple[i].astype(BF16),
        ln3_g=row(ln3_g), ln3_b=row(ln3_b))


def kernel(x_prompt, x_sample, p_prompt, p_sample, cache_k, cache_v, state_pool, page_table, w_in, b_in, lam_q1, lam_k1, lam_q2, lam_k2, g_sub, w_pool_grp, s_pool, w_pool_br, w_attn_br, w_o, ln1_g, ln1_b, w_rg, b_rg, w_re, b_re, w_gate, w_up, w_down, ln2_g, ln2_b, w_ple_gate, b_ple_gate, w_ple, ln3_g, ln3_b):
    assert w_in.shape[0] == DEPTH == 1
    bsz, seq, _ = x_prompt.shape
    dbs, dseq, _ = x_sample.shape
    n_past = page_table.shape[1] * PAGE_SIZE
    wts = _prepare_weights(w_in, b_in, w_pool_grp, s_pool, w_pool_br, w_attn_br, w_o, ln1_g, ln1_b,
                           w_rg, b_rg, w_re, b_re, w_gate, w_up, w_down, ln2_g, ln2_b,
                           w_ple_gate, b_ple_gate, w_ple, ln3_g, ln3_b)
    lam_vecs = [a[0].reshape(1, HEAD_DIM).astype(F32) for a in (lam_q1, lam_k1, lam_q2, lam_k2)]
    g_row = g_sub[0].reshape(1, V_DIM).astype(F32)

    xp = x_prompt.reshape(bsz * seq, D_MODEL)
    qt_p, kf_p, vf_p, kb_p, vt_p, yp_p, tail_p = _proj_prompt(
        xp, wts["w_qkvu"], wts["b_qkvu"], wts["wg_bd"], wts["s_pool"], bsz, seq)

    xs = x_sample.reshape(dbs * dseq, D_MODEL)
    state_pad = jnp.pad(state_pool[0], ((0, 0), (POOL_HIST - POOL_STATE, 0), (0, 0)))
    q_s, kf_s, vf_s, yp_s, tail_s = _proj_sample(
        xs, wts["w_qkvu"], wts["b_qkvu"], wts["wg_bd"], wts["s_pool"], state_pad, dbs, dseq, n_past)

    ya_p, ya_s = _attention(qt_p, kb_p, vt_p, q_s, kf_s, vf_s, cache_k, cache_v, page_table, lam_vecs, g_row,
                            bsz, seq, dseq, tq=MOE_TILE)

    tiles_p = (bsz * seq) // MOE_TILE
    tiles_s = (dbs * dseq) // MOE_TILE
    n_tiles = tiles_p + tiles_s
    x1f, xsort, route, cnt = _merge((xp, yp_p, ya_p), (xs, yp_s, ya_s), wts,
                                    attn_tile=lambda i: _pair_major_tile(i, seq // MOE_TILE))
    cnt_pad = cnt[:, 0].reshape(n_tiles, ROUTER_COLS)[:, N_EXPERT_GROUPS:N_EXPERT_GROUPS + N_EXPERTS]
    n_steps = (n_tiles * REGION_GRANULES + N_EXPERTS * (MOE_SLOTS - 1) + MOE_SLOTS - 1) // MOE_SLOTS
    gin, step_expert, loc = _granule_schedule(cnt_pad.astype(jnp.int32), n_steps)
    ysort = _experts(xsort, gin, step_expert, wts["w_gate_e"], wts["w_up_e"], wts["w_down_e"], n_steps)
    y_p = _final(x1f, route, ysort, loc, p_prompt[0].reshape(bsz * seq, D_PLE), wts, 0)
    y_s = _final(x1f, route, ysort, loc, p_sample[0].reshape(dbs * dseq, D_PLE), wts, tiles_p)

    drop = POOL_HIST - POOL_STATE
    return (y_p.reshape(bsz, seq, D_MODEL),
            y_s.reshape(dbs, dseq, D_MODEL),
            kf_p.reshape(1, bsz, seq, N_HEADS, V_DIM),
            vf_p.reshape(1, bsz, seq, N_HEADS, V_DIM),
            tail_p[None, :, drop:, :],
            kf_s.reshape(1, dbs, dseq, N_HEADS, V_DIM),
            vf_s.reshape(1, dbs, dseq, N_HEADS, V_DIM),
            tail_s[None, :, drop:, :])
```

```python
import functools
import math

import jax
import jax.numpy as jnp
from jax import lax
from jax.experimental import pallas as pl
from jax.experimental.pallas import tpu as pltpu

F32 = jnp.float32
BF16 = jnp.bfloat16

D_MODEL = 1024
N_HEADS = 4
HEAD_DIM = 64
V_DIM = 2 * HEAD_DIM
D_QK = N_HEADS * 2 * HEAD_DIM
D_ATTN = N_HEADS * V_DIM
D_POOL = 512
POOL_WINDOWS = (2, 4, 8, 16)
POOL_GROUP = D_POOL // len(POOL_WINDOWS)
POOL_STATE = max(POOL_WINDOWS) - 1
POOL_HIST = POOL_STATE + 1
N_EXPERT_GROUPS = 4
EXPERTS_PER_GROUP = 8
N_EXPERTS = N_EXPERT_GROUPS * EXPERTS_PER_GROUP
D_EXPERT = 256
D_PLE = 256
PAGE_SIZE = 128
LN_EPS = 1e-5
DEPTH = 1
DEEPNORM_ALPHA = (2 * DEPTH) ** 0.25
LAM_INIT = 0.8 - 0.6 * math.exp(-0.3 * 0)
D_QKVU = D_POOL + 2 * D_QK + D_ATTN
ROUTER_COLS = 128
TOP_K_IN_GROUP = 2
MOE_TILE = 512
MOE_GRANULE = 16
MOE_REGION = -(-(TOP_K_IN_GROUP * MOE_TILE + N_EXPERTS * (MOE_GRANULE - 1)) // 128) * 128
REGION_GRANULES = MOE_REGION // MOE_GRANULE
MOE_SLOTS = 32
LOG2E = 1.4426950408889634
DENOM_ROWS = 16
SAMPLES_PER_STEP = 2
SCORES_AHEAD = 2
COMBINE_PARTS = 2
V7X_VMEM_LIMIT = 56 * 1024 * 1024
NEG_INF = float("-inf")


def _sigmoid(x):
    return 1.0 / (1.0 + jnp.exp(-x))


def _layer_norm(x, g, b):
    mu = jnp.mean(x, axis=-1, keepdims=True)
    xc = x - mu
    var = jnp.mean(xc * xc, axis=-1, keepdims=True)
    return xc * lax.rsqrt(var + LN_EPS) * g + b


def _lam_value(lq1_ref, lk1_ref, lq2_ref, lk2_ref):
    a = jnp.sum(lq1_ref[...] * lk1_ref[...], axis=1, keepdims=True)
    b = jnp.sum(lq2_ref[...] * lk2_ref[...], axis=1, keepdims=True)
    return jnp.exp(a) - jnp.exp(b) + LAM_INIT


def _window_sums(e):
    outs = []
    for gi, w in enumerate(POOL_WINDOWS):
        s = e[:, gi * POOL_GROUP:(gi + 1) * POOL_GROUP]
        step = 1
        while step < w:
            s = s + pltpu.roll(s, step, axis=0)
            step *= 2
        outs.append(s)
    return outs


def _pool_branch(wins, u, inv_cnts, wg_ref, sp_ref):
    ds = []
    for gi in range(len(POOL_WINDOWS)):
        ds.append(wins[gi] * inv_cnts[gi] - u[:, gi * POOL_GROUP:(gi + 1) * POOL_GROUP])
    d = jnp.concatenate(ds, axis=1).astype(BF16)
    return jnp.dot(d, wg_ref[...], preferred_element_type=F32) * sp_ref[...]


def _store_head_major(ref, x, n_rows):
    for hh in range(N_HEADS):
        ref[pl.ds(hh, n_rows, stride=N_HEADS), :] = x[:, hh * V_DIM:(hh + 1) * V_DIM]


def _proj_prompt_kernel(x_ref, w_ref, b_ref, wg_ref, sp_ref,
                        qt_ref, kf_ref, vf_ref, kb_ref, vt_ref, yp_ref, tail_ref,
                        ext_ref, *, tm, tiles_per_seq):
    t_in_seq = pl.program_id(0) % tiles_per_seq
    z = jnp.dot(x_ref[...].astype(BF16), w_ref[...], preferred_element_type=F32) + b_ref[...]
    u = z[:, 0:D_POOL]
    q = z[:, D_POOL:D_POOL + D_QK]
    k = z[:, D_POOL + D_QK:D_POOL + 2 * D_QK]
    v = z[:, D_POOL + 2 * D_QK:D_QKVU]
    qt_ref[...] = (q * (HEAD_DIM ** -0.5 * LOG2E)).T.astype(BF16)
    vt_ref[...] = v.T.astype(BF16)
    kb_ref[...] = k.astype(BF16)
    _store_head_major(kf_ref, k, tm)
    _store_head_major(vf_ref, v, tm)

    @pl.when(t_in_seq == 0)
    def _():
        ext_ref[0:POOL_HIST, :] = jnp.zeros((POOL_HIST, D_POOL), F32)

    ext_ref[POOL_HIST:POOL_HIST + tm, :] = u
    wins = [s[POOL_HIST:, :] for s in _window_sums(ext_ref[...])]
    pos = t_in_seq * tm + lax.broadcasted_iota(jnp.int32, (tm, 1), 0)
    inv_cnts = [1.0 / jnp.minimum(pos + 1, w).astype(F32) for w in POOL_WINDOWS]
    yp_ref[...] = _pool_branch(wins, u, inv_cnts, wg_ref, sp_ref).astype(BF16)
    tail = u[tm - POOL_HIST:, :]
    ext_ref[0:POOL_HIST, :] = tail
    tail_ref[...] = tail


def _proj_sample_kernel(x_ref, w_ref, b_ref, wg_ref, sp_ref, st_ref,
                        q_ref, kf_ref, vf_ref, yp_ref, tail_ref,
                        ext_ref, *, n_seq, t_new, n_past):
    z = jnp.dot(x_ref[...].astype(BF16), w_ref[...], preferred_element_type=F32) + b_ref[...]
    u = z[:, 0:D_POOL]
    q_ref[...] = z[:, D_POOL:D_POOL + D_QK] * (HEAD_DIM ** -0.5)
    _store_head_major(kf_ref, z[:, D_POOL + D_QK:D_POOL + 2 * D_QK], n_seq * t_new)
    _store_head_major(vf_ref, z[:, D_POOL + 2 * D_QK:D_QKVU], n_seq * t_new)
    rows = POOL_HIST + t_new
    ext_ref[:, 0:POOL_HIST, :] = st_ref[...]
    ext_ref[:, POOL_HIST:rows, :] = u.reshape(n_seq, t_new, D_POOL)
    e3 = ext_ref[...]
    wins = [s.reshape(n_seq, rows, POOL_GROUP)[:, POOL_HIST:, :].reshape(n_seq * t_new, POOL_GROUP)
            for s in _window_sums(e3.reshape(n_seq * rows, D_POOL))]
    inv_cnts = [1.0 / float(min(n_past + 1, w)) for w in POOL_WINDOWS]
    yp_ref[...] = _pool_branch(wins, u, inv_cnts, wg_ref, sp_ref).astype(BF16)
    tail_ref[...] = e3[:, rows - POOL_HIST:, :]


def _attn_prompt_pair(slope, lam, qt_refs, k_ref, vt_ref, boff_ref, bdiag_ref, g_ref, o_ref,
                      q2t_ref, m_ref, acc_ref, *, tq, nq):
    pair = pl.program_id(2)
    tiles = (nq - 1 - pair, pair)
    half = tq // 2
    all_lanes = slice(0, 2 * tq)
    late_lanes = (slice(half, tq), slice(tq + half, 2 * tq))
    for slot, qt_ref in enumerate(qt_refs):
        qt = qt_ref[...]
        row = lax.broadcasted_iota(jnp.int32, qt.shape, 0)
        zero = jnp.zeros_like(qt)
        q2t_ref[slot] = jnp.concatenate([jnp.where(row < HEAD_DIM, qt, zero), jnp.where(row >= HEAD_DIM, qt, zero)],
                                        axis=1)

    def both_maps(bias):
        return jnp.concatenate([bias, bias], axis=1)

    def scores(slot, k_start, n_keys, lanes, bias):
        kblk = k_ref[pl.ds(k_start, n_keys), :]
        return jnp.dot(kblk, q2t_ref[slot, :, lanes], preferred_element_type=F32) + bias

    def absorb(t, slot, k_start, n_keys, lanes, shift, first):
        vt_aug = jnp.concatenate([vt_ref[:, pl.ds(k_start, n_keys)], jnp.ones((DENOM_ROWS, n_keys), BF16)], axis=0)
        m_new = jnp.max(t, axis=0, keepdims=True) + shift
        if not first:
            m_prev = m_ref[slot, :, lanes]
            m_new = jnp.maximum(m_prev, m_new)
        pv = jnp.dot(vt_aug, jnp.exp2(t - (m_new - shift)).astype(BF16), preferred_element_type=F32)
        acc_ref[slot, :, lanes] = pv if first else jnp.exp2(m_prev - m_new) * acc_ref[slot, :, lanes] + pv
        m_ref[slot, :, lanes] = m_new

    zero_shift = jnp.float32(0.0)
    tasks = []
    for slot in range(2):
        k_diag = pl.multiple_of(tiles[slot] * tq, tq)
        k_diag2 = pl.multiple_of(k_diag + half, half)
        tasks.append((slot, k_diag, half, all_lanes, lambda: both_maps(bdiag_ref[0:half, :]), zero_shift, True))
        for lanes in late_lanes:
            tasks.append((slot, k_diag2, half, lanes, lambda: bdiag_ref[half:tq, half:tq], zero_shift, False))
    for j in range(nq - 1):
        in_second = j >= tiles[0]
        kb = jnp.where(in_second, j - tiles[0], j)
        q_tile = jnp.where(in_second, tiles[1], tiles[0])
        shift = -(slope * LOG2E) * ((q_tile - kb) * tq).astype(F32)
        tasks.append((in_second.astype(jnp.int32), pl.multiple_of(kb * tq, tq), tq, all_lanes,
                      lambda: both_maps(boff_ref[...]), shift, False))
    queue = []
    for slot, k_start, n_keys, lanes, bias_fn, shift, first in tasks:
        queue.append((scores(slot, k_start, n_keys, lanes, bias_fn()), slot, k_start, n_keys, lanes, shift, first))
        if len(queue) > SCORES_AHEAD:
            absorb(*queue.pop(0))
    while queue:
        absorb(*queue.pop(0))

    for slot in range(2):
        acc = acc_ref[slot]
        on = acc[0:V_DIM, :] / acc[V_DIM:V_DIM + 1, :]
        o = (on[:, 0:tq] - lam * on[:, tq:]).T
        o = o * lax.rsqrt(jnp.mean(o * o, axis=1, keepdims=True) + LN_EPS) * g_ref[...] * (1.0 - LAM_INIT)
        o_ref[slot * tq:(slot + 1) * tq, :] = o.astype(BF16)


def _attn_sample_body(lam, sj, q_ref, kn_ref, vn_ref, bias_ref, g_ref, k_pages, v_pages, *, t_new):
    n_pages = len(k_pages)
    n_past = n_pages * PAGE_SIZE
    rows = slice(sj * t_new, (sj + 1) * t_new)
    q = q_ref[rows, :]
    lane = lax.broadcasted_iota(jnp.int32, (t_new, V_DIM), 1)
    pad = jnp.zeros((PAGE_SIZE - t_new, V_DIM), BF16)
    nt_dims = (((1,), (1,)), ((), ()))

    def head_rows(pages, new_ref, hh):
        past = jnp.concatenate([pg[pl.ds(hh, PAGE_SIZE, stride=N_HEADS), :] for pg in pages], axis=0)
        new = new_ref[pl.ds(sj * t_new * N_HEADS + hh, t_new, stride=N_HEADS), :]
        new = jnp.concatenate([new.astype(BF16), pad], axis=0)
        return past.astype(BF16), new

    s_rows = []
    for hh in range(N_HEADS):
        qh = q[:, hh * V_DIM:(hh + 1) * V_DIM]
        qh2 = jnp.concatenate([jnp.where(lane < HEAD_DIM, qh, 0.0), jnp.where(lane >= HEAD_DIM, qh, 0.0)],
                              axis=0).astype(BF16)
        k_past, k_new = head_rows(k_pages, kn_ref, hh)
        s_rows.append(jnp.concatenate(
            [lax.dot_general(qh2, k_past, nt_dims, preferred_element_type=F32),
             lax.dot_general(qh2, k_new, nt_dims, preferred_element_type=F32)], axis=1))
    s = jnp.concatenate(s_rows, axis=0) + bias_ref[...]
    m = jnp.max(s, axis=1, keepdims=True)
    p = jnp.exp(s - m)
    p = p / jnp.sum(p, axis=1, keepdims=True)
    outs = []
    for h0 in range(0, N_HEADS, 2):
        a2 = jnp.concatenate(
            [p[hh * 2 * t_new:hh * 2 * t_new + t_new, :] - lam * p[hh * 2 * t_new + t_new:(hh + 1) * 2 * t_new, :]
             for hh in (h0, h0 + 1)], axis=0).astype(BF16)
        (vp0, vn0), (vp1, vn1) = head_rows(v_pages, vn_ref, h0), head_rows(v_pages, vn_ref, h0 + 1)
        o2 = (jnp.dot(a2[:, 0:n_past], jnp.concatenate([vp0, vp1], axis=1), preferred_element_type=F32)
              + jnp.dot(a2[:, n_past:], jnp.concatenate([vn0, vn1], axis=1), preferred_element_type=F32))
        for k in range(2):
            oh = o2[k * t_new:(k + 1) * t_new, k * V_DIM:(k + 1) * V_DIM]
            oh = oh * lax.rsqrt(jnp.mean(oh * oh, axis=1, keepdims=True) + LN_EPS) * g_ref[...] * (1.0 - LAM_INIT)
            outs.append(oh)
    return jnp.concatenate(outs, axis=1)


def _attn_kernel(pt_ref, slopes_ref, lq1_ref, lk1_ref, lq2_ref, lk2_ref,
                 qta_ref, qtb_ref, k_ref, vt_ref, boff_ref, bdiag_ref, g_ref, bias_s_ref, *rest,
                 n_pages, t_new, tq, nq):
    qs_ref, kn_ref, vn_ref = rest[:3]
    pages = rest[3:3 + SAMPLES_PER_STEP * 2 * n_pages]
    op_ref, os_ref, q2t_ref, m_ref, acc_ref = rest[3 + SAMPLES_PER_STEP * 2 * n_pages:]
    lam = _lam_value(lq1_ref, lk1_ref, lq2_ref, lk2_ref)
    def sample(sj):
        pg = pages[sj * 2 * n_pages:(sj + 1) * 2 * n_pages]
        return _attn_sample_body(lam, sj, qs_ref, kn_ref, vn_ref, bias_s_ref, g_ref, pg[:n_pages], pg[n_pages:],
                                 t_new=t_new)

    outs = [sample(sj) for sj in range(SAMPLES_PER_STEP // 2)]
    _attn_prompt_pair(slopes_ref[pl.program_id(1)], lam, (qta_ref, qtb_ref), k_ref, vt_ref, boff_ref, bdiag_ref,
                      g_ref, op_ref, q2t_ref, m_ref, acc_ref, tq=tq, nq=nq)
    outs += [sample(sj) for sj in range(SAMPLES_PER_STEP // 2, SAMPLES_PER_STEP)]
    os_ref[...] = jnp.concatenate(outs, axis=0).astype(BF16)


def _merge_kernel(xp_ref, ypp_ref, yap_ref, xs_in_ref, yps_ref, yas_ref,
                  wgate_ref, bgate_ref, wpb_ref, wab_ref, wo_ref,
                  g1_ref, b1_ref, wrt_hi_ref, wrt_lo_ref, brt_ref, upper_ref, lower_ref,
                  x1f_ref, xs_ref, route_ref, cnt_ref, *, tm, prompt_tiles):
    is_sample = pl.program_id(0) >= prompt_tiles
    x = jnp.where(is_sample, xs_in_ref[...], xp_ref[...])
    y_pool = jnp.where(is_sample, yps_ref[...], ypp_ref[...])
    y_attn = jnp.where(is_sample, yas_ref[...], yap_ref[...])
    zg = jnp.dot(x.astype(BF16), wgate_ref[...], preferred_element_type=F32) + bgate_ref[...]
    gp = _sigmoid(zg[:, 0:D_MODEL])
    ga = _sigmoid(zg[:, D_MODEL:])
    merged = (gp * jnp.dot(y_pool, wpb_ref[...], preferred_element_type=F32)
              + ga * jnp.dot(y_attn, wab_ref[...], preferred_element_type=F32))
    x1 = _layer_norm(DEEPNORM_ALPHA * x + jnp.dot(merged.astype(BF16), wo_ref[...], preferred_element_type=F32),
                     g1_ref[...], b1_ref[...])
    x1f_ref[...] = x1
    x1_hi = x1.astype(BF16)
    x1_lo = (x1 - x1_hi.astype(F32)).astype(BF16)
    nt_dims = (((1,), (1,)), ((), ()))
    lg = (lax.dot_general(wrt_hi_ref[...], x1_hi, nt_dims, preferred_element_type=F32)
          + lax.dot_general(wrt_hi_ref[...], x1_lo, nt_dims, preferred_element_type=F32)
          + lax.dot_general(wrt_lo_ref[...], x1_hi, nt_dims, preferred_element_type=F32)) + brt_ref[...]
    row = lax.broadcasted_iota(jnp.int32, lg.shape, 0)
    big = jnp.int32(4 * ROUTER_COLS)
    lgg = jnp.where(row < N_EXPERT_GROUPS, lg, NEG_INF)
    gmax = jnp.max(lgg, axis=0, keepdims=True)
    g_w = 1.0 / jnp.sum(jnp.exp(lgg - gmax), axis=0, keepdims=True)
    g_idx = jnp.min(jnp.where(lgg == gmax, row, big), axis=0, keepdims=True)
    lo_row = N_EXPERT_GROUPS + g_idx * EXPERTS_PER_GROUP
    le = jnp.where((row >= lo_row) & (row < lo_row + EXPERTS_PER_GROUP), lg, NEG_INF)
    v1 = jnp.max(le, axis=0, keepdims=True)
    i1 = jnp.min(jnp.where(le == v1, row, big), axis=0, keepdims=True)
    le2 = jnp.where(row == i1, NEG_INF, le)
    v2 = jnp.max(le2, axis=0, keepdims=True)
    i2 = jnp.min(jnp.where(le2 == v2, row, big), axis=0, keepdims=True)
    e21 = jnp.exp(v2 - v1)
    c1 = g_w / (1.0 + e21)
    c2 = g_w * e21 / (1.0 + e21)
    sel1 = row == i1
    sel2 = row == i2
    member = jnp.where(sel1 | sel2, 1.0, 0.0)
    before = jnp.dot(member.astype(BF16), upper_ref[...], preferred_element_type=F32)
    cnt = jnp.sum(member, axis=1, keepdims=True)
    cnt_pad = jnp.ceil(cnt * (1.0 / MOE_GRANULE)) * MOE_GRANULE
    cnt_b = jnp.broadcast_to(cnt_pad, (ROUTER_COLS, ROUTER_COLS))
    seg_start = jnp.dot(lower_ref[...], cnt_b.astype(BF16), preferred_element_type=F32)
    cnt_ref[...] = cnt_b
    slot = seg_start[:, 0:1] + before
    pos1 = jnp.sum(jnp.where(sel1, slot, 0.0), axis=0, keepdims=True)
    pos2 = jnp.sum(jnp.where(sel2, slot, 0.0), axis=0, keepdims=True)
    srow = lax.broadcasted_iota(jnp.int32, (xs_ref.shape[0], tm), 0).astype(F32)
    perm = jnp.where((srow == pos1) | (srow == pos2), 1.0, 0.0).astype(BF16)
    xs_ref[...] = jnp.dot(perm, x1_hi, preferred_element_type=F32).astype(BF16)
    info = (jnp.where(row == 0, pos1, 0.0) + jnp.where(row == 1, pos2, 0.0)
            + jnp.where(row == 2, c1, 0.0) + jnp.where(row == 3, c2, 0.0))
    route_ref[...] = info.T


def _experts_kernel(gin_ref, sexp_ref, xs_hbm, wg_ref, wu_ref, wd_ref, y_ref,
                    xbuf, sems, wg_b, wu_b, wd_b, *, n_slots, n_steps):
    t = pl.program_id(0)
    expert = sexp_ref[t]
    is_expert = expert < N_EXPERTS
    cur = t % 2

    def gather(step, slot):
        for k in range(n_slots):
            src = pl.multiple_of(gin_ref[step * n_slots + k] * MOE_GRANULE, MOE_GRANULE)
            pltpu.make_async_copy(xs_hbm.at[pl.ds(src, MOE_GRANULE), :],
                                  xbuf.at[slot, pl.ds(k * MOE_GRANULE, MOE_GRANULE), :], sems.at[slot]).start()

    @pl.when((t == 0) & is_expert)
    def _():
        gather(0, 0)

    nxt = jnp.minimum(t + 1, n_steps - 1)

    @pl.when((t + 1 < n_steps) & (sexp_ref[nxt] < N_EXPERTS))
    def _():
        gather(nxt, 1 - cur)

    @pl.when(is_expert & ((t == 0) | (expert != sexp_ref[jnp.maximum(t - 1, 0)])))
    def _():
        wg_b[...] = wg_ref[...].astype(BF16)
        wu_b[...] = wu_ref[...].astype(BF16)
        wd_b[...] = wd_ref[...].astype(BF16)

    @pl.when(is_expert)
    def _():
        pltpu.make_async_copy(xbuf.at[cur], xbuf.at[cur], sems.at[cur]).wait()
        rows = (n_slots // 2) * MOE_GRANULE

        def gate_up(row0):
            xg = xbuf[cur, row0:row0 + rows, :]
            return (jnp.dot(xg, wg_b[...], preferred_element_type=F32),
                    jnp.dot(xg, wu_b[...], preferred_element_type=F32))

        def act_down(gate, up):
            hact = (gate * _sigmoid(gate) * up).astype(BF16)
            return jnp.dot(hact, wd_b[...], preferred_element_type=F32).astype(BF16)

        gate_a, up_a = gate_up(0)
        gate_b, up_b = gate_up(rows)
        y_ref[0:rows, :] = act_down(gate_a, up_a)
        y_ref[rows:, :] = act_down(gate_b, up_b)

    @pl.when(jnp.logical_not(is_expert))
    def _():
        y_ref[...] = jnp.zeros(y_ref.shape, BF16)


def _final_kernel(loc_ref, x1f_ref, route_ref, *rest, n_granules):
    g_refs = rest[:n_granules]
    p_ref, g2_ref, b2_ref, wpg_ref, bpg_ref, wple_ref, g3_ref, b3_ref, y_ref = rest[n_granules:]
    ys = jnp.concatenate([r[...] for r in g_refs], axis=0)
    tm = route_ref.shape[0]
    part = tm // COMBINE_PARTS
    parts = [slice(k * part, (k + 1) * part) for k in range(COMBINE_PARTS)]
    scol = lax.broadcasted_iota(jnp.int32, (part, ys.shape[0]), 1).astype(F32)

    def combine(rows):
        route = route_ref[rows, :]
        comb = (jnp.where(scol == route[:, 0:1], route[:, 2:3], 0.0)
                + jnp.where(scol == route[:, 1:2], route[:, 3:4], 0.0)).astype(BF16)
        return jnp.dot(comb, ys, preferred_element_type=F32)

    def ple(rows, moe):
        x2 = _layer_norm(DEEPNORM_ALPHA * x1f_ref[rows, :] + moe, g2_ref[...], b2_ref[...])
        z = jnp.dot(x2.astype(BF16), wpg_ref[...], preferred_element_type=F32) + bpg_ref[...]
        pe = jnp.dot(p_ref[rows, :].astype(BF16), wple_ref[...], preferred_element_type=F32)
        return x2, z, pe

    def finish(rows, x2, z, pe):
        y_ref[rows, :] = _layer_norm(DEEPNORM_ALPHA * x2 + _sigmoid(z) * pe, g3_ref[...], b3_ref[...])

    moe, mid = {}, {}
    for k in range(COMBINE_PARTS + 2):
        if k < COMBINE_PARTS:
            moe[k] = combine(parts[k])
        if 0 <= k - 1 < COMBINE_PARTS:
            mid[k - 1] = ple(parts[k - 1], moe.pop(k - 1))
        if 0 <= k - 2 < COMBINE_PARTS:
            finish(parts[k - 2], *mid.pop(k - 2))


def _full(shape):
    nd = len(shape)
    return pl.BlockSpec(shape, lambda *_: (0,) * nd)


def _cparams(sem):
    return pltpu.CompilerParams(dimension_semantics=sem, vmem_limit_bytes=V7X_VMEM_LIMIT)


def _proj_prompt(x2d, w_qkvu, b_qkvu, wg_bd, s_pool, n_seq, seq_len, tm=512):
    n = x2d.shape[0]
    tps = seq_len // tm
    row = lambda i: (i, 0)
    col = lambda i: (0, i)
    outs = pl.pallas_call(
        functools.partial(_proj_prompt_kernel, tm=tm, tiles_per_seq=tps),
        grid=(n // tm,),
        in_specs=[pl.BlockSpec((tm, D_MODEL), row), _full(w_qkvu.shape), _full(b_qkvu.shape),
                  _full(wg_bd.shape), _full(s_pool.shape)],
        out_specs=[pl.BlockSpec((D_QK, tm), col), pl.BlockSpec((tm * N_HEADS, V_DIM), row),
                   pl.BlockSpec((tm * N_HEADS, V_DIM), row), pl.BlockSpec((tm, D_QK), row),
                   pl.BlockSpec((D_ATTN, tm), col), pl.BlockSpec((tm, D_POOL), row),
                   pl.BlockSpec((None, POOL_HIST, D_POOL), lambda i: (i // tps, 0, 0))],
        out_shape=[jax.ShapeDtypeStruct((D_QK, n), BF16), jax.ShapeDtypeStruct((n * N_HEADS, V_DIM), F32),
                   jax.ShapeDtypeStruct((n * N_HEADS, V_DIM), F32), jax.ShapeDtypeStruct((n, D_QK), BF16),
                   jax.ShapeDtypeStruct((D_ATTN, n), BF16), jax.ShapeDtypeStruct((n, D_POOL), BF16),
                   jax.ShapeDtypeStruct((n_seq, POOL_HIST, D_POOL), F32)],
        scratch_shapes=[pltpu.VMEM((POOL_HIST + tm, D_POOL), F32)],
        compiler_params=_cparams(("arbitrary",)),
        name="proj_pool_prompt",
    )(x2d, w_qkvu, b_qkvu, wg_bd, s_pool)
    return outs


def _proj_sample(x2d, w_qkvu, b_qkvu, wg_bd, s_pool, state_pad, n_seq, t_new, n_past):
    n = x2d.shape[0]
    rows = POOL_HIST + t_new
    return pl.pallas_call(
        functools.partial(_proj_sample_kernel, n_seq=n_seq, t_new=t_new, n_past=n_past),
        grid=(1,),
        in_specs=[_full(x2d.shape), _full(w_qkvu.shape), _full(b_qkvu.shape), _full(wg_bd.shape),
                  _full(s_pool.shape), _full(state_pad.shape)],
        out_specs=[_full((n, D_QK)), _full((n * N_HEADS, V_DIM)), _full((n * N_HEADS, V_DIM)),
                   _full((n, D_POOL)), _full((n_seq, POOL_HIST, D_POOL))],
        out_shape=[jax.ShapeDtypeStruct((n, D_QK), F32), jax.ShapeDtypeStruct((n * N_HEADS, V_DIM), F32),
                   jax.ShapeDtypeStruct((n * N_HEADS, V_DIM), F32), jax.ShapeDtypeStruct((n, D_POOL), BF16),
                   jax.ShapeDtypeStruct((n_seq, POOL_HIST, D_POOL), F32)],
        scratch_shapes=[pltpu.VMEM((n_seq, rows, D_POOL), F32)],
        compiler_params=_cparams(("arbitrary",)),
        name="proj_pool_sample",
    )(x2d, w_qkvu, b_qkvu, wg_bd, s_pool, state_pad)


def _alibi_slopes():
    return 2.0 ** (-8.0 * jnp.arange(1, N_HEADS + 1, dtype=F32) / N_HEADS)


def _attention(qt, kb, vt, q_s, k_new, v_new, cache_k, cache_v, page_table, lam_vecs, g_sub,
               n_seq, seq_len, t_new, tq=512):
    n = kb.shape[0]
    nq = seq_len // tq
    n_pairs = nq // 2
    n_smp, n_pages = page_table.shape
    n_past = n_pages * PAGE_SIZE
    assert nq % 2 == 0 and n_smp == SAMPLES_PER_STEP * n_seq * N_HEADS * n_pairs, (n_smp, n_seq, nq)
    slopes = _alibi_slopes()
    rel = (jnp.arange(tq, dtype=jnp.int32)[None, :] - jnp.arange(tq, dtype=jnp.int32)[:, None])
    b_off = -(slopes * LOG2E)[:, None, None] * rel.astype(F32)[None]
    b_diag = jnp.where(rel[None] >= 0, b_off, NEG_INF)
    r = jnp.arange(2 * N_HEADS * t_new, dtype=jnp.int32)
    r_head, r_q = r // (2 * t_new), r % t_new
    row_slope = slopes[r_head][:, None]
    kpos = jnp.arange(n_past, dtype=jnp.int32)[None, :]
    b_past = -row_slope * (n_past + r_q[:, None] - kpos).astype(F32)
    j = jnp.arange(PAGE_SIZE, dtype=jnp.int32)[None, :]
    dist_new = r_q[:, None] - j
    b_new = jnp.where((dist_new >= 0) & (j < t_new), -row_slope * dist_new.astype(F32), NEG_INF)
    bias_s = jnp.concatenate([b_past, b_new], axis=1)
    page_rows = PAGE_SIZE * N_HEADS
    ck = cache_k.reshape(-1, V_DIM)
    cv = cache_v.reshape(-1, V_DIM)
    pt = page_table.reshape(-1).astype(jnp.int32)
    step = lambda b, h, i: (b * N_HEADS + h) * n_pairs + i
    const2 = lambda b, h, i, pt_ref: (0, 0)
    vec = pl.BlockSpec((1, HEAD_DIM), const2)
    smp_rows = SAMPLES_PER_STEP * t_new
    tok = pl.BlockSpec((smp_rows, D_QK), lambda b, h, i, pt_ref: (step(b, h, i), 0))
    tok_hm = pl.BlockSpec((smp_rows * N_HEADS, V_DIM), lambda b, h, i, pt_ref: (step(b, h, i), 0))
    table = pl.BlockSpec((None, tq, tq), lambda b, h, i, pt_ref: (h, 0, 0))

    def page_spec(sj, pi):
        return pl.BlockSpec((page_rows, V_DIM), lambda b, h, i, pt_ref: (
            pt_ref[(step(b, h, i) * SAMPLES_PER_STEP + sj) * n_pages + pi], 0))

    pages = []
    for sj in range(SAMPLES_PER_STEP):
        pages += [(page_spec(sj, pi), ck) for pi in range(n_pages)] + [(page_spec(sj, pi), cv) for pi in range(n_pages)]
    grid_spec = pltpu.PrefetchScalarGridSpec(
        num_scalar_prefetch=1,
        grid=(n_seq, N_HEADS, n_pairs),
        in_specs=[pl.BlockSpec(memory_space=pltpu.SMEM), vec, vec, vec, vec,
                  pl.BlockSpec((V_DIM, tq), lambda b, h, i, pt_ref: (h, b * nq + nq - 1 - i)),
                  pl.BlockSpec((V_DIM, tq), lambda b, h, i, pt_ref: (h, b * nq + i)),
                  pl.BlockSpec((seq_len, V_DIM), lambda b, h, i, pt_ref: (b, h)),
                  pl.BlockSpec((V_DIM, seq_len), lambda b, h, i, pt_ref: (h, b)),
                  table, table, pl.BlockSpec((1, V_DIM), const2), pl.BlockSpec(bias_s.shape, const2),
                  tok, tok_hm, tok_hm] + [spec for spec, _ in pages],
        out_specs=[pl.BlockSpec((2 * tq, V_DIM), lambda b, h, i, pt_ref: (b * n_pairs + i, h)), tok],
        scratch_shapes=[pltpu.VMEM((2, V_DIM, 2 * tq), BF16), pltpu.VMEM((2, 1, 2 * tq), F32),
                        pltpu.VMEM((2, V_DIM + DENOM_ROWS, 2 * tq), F32)],
    )
    return pl.pallas_call(
        functools.partial(_attn_kernel, n_pages=n_pages, t_new=t_new, tq=tq, nq=nq),
        grid_spec=grid_spec,
        out_shape=[jax.ShapeDtypeStruct((n, D_ATTN), BF16), jax.ShapeDtypeStruct((n_smp * t_new, D_ATTN), BF16)],
        compiler_params=_cparams(("arbitrary", "arbitrary", "arbitrary")),
        name="attention",
    )(pt, slopes, *lam_vecs, qt, qt, kb, vt, b_off, b_diag, g_sub, bias_s, q_s, k_new, v_new,
      *[arr for _, arr in pages])


def _pair_major_tile(i, nq):
    b, qi = i // nq, i % nq
    return b * nq + jnp.where(qi >= nq // 2, 2 * (nq - 1 - qi), 2 * qi + 1)


def _merge(prompt, sample, wts, attn_tile):
    tm = MOE_TILE
    tiles_p = prompt[0].shape[0] // tm
    n_tiles = tiles_p + sample[0].shape[0] // tm
    p_row = lambda i: (jnp.minimum(i, tiles_p - 1), 0)
    s_row = lambda i: (jnp.maximum(i - tiles_p, 0), 0)
    row = lambda i: (i, 0)
    consts = [wts[k] for k in ("w_gate", "b_gate", "w_pool_br", "w_attn_br", "w_o", "ln1_g", "ln1_b",
                               "wrt_hi", "wrt_lo", "b_rt", "upper", "lower")]
    return pl.pallas_call(
        functools.partial(_merge_kernel, tm=tm, prompt_tiles=tiles_p),
        grid=(n_tiles,),
        in_specs=[pl.BlockSpec((tm, D_MODEL), p_row), pl.BlockSpec((tm, D_POOL), p_row),
                  pl.BlockSpec((tm, D_ATTN), lambda i: (attn_tile(jnp.minimum(i, tiles_p - 1)), 0)),
                  pl.BlockSpec((tm, D_MODEL), s_row), pl.BlockSpec((tm, D_POOL), s_row),
                  pl.BlockSpec((tm, D_ATTN), s_row)] + [_full(c.shape) for c in consts],
        out_specs=[pl.BlockSpec((tm, D_MODEL), row), pl.BlockSpec((MOE_REGION, D_MODEL), row),
                   pl.BlockSpec((tm, ROUTER_COLS), row), pl.BlockSpec((ROUTER_COLS, ROUTER_COLS), row)],
        out_shape=[jax.ShapeDtypeStruct((n_tiles * tm, D_MODEL), F32),
                   jax.ShapeDtypeStruct((n_tiles * MOE_REGION, D_MODEL), BF16),
                   jax.ShapeDtypeStruct((n_tiles * tm, ROUTER_COLS), F32),
                   jax.ShapeDtypeStruct((n_tiles * ROUTER_COLS, ROUTER_COLS), F32)],
        compiler_params=_cparams(("arbitrary",)),
        name="merge_ln1_route_dispatch",
    )(*prompt, *sample, *consts)


def _granule_schedule(cnt_pad, n_steps):
    n_tiles = cnt_pad.shape[0]
    gc = cnt_pad // MOE_GRANULE
    seg = jnp.cumsum(gc, axis=1) - gc
    tot = jnp.sum(gc, axis=0)
    ahead = jnp.cumsum(gc, axis=0) - gc
    slots_e = ((tot + MOE_SLOTS - 1) // MOE_SLOTS) * MOE_SLOTS
    first_slot = jnp.cumsum(slots_e) - slots_e
    p0 = (jnp.arange(n_steps, dtype=jnp.int32) * MOE_SLOTS)[None, :]
    in_e = ((p0 >= first_slot[:, None]) & (p0 < (first_slot + slots_e)[:, None])).astype(jnp.int32)
    q0 = p0 - jnp.sum(in_e * first_slot[:, None], axis=0, keepdims=True)
    tot_s = jnp.sum(in_e * tot[:, None], axis=0, keepdims=True)
    e_id = jnp.sum(in_e * jnp.arange(N_EXPERTS, dtype=jnp.int32)[:, None], axis=0)
    step_expert = jnp.where(q0[0] < tot_s[0], e_id, N_EXPERTS).astype(jnp.int32)
    pick = lambda tab: jnp.sum(tab[:, :, None] * in_e[None, :, :], axis=1)[:, :, None]
    ahead_s, gc_s, seg_s = pick(ahead), pick(gc), pick(seg)
    q = q0[0][None, :, None] + jnp.arange(MOE_SLOTS, dtype=jnp.int32)[None, None, :]
    in_t = ((q >= ahead_s) & (q < ahead_s + gc_s)).astype(jnp.int32)
    tile_base = (jnp.arange(n_tiles, dtype=jnp.int32) * REGION_GRANULES)[:, None, None]
    gid = jnp.sum(in_t * (tile_base + seg_s + q - ahead_s), axis=0)
    gin = jnp.where(q[0] < tot_s[0][:, None], gid, 0).astype(jnp.int32).reshape(-1)
    s = jnp.arange(REGION_GRANULES, dtype=jnp.int32)[None, None, :]
    in_g = ((s >= seg[:, :, None]) & (s < (seg + gc)[:, :, None])).astype(jnp.int32)
    loc = jnp.sum(in_g * ((first_slot[None, :] + ahead - seg)[:, :, None] + s), axis=1)
    used = jnp.sum(in_g, axis=1) > 0
    loc = jnp.where(used, loc, loc[:, 0:1]).astype(jnp.int32)
    return gin, step_expert, loc.reshape(-1)


def _experts(xs, gin, step_expert, w_gate, w_up, w_down, n_steps):
    rows = MOE_SLOTS * MOE_GRANULE
    w_sel = lambda t, gin_ref, se_ref: (jnp.minimum(se_ref[t], N_EXPERTS - 1), 0, 0)
    grid_spec = pltpu.PrefetchScalarGridSpec(
        num_scalar_prefetch=2,
        grid=(n_steps,),
        in_specs=[pl.BlockSpec(memory_space=pl.ANY),
                  pl.BlockSpec((None, D_MODEL, D_EXPERT), w_sel), pl.BlockSpec((None, D_MODEL, D_EXPERT), w_sel),
                  pl.BlockSpec((None, D_EXPERT, D_MODEL), w_sel)],
        out_specs=pl.BlockSpec((rows, D_MODEL), lambda t, gin_ref, se_ref: (t, 0)),
        scratch_shapes=[pltpu.VMEM((2, rows, D_MODEL), BF16), pltpu.SemaphoreType.DMA((2,)),
                        pltpu.VMEM((D_MODEL, D_EXPERT), BF16), pltpu.VMEM((D_MODEL, D_EXPERT), BF16),
                        pltpu.VMEM((D_EXPERT, D_MODEL), BF16)],
    )
    return pl.pallas_call(
        functools.partial(_experts_kernel, n_slots=MOE_SLOTS, n_steps=n_steps),
        grid_spec=grid_spec,
        out_shape=jax.ShapeDtypeStruct((n_steps * rows, D_MODEL), BF16),
        compiler_params=_cparams(("arbitrary",)),
        name="expert_mlps",
    )(gin, step_expert, xs, w_gate, w_up, w_down)


def _final(x1f, route, ys, loc, p2d, wts, tile0):
    tm = MOE_TILE
    n = p2d.shape[0]
    consts = [wts[k] for k in ("ln2_g", "ln2_b", "w_ple_gate", "b_ple_gate", "w_ple", "ln3_g", "ln3_b")]

    def granule_spec(s):
        return pl.BlockSpec((MOE_GRANULE, D_MODEL),
                            lambda i, loc_ref: (loc_ref[(i + tile0) * REGION_GRANULES + s], 0))

    grid_spec = pltpu.PrefetchScalarGridSpec(
        num_scalar_prefetch=1,
        grid=(n // tm,),
        in_specs=[pl.BlockSpec((tm, D_MODEL), lambda i, loc_ref: (i + tile0, 0)),
                  pl.BlockSpec((tm, ROUTER_COLS), lambda i, loc_ref: (i + tile0, 0))]
                 + [granule_spec(s) for s in range(REGION_GRANULES)]
                 + [pl.BlockSpec((tm, D_PLE), lambda i, loc_ref: (i, 0))]
                 + [pl.BlockSpec(c.shape, lambda i, loc_ref: (0, 0)) for c in consts],
        out_specs=pl.BlockSpec((tm, D_MODEL), lambda i, loc_ref: (i, 0)),
    )
    return pl.pallas_call(
        functools.partial(_final_kernel, n_granules=REGION_GRANULES),
        grid_spec=grid_spec,
        out_shape=jax.ShapeDtypeStruct((n, D_MODEL), F32),
        compiler_params=_cparams(("arbitrary",)),
        name="combine_ln2_ple_ln3",
    )(loc, x1f, route, *([ys] * REGION_GRANULES), p2d, *consts)


def _prepare_weights(w_in, b_in, w_pool_grp, s_pool, w_pool_br, w_attn_br, w_o, ln1_g, ln1_b,
                     w_rg, b_rg, w_re, b_re, w_gate, w_up, w_down, ln2_g, ln2_b,
                     w_ple_gate, b_ple_gate, w_ple, ln3_g, ln3_b):
    i = 0
    row = lambda a: a[i].reshape(1, -1).astype(F32)
    wg_bd = jnp.zeros((D_POOL, D_POOL), F32)
    for gi in range(len(POOL_WINDOWS)):
        sl = slice(gi * POOL_GROUP, (gi + 1) * POOL_GROUP)
        wg_bd = wg_bd.at[sl, sl].set(w_pool_grp[i, gi])
    w_r = jnp.zeros((D_MODEL, ROUTER_COLS), F32)
    w_r = w_r.at[:, 0:N_EXPERT_GROUPS].set(w_rg[i])
    w_r = w_r.at[:, N_EXPERT_GROUPS:N_EXPERT_GROUPS + N_EXPERTS].set(w_re[i].reshape(D_MODEL, N_EXPERTS))
    b_r = jnp.zeros((1, ROUTER_COLS), F32)
    b_r = b_r.at[0, 0:N_EXPERT_GROUPS].set(b_rg[i])
    b_r = b_r.at[0, N_EXPERT_GROUPS:N_EXPERT_GROUPS + N_EXPERTS].set(b_re[i].reshape(-1))
    wrt_hi = w_r.T.astype(BF16)
    wrt_lo = (w_r.T - wrt_hi.astype(F32)).astype(BF16)
    tok = jnp.arange(MOE_TILE, dtype=jnp.int32)
    upper = (tok[:, None] < tok[None, :]).astype(BF16)
    rr = jnp.arange(ROUTER_COLS, dtype=jnp.int32)
    lower = (rr[None, :] < rr[:, None]).astype(BF16)
    return dict(
        w_gate_e=w_gate[i].reshape(N_EXPERTS, D_MODEL, D_EXPERT), w_up_e=w_up[i].reshape(N_EXPERTS, D_MODEL, D_EXPERT),
        w_down_e=w_down[i].reshape(N_EXPERTS, D_EXPERT, D_MODEL),
        wrt_hi=wrt_hi, wrt_lo=wrt_lo, b_rt=b_r.reshape(ROUTER_COLS, 1), upper=upper, lower=lower,
        w_qkvu=w_in[i, :, 0:D_QKVU].astype(BF16), b_qkvu=b_in[i, 0:D_QKVU].reshape(1, -1),
        w_gate=w_in[i, :, D_QKVU:].astype(BF16), b_gate=b_in[i, D_QKVU:].reshape(1, -1),
        wg_bd=wg_bd.astype(BF16), s_pool=row(s_pool),
        w_pool_br=w_pool_br[i].astype(BF16), w_attn_br=w_attn_br[i].astype(BF16), w_o=w_o[i].astype(BF16),
        ln1_g=row(ln1_g), ln1_b=row(ln1_b), ln2_g=row(ln2_g), ln2_b=row(ln2_b),
        w_ple_gate=w_ple_gate[i].astype(BF16), b_ple_gate=row(b_ple_gate), w_ple=w_ple[i].astype(BF16),
        ln3_g=row(ln3_g), ln3_b=row(ln3_b))


def kernel(x_prompt, x_sample, p_prompt, p_sample, cache_k, cache_v, state_pool, page_table, w_in, b_in, lam_q1, lam_k1, lam_q2, lam_k2, g_sub, w_pool_grp, s_pool, w_pool_br, w_attn_br, w_o, ln1_g, ln1_b, w_rg, b_rg, w_re, b_re, w_gate, w_up, w_down, ln2_g, ln2_b, w_ple_gate, b_ple_gate, w_ple, ln3_g, ln3_b):
    assert w_in.shape[0] == DEPTH == 1
    bsz, seq, _ = x_prompt.shape
    dbs, dseq, _ = x_sample.shape
    n_past = page_table.shape[1] * PAGE_SIZE
    wts = _prepare_weights(w_in, b_in, w_pool_grp, s_pool, w_pool_br, w_attn_br, w_o, ln1_g, ln1_b,
                           w_rg, b_rg, w_re, b_re, w_gate, w_up, w_down, ln2_g, ln2_b,
                           w_ple_gate, b_ple_gate, w_ple, ln3_g, ln3_b)
    lam_vecs = [a[0].reshape(1, HEAD_DIM).astype(F32) for a in (lam_q1, lam_k1, lam_q2, lam_k2)]
    g_row = g_sub[0].reshape(1, V_DIM).astype(F32)

    xp = x_prompt.reshape(bsz * seq, D_MODEL)
    qt_p, kf_p, vf_p, kb_p, vt_p, yp_p, tail_p = _proj_prompt(
        xp, wts["w_qkvu"], wts["b_qkvu"], wts["wg_bd"], wts["s_pool"], bsz, seq)

    xs = x_sample.reshape(dbs * dseq, D_MODEL)
    state_pad = jnp.pad(state_pool[0], ((0, 0), (POOL_HIST - POOL_STATE, 0), (0, 0)))
    q_s, kf_s, vf_s, yp_s, tail_s = _proj_sample(
        xs, wts["w_qkvu"], wts["b_qkvu"], wts["wg_bd"], wts["s_pool"], state_pad, dbs, dseq, n_past)

    ya_p, ya_s = _attention(qt_p, kb_p, vt_p, q_s, kf_s, vf_s, cache_k, cache_v, page_table, lam_vecs, g_row,
                            bsz, seq, dseq, tq=MOE_TILE)

    tiles_p = (bsz * seq) // MOE_TILE
    tiles_s = (dbs * dseq) // MOE_TILE
    n_tiles = tiles_p + tiles_s
    x1f, xsort, route, cnt = _merge((xp, yp_p, ya_p), (xs, yp_s, ya_s), wts,
                                    attn_tile=lambda i: _pair_major_tile(i, seq // MOE_TILE))
    cnt_pad = cnt[:, 0].reshape(n_tiles, ROUTER_COLS)[:, N_EXPERT_GROUPS:N_EXPERT_GROUPS + N_EXPERTS]
    n_steps = (n_tiles * REGION_GRANULES + N_EXPERTS * (MOE_SLOTS - 1) + MOE_SLOTS - 1) // MOE_SLOTS
    gin, step_expert, loc = _granule_schedule(cnt_pad.astype(jnp.int32), n_steps)
    ysort = _experts(xsort, gin, step_expert, wts["w_gate_e"], wts["w_up_e"], wts["w_down_e"], n_steps)
    y_p = _final(x1f, route, ysort, loc, p_prompt[0].reshape(bsz * seq, D_PLE), wts, 0)
    y_s = _final(x1f, route, ysort, loc, p_sample[0].reshape(dbs * dseq, D_PLE), wts, tiles_p)

    drop = POOL_HIST - POOL_STATE
    return (y_p.reshape(bsz, seq, D_MODEL),
            y_s.reshape(dbs, dseq, D_MODEL),
            kf_p.reshape(1, bsz, seq, N_HEADS, V_DIM),
            vf_p.reshape(1, bsz, seq, N_HEADS, V_DIM),
            tail_p[None, :, drop:, :],
            kf_s.reshape(1, dbs, dseq, N_HEADS, V_DIM),
            vf_s.reshape(1, dbs, dseq, N_HEADS, V_DIM),
            tail_s[None, :, drop:, :])
```

```python
import functools
import math

import jax
import jax.numpy as jnp
from jax import lax
from jax.experimental import pallas as pl
from jax.experimental.pallas import tpu as pltpu

F32 = jnp.float32
BF16 = jnp.bfloat16

D_MODEL = 1024
N_HEADS = 4
HEAD_DIM = 64
V_DIM = 2 * HEAD_DIM
D_QK = N_HEADS * 2 * HEAD_DIM
D_ATTN = N_HEADS * V_DIM
D_POOL = 512
POOL_WINDOWS = (2, 4, 8, 16)
POOL_GROUP = D_POOL // len(POOL_WINDOWS)
POOL_STATE = max(POOL_WINDOWS) - 1
POOL_HIST = POOL_STATE + 1
N_EXPERT_GROUPS = 4
EXPERTS_PER_GROUP = 8
N_EXPERTS = N_EXPERT_GROUPS * EXPERTS_PER_GROUP
D_EXPERT = 256
D_PLE = 256
PAGE_SIZE = 128
LN_EPS = 1e-5
DEPTH = 1
DEEPNORM_ALPHA = (2 * DEPTH) ** 0.25
LAM_INIT = 0.8 - 0.6 * math.exp(-0.3 * 0)
D_QKVU = D_POOL + 2 * D_QK + D_ATTN
ROUTER_COLS = 128
TOP_K_IN_GROUP = 2
MOE_TILE = 512
MOE_GRANULE = 16
MOE_REGION = -(-(TOP_K_IN_GROUP * MOE_TILE + N_EXPERTS * (MOE_GRANULE - 1)) // 128) * 128
REGION_GRANULES = MOE_REGION // MOE_GRANULE
MOE_SLOTS = 64
LOG2E = 1.4426950408889634
DENOM_ROWS = 16
SAMPLES_PER_STEP = 2
SCORES_AHEAD = 2
COMBINE_PARTS = 2
V7X_VMEM_LIMIT = 56 * 1024 * 1024
NEG_INF = float("-inf")


def _sigmoid(x):
    return 1.0 / (1.0 + jnp.exp(-x))


def _layer_norm(x, g, b):
    mu = jnp.mean(x, axis=-1, keepdims=True)
    xc = x - mu
    var = jnp.mean(xc * xc, axis=-1, keepdims=True)
    return xc * lax.rsqrt(var + LN_EPS) * g + b


def _lam_value(lq1_ref, lk1_ref, lq2_ref, lk2_ref):
    a = jnp.sum(lq1_ref[...] * lk1_ref[...], axis=1, keepdims=True)
    b = jnp.sum(lq2_ref[...] * lk2_ref[...], axis=1, keepdims=True)
    return jnp.exp(a) - jnp.exp(b) + LAM_INIT


def _window_sums(e):
    outs = []
    for gi, w in enumerate(POOL_WINDOWS):
        s = e[:, gi * POOL_GROUP:(gi + 1) * POOL_GROUP]
        step = 1
        while step < w:
            s = s + pltpu.roll(s, step, axis=0)
            step *= 2
        outs.append(s)
    return outs


def _pool_branch(wins, u, inv_cnts, wg_ref, sp_ref):
    ds = []
    for gi in range(len(POOL_WINDOWS)):
        ds.append(wins[gi] * inv_cnts[gi] - u[:, gi * POOL_GROUP:(gi + 1) * POOL_GROUP])
    d = jnp.concatenate(ds, axis=1).astype(BF16)
    return jnp.dot(d, wg_ref[...], preferred_element_type=F32) * sp_ref[...]


def _store_head_major(ref, x, n_rows):
    for hh in range(N_HEADS):
        ref[pl.ds(hh, n_rows, stride=N_HEADS), :] = x[:, hh * V_DIM:(hh + 1) * V_DIM]


def _proj_prompt_kernel(x_ref, w_ref, b_ref, wg_ref, sp_ref,
                        qt_ref, kf_ref, vf_ref, kb_ref, vt_ref, yp_ref, tail_ref,
                        ext_ref, *, tm, tiles_per_seq):
    t_in_seq = pl.program_id(0) % tiles_per_seq
    z = jnp.dot(x_ref[...].astype(BF16), w_ref[...], preferred_element_type=F32) + b_ref[...]
    u = z[:, 0:D_POOL]
    q = z[:, D_POOL:D_POOL + D_QK]
    k = z[:, D_POOL + D_QK:D_POOL + 2 * D_QK]
    v = z[:, D_POOL + 2 * D_QK:D_QKVU]
    qt_ref[...] = (q * (HEAD_DIM ** -0.5 * LOG2E)).T.astype(BF16)
    vt_ref[...] = v.T.astype(BF16)
    kb_ref[...] = k.astype(BF16)
    _store_head_major(kf_ref, k, tm)
    _store_head_major(vf_ref, v, tm)

    @pl.when(t_in_seq == 0)
    def _():
        ext_ref[0:POOL_HIST, :] = jnp.zeros((POOL_HIST, D_POOL), F32)

    ext_ref[POOL_HIST:POOL_HIST + tm, :] = u
    wins = [s[POOL_HIST:, :] for s in _window_sums(ext_ref[...])]
    pos = t_in_seq * tm + lax.broadcasted_iota(jnp.int32, (tm, 1), 0)
    inv_cnts = [1.0 / jnp.minimum(pos + 1, w).astype(F32) for w in POOL_WINDOWS]
    yp_ref[...] = _pool_branch(wins, u, inv_cnts, wg_ref, sp_ref).astype(BF16)
    tail = u[tm - POOL_HIST:, :]
    ext_ref[0:POOL_HIST, :] = tail
    tail_ref[...] = tail


def _proj_sample_kernel(x_ref, w_ref, b_ref, wg_ref, sp_ref, st_ref,
                        q_ref, kf_ref, vf_ref, yp_ref, tail_ref,
                        ext_ref, *, n_seq, t_new, n_past):
    z = jnp.dot(x_ref[...].astype(BF16), w_ref[...], preferred_element_type=F32) + b_ref[...]
    u = z[:, 0:D_POOL]
    q_ref[...] = z[:, D_POOL:D_POOL + D_QK] * (HEAD_DIM ** -0.5)
    _store_head_major(kf_ref, z[:, D_POOL + D_QK:D_POOL + 2 * D_QK], n_seq * t_new)
    _store_head_major(vf_ref, z[:, D_POOL + 2 * D_QK:D_QKVU], n_seq * t_new)
    rows = POOL_HIST + t_new
    ext_ref[:, 0:POOL_HIST, :] = st_ref[...]
    ext_ref[:, POOL_HIST:rows, :] = u.reshape(n_seq, t_new, D_POOL)
    e3 = ext_ref[...]
    wins = [s.reshape(n_seq, rows, POOL_GROUP)[:, POOL_HIST:, :].reshape(n_seq * t_new, POOL_GROUP)
            for s in _window_sums(e3.reshape(n_seq * rows, D_POOL))]
    inv_cnts = [1.0 / float(min(n_past + 1, w)) for w in POOL_WINDOWS]
    yp_ref[...] = _pool_branch(wins, u, inv_cnts, wg_ref, sp_ref).astype(BF16)
    tail_ref[...] = e3[:, rows - POOL_HIST:, :]


def _attn_prompt_pair(slope, lam, qt_refs, k_ref, vt_ref, boff_ref, bdiag_ref, g_ref, o_ref,
                      q2t_ref, m_ref, acc_ref, *, tq, nq):
    pair = pl.program_id(2)
    tiles = (nq - 1 - pair, pair)
    half = tq // 2
    all_lanes = slice(0, 2 * tq)
    late_lanes = (slice(half, tq), slice(tq + half, 2 * tq))
    for slot, qt_ref in enumerate(qt_refs):
        qt = qt_ref[...]
        row = lax.broadcasted_iota(jnp.int32, qt.shape, 0)
        zero = jnp.zeros_like(qt)
        q2t_ref[slot] = jnp.concatenate([jnp.where(row < HEAD_DIM, qt, zero), jnp.where(row >= HEAD_DIM, qt, zero)],
                                        axis=1)

    def both_maps(bias):
        return jnp.concatenate([bias, bias], axis=1)

    def scores(slot, k_start, n_keys, lanes, bias):
        kblk = k_ref[pl.ds(k_start, n_keys), :]
        return jnp.dot(kblk, q2t_ref[slot, :, lanes], preferred_element_type=F32) + bias

    def absorb(t, slot, k_start, n_keys, lanes, shift, first):
        vt_aug = jnp.concatenate([vt_ref[:, pl.ds(k_start, n_keys)], jnp.ones((DENOM_ROWS, n_keys), BF16)], axis=0)
        m_new = jnp.max(t, axis=0, keepdims=True) + shift
        if not first:
            m_prev = m_ref[slot, :, lanes]
            m_new = jnp.maximum(m_prev, m_new)
        pv = jnp.dot(vt_aug, jnp.exp2(t - (m_new - shift)).astype(BF16), preferred_element_type=F32)
        acc_ref[slot, :, lanes] = pv if first else jnp.exp2(m_prev - m_new) * acc_ref[slot, :, lanes] + pv
        m_ref[slot, :, lanes] = m_new

    zero_shift = jnp.float32(0.0)
    tasks = []
    for slot in range(2):
        k_diag = pl.multiple_of(tiles[slot] * tq, tq)
        k_diag2 = pl.multiple_of(k_diag + half, half)
        tasks.append((slot, k_diag, half, all_lanes, lambda: both_maps(bdiag_ref[0:half, :]), zero_shift, True))
        for lanes in late_lanes:
            tasks.append((slot, k_diag2, half, lanes, lambda: bdiag_ref[half:tq, half:tq], zero_shift, False))
    for j in range(nq - 1):
        in_second = j >= tiles[0]
        kb = jnp.where(in_second, j - tiles[0], j)
        q_tile = jnp.where(in_second, tiles[1], tiles[0])
        shift = -(slope * LOG2E) * ((q_tile - kb) * tq).astype(F32)
        tasks.append((in_second.astype(jnp.int32), pl.multiple_of(kb * tq, tq), tq, all_lanes,
                      lambda: both_maps(boff_ref[...]), shift, False))
    queue = []
    for slot, k_start, n_keys, lanes, bias_fn, shift, first in tasks:
        queue.append((scores(slot, k_start, n_keys, lanes, bias_fn()), slot, k_start, n_keys, lanes, shift, first))
        if len(queue) > SCORES_AHEAD:
            absorb(*queue.pop(0))
    while queue:
        absorb(*queue.pop(0))

    for slot in range(2):
        acc = acc_ref[slot]
        on = acc[0:V_DIM, :] / acc[V_DIM:V_DIM + 1, :]
        o = (on[:, 0:tq] - lam * on[:, tq:]).T
        o = o * lax.rsqrt(jnp.mean(o * o, axis=1, keepdims=True) + LN_EPS) * g_ref[...] * (1.0 - LAM_INIT)
        o_ref[slot * tq:(slot + 1) * tq, :] = o.astype(BF16)


def _attn_sample_body(lam, sj, q_ref, kn_ref, vn_ref, bias_ref, g_ref, k_pages, v_pages, *, t_new):
    n_pages = len(k_pages)
    n_past = n_pages * PAGE_SIZE
    rows = slice(sj * t_new, (sj + 1) * t_new)
    q = q_ref[rows, :]
    lane = lax.broadcasted_iota(jnp.int32, (t_new, V_DIM), 1)
    pad = jnp.zeros((PAGE_SIZE - t_new, V_DIM), BF16)
    nt_dims = (((1,), (1,)), ((), ()))

    def head_rows(pages, new_ref, hh):
        past = jnp.concatenate([pg[pl.ds(hh, PAGE_SIZE, stride=N_HEADS), :] for pg in pages], axis=0)
        new = new_ref[pl.ds(sj * t_new * N_HEADS + hh, t_new, stride=N_HEADS), :]
        new = jnp.concatenate([new.astype(BF16), pad], axis=0)
        return past.astype(BF16), new

    s_rows = []
    for hh in range(N_HEADS):
        qh = q[:, hh * V_DIM:(hh + 1) * V_DIM]
        qh2 = jnp.concatenate([jnp.where(lane < HEAD_DIM, qh, 0.0), jnp.where(lane >= HEAD_DIM, qh, 0.0)],
                              axis=0).astype(BF16)
        k_past, k_new = head_rows(k_pages, kn_ref, hh)
        s_rows.append(jnp.concatenate(
            [lax.dot_general(qh2, k_past, nt_dims, preferred_element_type=F32),
             lax.dot_general(qh2, k_new, nt_dims, preferred_element_type=F32)], axis=1))
    s = jnp.concatenate(s_rows, axis=0) + bias_ref[...]
    m = jnp.max(s, axis=1, keepdims=True)
    p = jnp.exp(s - m)
    p = p / jnp.sum(p, axis=1, keepdims=True)
    outs = []
    for h0 in range(0, N_HEADS, 2):
        a2 = jnp.concatenate(
            [p[hh * 2 * t_new:hh * 2 * t_new + t_new, :] - lam * p[hh * 2 * t_new + t_new:(hh + 1) * 2 * t_new, :]
             for hh in (h0, h0 + 1)], axis=0).astype(BF16)
        (vp0, vn0), (vp1, vn1) = head_rows(v_pages, vn_ref, h0), head_rows(v_pages, vn_ref, h0 + 1)
        o2 = (jnp.dot(a2[:, 0:n_past], jnp.concatenate([vp0, vp1], axis=1), preferred_element_type=F32)
              + jnp.dot(a2[:, n_past:], jnp.concatenate([vn0, vn1], axis=1), preferred_element_type=F32))
        for k in range(2):
            oh = o2[k * t_new:(k + 1) * t_new, k * V_DIM:(k + 1) * V_DIM]
            oh = oh * lax.rsqrt(jnp.mean(oh * oh, axis=1, keepdims=True) + LN_EPS) * g_ref[...] * (1.0 - LAM_INIT)
            outs.append(oh)
    return jnp.concatenate(outs, axis=1)


def _attn_kernel(pt_ref, slopes_ref, lq1_ref, lk1_ref, lq2_ref, lk2_ref,
                 qta_ref, qtb_ref, k_ref, vt_ref, boff_ref, bdiag_ref, g_ref, bias_s_ref, *rest,
                 n_pages, t_new, tq, nq):
    qs_ref, kn_ref, vn_ref = rest[:3]
    pages = rest[3:3 + SAMPLES_PER_STEP * 2 * n_pages]
    op_ref, os_ref, q2t_ref, m_ref, acc_ref = rest[3 + SAMPLES_PER_STEP * 2 * n_pages:]
    lam = _lam_value(lq1_ref, lk1_ref, lq2_ref, lk2_ref)
    def sample(sj):
        pg = pages[sj * 2 * n_pages:(sj + 1) * 2 * n_pages]
        return _attn_sample_body(lam, sj, qs_ref, kn_ref, vn_ref, bias_s_ref, g_ref, pg[:n_pages], pg[n_pages:],
                                 t_new=t_new)

    outs = [sample(sj) for sj in range(SAMPLES_PER_STEP // 2)]
    _attn_prompt_pair(slopes_ref[pl.program_id(1)], lam, (qta_ref, qtb_ref), k_ref, vt_ref, boff_ref, bdiag_ref,
                      g_ref, op_ref, q2t_ref, m_ref, acc_ref, tq=tq, nq=nq)
    outs += [sample(sj) for sj in range(SAMPLES_PER_STEP // 2, SAMPLES_PER_STEP)]
    os_ref[...] = jnp.concatenate(outs, axis=0).astype(BF16)


def _merge_kernel(xp_ref, ypp_ref, yap_ref, xs_in_ref, yps_ref, yas_ref,
                  wgate_ref, bgate_ref, wpb_ref, wab_ref, wo_ref,
                  g1_ref, b1_ref, wrt_hi_ref, wrt_lo_ref, brt_ref, upper_ref, lower_ref,
                  x1f_ref, xs_ref, route_ref, cnt_ref, *, tm, prompt_tiles):
    is_sample = pl.program_id(0) >= prompt_tiles
    x = jnp.where(is_sample, xs_in_ref[...], xp_ref[...])
    y_pool = jnp.where(is_sample, yps_ref[...], ypp_ref[...])
    y_attn = jnp.where(is_sample, yas_ref[...], yap_ref[...])
    zg = jnp.dot(x.astype(BF16), wgate_ref[...], preferred_element_type=F32) + bgate_ref[...]
    gp = _sigmoid(zg[:, 0:D_MODEL])
    ga = _sigmoid(zg[:, D_MODEL:])
    merged = (gp * jnp.dot(y_pool, wpb_ref[...], preferred_element_type=F32)
              + ga * jnp.dot(y_attn, wab_ref[...], preferred_element_type=F32))
    x1 = _layer_norm(DEEPNORM_ALPHA * x + jnp.dot(merged.astype(BF16), wo_ref[...], preferred_element_type=F32),
                     g1_ref[...], b1_ref[...])
    x1f_ref[...] = x1
    x1_hi = x1.astype(BF16)
    x1_lo = (x1 - x1_hi.astype(F32)).astype(BF16)
    nt_dims = (((1,), (1,)), ((), ()))
    lg = (lax.dot_general(wrt_hi_ref[...], x1_hi, nt_dims, preferred_element_type=F32)
          + lax.dot_general(wrt_hi_ref[...], x1_lo, nt_dims, preferred_element_type=F32)
          + lax.dot_general(wrt_lo_ref[...], x1_hi, nt_dims, preferred_element_type=F32)) + brt_ref[...]
    row = lax.broadcasted_iota(jnp.int32, lg.shape, 0)
    big = jnp.int32(4 * ROUTER_COLS)
    lgg = jnp.where(row < N_EXPERT_GROUPS, lg, NEG_INF)
    gmax = jnp.max(lgg, axis=0, keepdims=True)
    g_w = 1.0 / jnp.sum(jnp.exp(lgg - gmax), axis=0, keepdims=True)
    g_idx = jnp.min(jnp.where(lgg == gmax, row, big), axis=0, keepdims=True)
    lo_row = N_EXPERT_GROUPS + g_idx * EXPERTS_PER_GROUP
    le = jnp.where((row >= lo_row) & (row < lo_row + EXPERTS_PER_GROUP), lg, NEG_INF)
    v1 = jnp.max(le, axis=0, keepdims=True)
    i1 = jnp.min(jnp.where(le == v1, row, big), axis=0, keepdims=True)
    le2 = jnp.where(row == i1, NEG_INF, le)
    v2 = jnp.max(le2, axis=0, keepdims=True)
    i2 = jnp.min(jnp.where(le2 == v2, row, big), axis=0, keepdims=True)
    e21 = jnp.exp(v2 - v1)
    c1 = g_w / (1.0 + e21)
    c2 = g_w * e21 / (1.0 + e21)
    sel1 = row == i1
    sel2 = row == i2
    member = jnp.where(sel1 | sel2, 1.0, 0.0)
    before = jnp.dot(member.astype(BF16), upper_ref[...], preferred_element_type=F32)
    cnt = jnp.sum(member, axis=1, keepdims=True)
    cnt_pad = jnp.ceil(cnt * (1.0 / MOE_GRANULE)) * MOE_GRANULE
    cnt_b = jnp.broadcast_to(cnt_pad, (ROUTER_COLS, ROUTER_COLS))
    seg_start = jnp.dot(lower_ref[...], cnt_b.astype(BF16), preferred_element_type=F32)
    cnt_ref[...] = cnt_b
    slot = seg_start[:, 0:1] + before
    pos1 = jnp.sum(jnp.where(sel1, slot, 0.0), axis=0, keepdims=True)
    pos2 = jnp.sum(jnp.where(sel2, slot, 0.0), axis=0, keepdims=True)
    srow = lax.broadcasted_iota(jnp.int32, (xs_ref.shape[0], tm), 0).astype(F32)
    perm = jnp.where((srow == pos1) | (srow == pos2), 1.0, 0.0).astype(BF16)
    xs_ref[...] = jnp.dot(perm, x1_hi, preferred_element_type=F32).astype(BF16)
    info = (jnp.where(row == 0, pos1, 0.0) + jnp.where(row == 1, pos2, 0.0)
            + jnp.where(row == 2, c1, 0.0) + jnp.where(row == 3, c2, 0.0))
    route_ref[...] = info.T


def _experts_kernel(gin_ref, sexp_ref, xs_hbm, wg_ref, wu_ref, wd_ref, y_ref,
                    xbuf, sems, wg_b, wu_b, wd_b, *, n_slots, n_steps):
    t = pl.program_id(0)
    expert = sexp_ref[t]
    is_expert = expert < N_EXPERTS
    cur = t % 2

    def gather(step, slot):
        for k in range(n_slots):
            src = pl.multiple_of(gin_ref[step * n_slots + k] * MOE_GRANULE, MOE_GRANULE)
            pltpu.make_async_copy(xs_hbm.at[pl.ds(src, MOE_GRANULE), :],
                                  xbuf.at[slot, pl.ds(k * MOE_GRANULE, MOE_GRANULE), :], sems.at[slot]).start()

    @pl.when((t == 0) & is_expert)
    def _():
        gather(0, 0)

    nxt = jnp.minimum(t + 1, n_steps - 1)

    @pl.when((t + 1 < n_steps) & (sexp_ref[nxt] < N_EXPERTS))
    def _():
        gather(nxt, 1 - cur)

    @pl.when(is_expert & ((t == 0) | (expert != sexp_ref[jnp.maximum(t - 1, 0)])))
    def _():
        wg_b[...] = wg_ref[...].astype(BF16)
        wu_b[...] = wu_ref[...].astype(BF16)
        wd_b[...] = wd_ref[...].astype(BF16)

    @pl.when(is_expert)
    def _():
        pltpu.make_async_copy(xbuf.at[cur], xbuf.at[cur], sems.at[cur]).wait()
        rows = (n_slots // 2) * MOE_GRANULE

        def gate_up(row0):
            xg = xbuf[cur, row0:row0 + rows, :]
            return (jnp.dot(xg, wg_b[...], preferred_element_type=F32),
                    jnp.dot(xg, wu_b[...], preferred_element_type=F32))

        def act_down(gate, up):
            hact = (gate * _sigmoid(gate) * up).astype(BF16)
            return jnp.dot(hact, wd_b[...], preferred_element_type=F32).astype(BF16)

        gate_a, up_a = gate_up(0)
        gate_b, up_b = gate_up(rows)
        y_ref[0:rows, :] = act_down(gate_a, up_a)
        y_ref[rows:, :] = act_down(gate_b, up_b)

    @pl.when(jnp.logical_not(is_expert))
    def _():
        y_ref[...] = jnp.zeros(y_ref.shape, BF16)


def _final_kernel(loc_ref, x1f_ref, route_ref, *rest, n_granules):
    g_refs = rest[:n_granules]
    p_ref, g2_ref, b2_ref, wpg_ref, bpg_ref, wple_ref, g3_ref, b3_ref, y_ref = rest[n_granules:]
    ys = jnp.concatenate([r[...] for r in g_refs], axis=0)
    tm = route_ref.shape[0]
    part = tm // COMBINE_PARTS
    parts = [slice(k * part, (k + 1) * part) for k in range(COMBINE_PARTS)]
    scol = lax.broadcasted_iota(jnp.int32, (part, ys.shape[0]), 1).astype(F32)

    def combine(rows):
        route = route_ref[rows, :]
        comb = (jnp.where(scol == route[:, 0:1], route[:, 2:3], 0.0)
                + jnp.where(scol == route[:, 1:2], route[:, 3:4], 0.0)).astype(BF16)
        return jnp.dot(comb, ys, preferred_element_type=F32)

    def ple(rows, moe):
        x2 = _layer_norm(DEEPNORM_ALPHA * x1f_ref[rows, :] + moe, g2_ref[...], b2_ref[...])
        z = jnp.dot(x2.astype(BF16), wpg_ref[...], preferred_element_type=F32) + bpg_ref[...]
        pe = jnp.dot(p_ref[rows, :].astype(BF16), wple_ref[...], preferred_element_type=F32)
        return x2, z, pe

    def finish(rows, x2, z, pe):
        y_ref[rows, :] = _layer_norm(DEEPNORM_ALPHA * x2 + _sigmoid(z) * pe, g3_ref[...], b3_ref[...])

    moe, mid = {}, {}
    for k in range(COMBINE_PARTS + 2):
        if k < COMBINE_PARTS:
            moe[k] = combine(parts[k])
        if 0 <= k - 1 < COMBINE_PARTS:
            mid[k - 1] = ple(parts[k - 1], moe.pop(k - 1))
        if 0 <= k - 2 < COMBINE_PARTS:
            finish(parts[k - 2], *mid.pop(k - 2))


def _full(shape):
    nd = len(shape)
    return pl.BlockSpec(shape, lambda *_: (0,) * nd)


def _cparams(sem):
    return pltpu.CompilerParams(dimension_semantics=sem, vmem_limit_bytes=V7X_VMEM_LIMIT)


def _proj_prompt(x2d, w_qkvu, b_qkvu, wg_bd, s_pool, n_seq, seq_len, tm=512):
    n = x2d.shape[0]
    tps = seq_len // tm
    row = lambda i: (i, 0)
    col = lambda i: (0, i)
    outs = pl.pallas_call(
        functools.partial(_proj_prompt_kernel, tm=tm, tiles_per_seq=tps),
        grid=(n // tm,),
        in_specs=[pl.BlockSpec((tm, D_MODEL), row), _full(w_qkvu.shape), _full(b_qkvu.shape),
                  _full(wg_bd.shape), _full(s_pool.shape)],
        out_specs=[pl.BlockSpec((D_QK, tm), col), pl.BlockSpec((tm * N_HEADS, V_DIM), row),
                   pl.BlockSpec((tm * N_HEADS, V_DIM), row), pl.BlockSpec((tm, D_QK), row),
                   pl.BlockSpec((D_ATTN, tm), col), pl.BlockSpec((tm, D_POOL), row),
                   pl.BlockSpec((None, POOL_HIST, D_POOL), lambda i: (i // tps, 0, 0))],
        out_shape=[jax.ShapeDtypeStruct((D_QK, n), BF16), jax.ShapeDtypeStruct((n * N_HEADS, V_DIM), F32),
                   jax.ShapeDtypeStruct((n * N_HEADS, V_DIM), F32), jax.ShapeDtypeStruct((n, D_QK), BF16),
                   jax.ShapeDtypeStruct((D_ATTN, n), BF16), jax.ShapeDtypeStruct((n, D_POOL), BF16),
                   jax.ShapeDtypeStruct((n_seq, POOL_HIST, D_POOL), F32)],
        scratch_shapes=[pltpu.VMEM((POOL_HIST + tm, D_POOL), F32)],
        compiler_params=_cparams(("arbitrary",)),
        name="proj_pool_prompt",
    )(x2d, w_qkvu, b_qkvu, wg_bd, s_pool)
    return outs


def _proj_sample(x2d, w_qkvu, b_qkvu, wg_bd, s_pool, state_pad, n_seq, t_new, n_past):
    n = x2d.shape[0]
    rows = POOL_HIST + t_new
    return pl.pallas_call(
        functools.partial(_proj_sample_kernel, n_seq=n_seq, t_new=t_new, n_past=n_past),
        grid=(1,),
        in_specs=[_full(x2d.shape), _full(w_qkvu.shape), _full(b_qkvu.shape), _full(wg_bd.shape),
                  _full(s_pool.shape), _full(state_pad.shape)],
        out_specs=[_full((n, D_QK)), _full((n * N_HEADS, V_DIM)), _full((n * N_HEADS, V_DIM)),
                   _full((n, D_POOL)), _full((n_seq, POOL_HIST, D_POOL))],
        out_shape=[jax.ShapeDtypeStruct((n, D_QK), F32), jax.ShapeDtypeStruct((n * N_HEADS, V_DIM), F32),
                   jax.ShapeDtypeStruct((n * N_HEADS, V_DIM), F32), jax.ShapeDtypeStruct((n, D_POOL), BF16),
                   jax.ShapeDtypeStruct((n_seq, POOL_HIST, D_POOL), F32)],
        scratch_shapes=[pltpu.VMEM((n_seq, rows, D_POOL), F32)],
        compiler_params=_cparams(("arbitrary",)),
        name="proj_pool_sample",
    )(x2d, w_qkvu, b_qkvu, wg_bd, s_pool, state_pad)


def _alibi_slopes():
    return 2.0 ** (-8.0 * jnp.arange(1, N_HEADS + 1, dtype=F32) / N_HEADS)


def _attention(qt, kb, vt, q_s, k_new, v_new, cache_k, cache_v, page_table, lam_vecs, g_sub,
               n_seq, seq_len, t_new, tq=512):
    n = kb.shape[0]
    nq = seq_len // tq
    n_pairs = nq // 2
    n_smp, n_pages = page_table.shape
    n_past = n_pages * PAGE_SIZE
    assert nq % 2 == 0 and n_smp == SAMPLES_PER_STEP * n_seq * N_HEADS * n_pairs, (n_smp, n_seq, nq)
    slopes = _alibi_slopes()
    rel = (jnp.arange(tq, dtype=jnp.int32)[None, :] - jnp.arange(tq, dtype=jnp.int32)[:, None])
    b_off = -(slopes * LOG2E)[:, None, None] * rel.astype(F32)[None]
    b_diag = jnp.where(rel[None] >= 0, b_off, NEG_INF)
    r = jnp.arange(2 * N_HEADS * t_new, dtype=jnp.int32)
    r_head, r_q = r // (2 * t_new), r % t_new
    row_slope = slopes[r_head][:, None]
    kpos = jnp.arange(n_past, dtype=jnp.int32)[None, :]
    b_past = -row_slope * (n_past + r_q[:, None] - kpos).astype(F32)
    j = jnp.arange(PAGE_SIZE, dtype=jnp.int32)[None, :]
    dist_new = r_q[:, None] - j
    b_new = jnp.where((dist_new >= 0) & (j < t_new), -row_slope * dist_new.astype(F32), NEG_INF)
    bias_s = jnp.concatenate([b_past, b_new], axis=1)
    page_rows = PAGE_SIZE * N_HEADS
    ck = cache_k.reshape(-1, V_DIM)
    cv = cache_v.reshape(-1, V_DIM)
    pt = page_table.reshape(-1).astype(jnp.int32)
    step = lambda b, h, i: (b * N_HEADS + h) * n_pairs + i
    const2 = lambda b, h, i, pt_ref: (0, 0)
    vec = pl.BlockSpec((1, HEAD_DIM), const2)
    smp_rows = SAMPLES_PER_STEP * t_new
    tok = pl.BlockSpec((smp_rows, D_QK), lambda b, h, i, pt_ref: (step(b, h, i), 0))
    tok_hm = pl.BlockSpec((smp_rows * N_HEADS, V_DIM), lambda b, h, i, pt_ref: (step(b, h, i), 0))
    table = pl.BlockSpec((None, tq, tq), lambda b, h, i, pt_ref: (h, 0, 0))

    def page_spec(sj, pi):
        return pl.BlockSpec((page_rows, V_DIM), lambda b, h, i, pt_ref: (
            pt_ref[(step(b, h, i) * SAMPLES_PER_STEP + sj) * n_pages + pi], 0))

    pages = []
    for sj in range(SAMPLES_PER_STEP):
        pages += [(page_spec(sj, pi), ck) for pi in range(n_pages)] + [(page_spec(sj, pi), cv) for pi in range(n_pages)]
    grid_spec = pltpu.PrefetchScalarGridSpec(
        num_scalar_prefetch=1,
        grid=(n_seq, N_HEADS, n_pairs),
        in_specs=[pl.BlockSpec(memory_space=pltpu.SMEM), vec, vec, vec, vec,
                  pl.BlockSpec((V_DIM, tq), lambda b, h, i, pt_ref: (h, b * nq + nq - 1 - i)),
                  pl.BlockSpec((V_DIM, tq), lambda b, h, i, pt_ref: (h, b * nq + i)),
                  pl.BlockSpec((seq_len, V_DIM), lambda b, h, i, pt_ref: (b, h)),
                  pl.BlockSpec((V_DIM, seq_len), lambda b, h, i, pt_ref: (h, b)),
                  table, table, pl.BlockSpec((1, V_DIM), const2), pl.BlockSpec(bias_s.shape, const2),
                  tok, tok_hm, tok_hm] + [spec for spec, _ in pages],
        out_specs=[pl.BlockSpec((2 * tq, V_DIM), lambda b, h, i, pt_ref: (b * n_pairs + i, h)), tok],
        scratch_shapes=[pltpu.VMEM((2, V_DIM, 2 * tq), BF16), pltpu.VMEM((2, 1, 2 * tq), F32),
                        pltpu.VMEM((2, V_DIM + DENOM_ROWS, 2 * tq), F32)],
    )
    return pl.pallas_call(
        functools.partial(_attn_kernel, n_pages=n_pages, t_new=t_new, tq=tq, nq=nq),
        grid_spec=grid_spec,
        out_shape=[jax.ShapeDtypeStruct((n, D_ATTN), BF16), jax.ShapeDtypeStruct((n_smp * t_new, D_ATTN), BF16)],
        compiler_params=_cparams(("arbitrary", "arbitrary", "arbitrary")),
        name="attention",
    )(pt, slopes, *lam_vecs, qt, qt, kb, vt, b_off, b_diag, g_sub, bias_s, q_s, k_new, v_new,
      *[arr for _, arr in pages])


def _pair_major_tile(i, nq):
    b, qi = i // nq, i % nq
    return b * nq + jnp.where(qi >= nq // 2, 2 * (nq - 1 - qi), 2 * qi + 1)


def _merge(prompt, sample, wts, attn_tile):
    tm = MOE_TILE
    tiles_p = prompt[0].shape[0] // tm
    n_tiles = tiles_p + sample[0].shape[0] // tm
    p_row = lambda i: (jnp.minimum(i, tiles_p - 1), 0)
    s_row = lambda i: (jnp.maximum(i - tiles_p, 0), 0)
    row = lambda i: (i, 0)
    consts = [wts[k] for k in ("w_gate", "b_gate", "w_pool_br", "w_attn_br", "w_o", "ln1_g", "ln1_b",
                               "wrt_hi", "wrt_lo", "b_rt", "upper", "lower")]
    return pl.pallas_call(
        functools.partial(_merge_kernel, tm=tm, prompt_tiles=tiles_p),
        grid=(n_tiles,),
        in_specs=[pl.BlockSpec((tm, D_MODEL), p_row), pl.BlockSpec((tm, D_POOL), p_row),
                  pl.BlockSpec((tm, D_ATTN), lambda i: (attn_tile(jnp.minimum(i, tiles_p - 1)), 0)),
                  pl.BlockSpec((tm, D_MODEL), s_row), pl.BlockSpec((tm, D_POOL), s_row),
                  pl.BlockSpec((tm, D_ATTN), s_row)] + [_full(c.shape) for c in consts],
        out_specs=[pl.BlockSpec((tm, D_MODEL), row), pl.BlockSpec((MOE_REGION, D_MODEL), row),
                   pl.BlockSpec((tm, ROUTER_COLS), row), pl.BlockSpec((ROUTER_COLS, ROUTER_COLS), row)],
        out_shape=[jax.ShapeDtypeStruct((n_tiles * tm, D_MODEL), F32),
                   jax.ShapeDtypeStruct((n_tiles * MOE_REGION, D_MODEL), BF16),
                   jax.ShapeDtypeStruct((n_tiles * tm, ROUTER_COLS), F32),
                   jax.ShapeDtypeStruct((n_tiles * ROUTER_COLS, ROUTER_COLS), F32)],
        compiler_params=_cparams(("arbitrary",)),
        name="merge_ln1_route_dispatch",
    )(*prompt, *sample, *consts)


def _granule_schedule(cnt_pad, n_steps):
    n_tiles = cnt_pad.shape[0]
    gc = cnt_pad // MOE_GRANULE
    seg = jnp.cumsum(gc, axis=1) - gc
    tot = jnp.sum(gc, axis=0)
    ahead = jnp.cumsum(gc, axis=0) - gc
    slots_e = ((tot + MOE_SLOTS - 1) // MOE_SLOTS) * MOE_SLOTS
    first_slot = jnp.cumsum(slots_e) - slots_e
    p0 = (jnp.arange(n_steps, dtype=jnp.int32) * MOE_SLOTS)[None, :]
    in_e = ((p0 >= first_slot[:, None]) & (p0 < (first_slot + slots_e)[:, None])).astype(jnp.int32)
    q0 = p0 - jnp.sum(in_e * first_slot[:, None], axis=0, keepdims=True)
    tot_s = jnp.sum(in_e * tot[:, None], axis=0, keepdims=True)
    e_id = jnp.sum(in_e * jnp.arange(N_EXPERTS, dtype=jnp.int32)[:, None], axis=0)
    step_expert = jnp.where(q0[0] < tot_s[0], e_id, N_EXPERTS).astype(jnp.int32)
    pick = lambda tab: jnp.sum(tab[:, :, None] * in_e[None, :, :], axis=1)[:, :, None]
    ahead_s, gc_s, seg_s = pick(ahead), pick(gc), pick(seg)
    q = q0[0][None, :, None] + jnp.arange(MOE_SLOTS, dtype=jnp.int32)[None, None, :]
    in_t = ((q >= ahead_s) & (q < ahead_s + gc_s)).astype(jnp.int32)
    tile_base = (jnp.arange(n_tiles, dtype=jnp.int32) * REGION_GRANULES)[:, None, None]
    gid = jnp.sum(in_t * (tile_base + seg_s + q - ahead_s), axis=0)
    gin = jnp.where(q[0] < tot_s[0][:, None], gid, 0).astype(jnp.int32).reshape(-1)
    s = jnp.arange(REGION_GRANULES, dtype=jnp.int32)[None, None, :]
    in_g = ((s >= seg[:, :, None]) & (s < (seg + gc)[:, :, None])).astype(jnp.int32)
    loc = jnp.sum(in_g * ((first_slot[None, :] + ahead - seg)[:, :, None] + s), axis=1)
    used = jnp.sum(in_g, axis=1) > 0
    loc = jnp.where(used, loc, loc[:, 0:1]).astype(jnp.int32)
    return gin, step_expert, loc.reshape(-1)


def _experts(xs, gin, step_expert, w_gate, w_up, w_down, n_steps):
    rows = MOE_SLOTS * MOE_GRANULE
    w_sel = lambda t, gin_ref, se_ref: (jnp.minimum(se_ref[t], N_EXPERTS - 1), 0, 0)
    grid_spec = pltpu.PrefetchScalarGridSpec(
        num_scalar_prefetch=2,
        grid=(n_steps,),
        in_specs=[pl.BlockSpec(memory_space=pl.ANY),
                  pl.BlockSpec((None, D_MODEL, D_EXPERT), w_sel), pl.BlockSpec((None, D_MODEL, D_EXPERT), w_sel),
                  pl.BlockSpec((None, D_EXPERT, D_MODEL), w_sel)],
        out_specs=pl.BlockSpec((rows, D_MODEL), lambda t, gin_ref, se_ref: (t, 0)),
        scratch_shapes=[pltpu.VMEM((2, rows, D_MODEL), BF16), pltpu.SemaphoreType.DMA((2,)),
                        pltpu.VMEM((D_MODEL, D_EXPERT), BF16), pltpu.VMEM((D_MODEL, D_EXPERT), BF16),
                        pltpu.VMEM((D_EXPERT, D_MODEL), BF16)],
    )
    return pl.pallas_call(
        functools.partial(_experts_kernel, n_slots=MOE_SLOTS, n_steps=n_steps),
        grid_spec=grid_spec,
        out_shape=jax.ShapeDtypeStruct((n_steps * rows, D_MODEL), BF16),
        compiler_params=_cparams(("arbitrary",)),
        name="expert_mlps",
    )(gin, step_expert, xs, w_gate, w_up, w_down)


def _final(x1f, route, ys, loc, p2d, wts, tile0):
    tm = MOE_TILE
    n = p2d.shape[0]
    consts = [wts[k] for k in ("ln2_g", "ln2_b", "w_ple_gate", "b_ple_gate", "w_ple", "ln3_g", "ln3_b")]

    def granule_spec(s):
        return pl.BlockSpec((MOE_GRANULE, D_MODEL),
                            lambda i, loc_ref: (loc_ref[(i + tile0) * REGION_GRANULES + s], 0))

    grid_spec = pltpu.PrefetchScalarGridSpec(
        num_scalar_prefetch=1,
        grid=(n // tm,),
        in_specs=[pl.BlockSpec((tm, D_MODEL), lambda i, loc_ref: (i + tile0, 0)),
                  pl.BlockSpec((tm, ROUTER_COLS), lambda i, loc_ref: (i + tile0, 0))]
                 + [granule_spec(s) for s in range(REGION_GRANULES)]
                 + [pl.BlockSpec((tm, D_PLE), lambda i, loc_ref: (i, 0))]
                 + [pl.BlockSpec(c.shape, lambda i, loc_ref: (0, 0)) for c in consts],
        out_specs=pl.BlockSpec((tm, D_MODEL), lambda i, loc_ref: (i, 0)),
    )
    return pl.pallas_call(
        functools.partial(_final_kernel, n_granules=REGION_GRANULES),
        grid_spec=grid_spec,
        out_shape=jax.ShapeDtypeStruct((n, D_MODEL), F32),
        compiler_params=_cparams(("arbitrary",)),
        name="combine_ln2_ple_ln3",
    )(loc, x1f, route, *([ys] * REGION_GRANULES), p2d, *consts)


def _prepare_weights(w_in, b_in, w_pool_grp, s_pool, w_pool_br, w_attn_br, w_o, ln1_g, ln1_b,
                     w_rg, b_rg, w_re, b_re, w_gate, w_up, w_down, ln2_g, ln2_b,
                     w_ple_gate, b_ple_gate, w_ple, ln3_g, ln3_b):
    i = 0
    row = lambda a: a[i].reshape(1, -1).astype(F32)
    wg_bd = jnp.zeros((D_POOL, D_POOL), F32)
    for gi in range(len(POOL_WINDOWS)):
        sl = slice(gi * POOL_GROUP, (gi + 1) * POOL_GROUP)
        wg_bd = wg_bd.at[sl, sl].set(w_pool_grp[i, gi])
    w_r = jnp.zeros((D_MODEL, ROUTER_COLS), F32)
    w_r = w_r.at[:, 0:N_EXPERT_GROUPS].set(w_rg[i])
    w_r = w_r.at[:, N_EXPERT_GROUPS:N_EXPERT_GROUPS + N_EXPERTS].set(w_re[i].reshape(D_MODEL, N_EXPERTS))
    b_r = jnp.zeros((1, ROUTER_COLS), F32)
    b_r = b_r.at[0, 0:N_EXPERT_GROUPS].set(b_rg[i])
    b_r = b_r.at[0, N_EXPERT_GROUPS:N_EXPERT_GROUPS + N_EXPERTS].set(b_re[i].reshape(-1))
    wrt_hi = w_r.T.astype(BF16)
    wrt_lo = (w_r.T - wrt_hi.astype(F32)).astype(BF16)
    tok = jnp.arange(MOE_TILE, dtype=jnp.int32)
    upper = (tok[:, None] < tok[None, :]).astype(BF16)
    rr = jnp.arange(ROUTER_COLS, dtype=jnp.int32)
    lower = (rr[None, :] < rr[:, None]).astype(BF16)
    return dict(
        w_gate_e=w_gate[i].reshape(N_EXPERTS, D_MODEL, D_EXPERT), w_up_e=w_up[i].reshape(N_EXPERTS, D_MODEL, D_EXPERT),
        w_down_e=w_down[i].reshape(N_EXPERTS, D_EXPERT, D_MODEL),
        wrt_hi=wrt_hi, wrt_lo=wrt_lo, b_rt=b_r.reshape(ROUTER_COLS, 1), upper=upper, lower=lower,
        w_qkvu=w_in[i, :, 0:D_QKVU].astype(BF16), b_qkvu=b_in[i, 0:D_QKVU].reshape(1, -1),
        w_gate=w_in[i, :, D_QKVU:].astype(BF16), b_gate=b_in[i, D_QKVU:].reshape(1, -1),
        wg_bd=wg_bd.astype(BF16), s_pool=row(s_pool),
        w_pool_br=w_pool_br[i].astype(BF16), w_attn_br=w_attn_br[i].astype(BF16), w_o=w_o[i].astype(BF16),
        ln1_g=row(ln1_g), ln1_b=row(ln1_b), ln2_g=row(ln2_g), ln2_b=row(ln2_b),
        w_ple_gate=w_ple_gate[i].astype(BF16), b_ple_gate=row(b_ple_gate), w_ple=w_ple[i].astype(BF16),
        ln3_g=row(ln3_g), ln3_b=row(ln3_b))


def kernel(x_prompt, x_sample, p_prompt, p_sample, cache_k, cache_v, state_pool, page_table, w_in, b_in, lam_q1, lam_k1, lam_q2, lam_k2, g_sub, w_pool_grp, s_pool, w_pool_br, w_attn_br, w_o, ln1_g, ln1_b, w_rg, b_rg, w_re, b_re, w_gate, w_up, w_down, ln2_g, ln2_b, w_ple_gate, b_ple_gate, w_ple, ln3_g, ln3_b):
    assert w_in.shape[0] == DEPTH == 1
    bsz, seq, _ = x_prompt.shape
    dbs, dseq, _ = x_sample.shape
    n_past = page_table.shape[1] * PAGE_SIZE
    wts = _prepare_weights(w_in, b_in, w_pool_grp, s_pool, w_pool_br, w_attn_br, w_o, ln1_g, ln1_b,
                           w_rg, b_rg, w_re, b_re, w_gate, w_up, w_down, ln2_g, ln2_b,
                           w_ple_gate, b_ple_gate, w_ple, ln3_g, ln3_b)
    lam_vecs = [a[0].reshape(1, HEAD_DIM).astype(F32) for a in (lam_q1, lam_k1, lam_q2, lam_k2)]
    g_row = g_sub[0].reshape(1, V_DIM).astype(F32)

    xp = x_prompt.reshape(bsz * seq, D_MODEL)
    qt_p, kf_p, vf_p, kb_p, vt_p, yp_p, tail_p = _proj_prompt(
        xp, wts["w_qkvu"], wts["b_qkvu"], wts["wg_bd"], wts["s_pool"], bsz, seq)

    xs = x_sample.reshape(dbs * dseq, D_MODEL)
    state_pad = jnp.pad(state_pool[0], ((0, 0), (POOL_HIST - POOL_STATE, 0), (0, 0)))
    q_s, kf_s, vf_s, yp_s, tail_s = _proj_sample(
        xs, wts["w_qkvu"], wts["b_qkvu"], wts["wg_bd"], wts["s_pool"], state_pad, dbs, dseq, n_past)

    ya_p, ya_s = _attention(qt_p, kb_p, vt_p, q_s, kf_s, vf_s, cache_k, cache_v, page_table, lam_vecs, g_row,
                            bsz, seq, dseq, tq=MOE_TILE)

    tiles_p = (bsz * seq) // MOE_TILE
    tiles_s = (dbs * dseq) // MOE_TILE
    n_tiles = tiles_p + tiles_s
    x1f, xsort, route, cnt = _merge((xp, yp_p, ya_p), (xs, yp_s, ya_s), wts,
                                    attn_tile=lambda i: _pair_major_tile(i, seq // MOE_TILE))
    cnt_pad = cnt[:, 0].reshape(n_tiles, ROUTER_COLS)[:, N_EXPERT_GROUPS:N_EXPERT_GROUPS + N_EXPERTS]
    n_steps = (n_tiles * REGION_GRANULES + N_EXPERTS * (MOE_SLOTS - 1) + MOE_SLOTS - 1) // MOE_SLOTS
    gin, step_expert, loc = _granule_schedule(cnt_pad.astype(jnp.int32), n_steps)
    ysort = _experts(xsort, gin, step_expert, wts["w_gate_e"], wts["w_up_e"], wts["w_down_e"], n_steps)
    y_p = _final(x1f, route, ysort, loc, p_prompt[0].reshape(bsz * seq, D_PLE), wts, 0)
    y_s = _final(x1f, route, ysort, loc, p_sample[0].reshape(dbs * dseq, D_PLE), wts, tiles_p)

    drop = POOL_HIST - POOL_STATE
    return (y_p.reshape(bsz, seq, D_MODEL),
            y_s.reshape(dbs, dseq, D_MODEL),
            kf_p.reshape(1, bsz, seq, N_HEADS, V_DIM),
            vf_p.reshape(1, bsz, seq, N_HEADS, V_DIM),
            tail_p[None, :, drop:, :],
            kf_s.reshape(1, dbs, dseq, N_HEADS, V_DIM),
            vf_s.reshape(1, dbs, dseq, N_HEADS, V_DIM),
            tail_s[None, :, drop:, :])
```

```python
import functools
import math

import jax
import jax.numpy as jnp
from jax import lax
from jax.experimental import pallas as pl
from jax.experimental.pallas import tpu as pltpu

F32 = jnp.float32
BF16 = jnp.bfloat16

D_MODEL = 1024
N_HEADS = 4
HEAD_DIM = 64
V_DIM = 2 * HEAD_DIM
D_QK = N_HEADS * 2 * HEAD_DIM
D_ATTN = N_HEADS * V_DIM
D_POOL = 512
POOL_WINDOWS = (2, 4, 8, 16)
POOL_GROUP = D_POOL // len(POOL_WINDOWS)
POOL_STATE = max(POOL_WINDOWS) - 1
POOL_HIST = POOL_STATE + 1
N_EXPERT_GROUPS = 4
EXPERTS_PER_GROUP = 8
N_EXPERTS = N_EXPERT_GROUPS * EXPERTS_PER_GROUP
D_EXPERT = 256
D_PLE = 256
PAGE_SIZE = 128
LN_EPS = 1e-5
DEPTH = 1
DEEPNORM_ALPHA = (2 * DEPTH) ** 0.25
LAM_INIT = 0.8 - 0.6 * math.exp(-0.3 * 0)
D_QKVU = D_POOL + 2 * D_QK + D_ATTN
ROUTER_COLS = 128
TOP_K_IN_GROUP = 2
MOE_TILE = 512
MOE_GRANULE = 16
MOE_REGION = -(-(TOP_K_IN_GROUP * MOE_TILE + N_EXPERTS * (MOE_GRANULE - 1)) // 128) * 128
REGION_GRANULES = MOE_REGION // MOE_GRANULE
MOE_SLOTS = 16
LOG2E = 1.4426950408889634
DENOM_ROWS = 16
SAMPLES_PER_STEP = 2
SCORES_AHEAD = 2
COMBINE_PARTS = 2
V7X_VMEM_LIMIT = 56 * 1024 * 1024
NEG_INF = float("-inf")


def _sigmoid(x):
    return 1.0 / (1.0 + jnp.exp(-x))


def _layer_norm(x, g, b):
    mu = jnp.mean(x, axis=-1, keepdims=True)
    xc = x - mu
    var = jnp.mean(xc * xc, axis=-1, keepdims=True)
    return xc * lax.rsqrt(var + LN_EPS) * g + b


def _lam_value(lq1_ref, lk1_ref, lq2_ref, lk2_ref):
    a = jnp.sum(lq1_ref[...] * lk1_ref[...], axis=1, keepdims=True)
    b = jnp.sum(lq2_ref[...] * lk2_ref[...], axis=1, keepdims=True)
    return jnp.exp(a) - jnp.exp(b) + LAM_INIT


def _window_sums(e):
    outs = []
    for gi, w in enumerate(POOL_WINDOWS):
        s = e[:, gi * POOL_GROUP:(gi + 1) * POOL_GROUP]
        step = 1
        while step < w:
            s = s + pltpu.roll(s, step, axis=0)
            step *= 2
        outs.append(s)
    return outs


def _pool_branch(wins, u, inv_cnts, wg_ref, sp_ref):
    ds = []
    for gi in range(len(POOL_WINDOWS)):
        ds.append(wins[gi] * inv_cnts[gi] - u[:, gi * POOL_GROUP:(gi + 1) * POOL_GROUP])
    d = jnp.concatenate(ds, axis=1).astype(BF16)
    return jnp.dot(d, wg_ref[...], preferred_element_type=F32) * sp_ref[...]


def _store_head_major(ref, x, n_rows):
    for hh in range(N_HEADS):
        ref[pl.ds(hh, n_rows, stride=N_HEADS), :] = x[:, hh * V_DIM:(hh + 1) * V_DIM]


def _proj_prompt_kernel(x_ref, w_ref, b_ref, wg_ref, sp_ref,
                        qt_ref, kf_ref, vf_ref, kb_ref, vt_ref, yp_ref, tail_ref,
                        ext_ref, *, tm, tiles_per_seq):
    t_in_seq = pl.program_id(0) % tiles_per_seq
    z = jnp.dot(x_ref[...].astype(BF16), w_ref[...], preferred_element_type=F32) + b_ref[...]
    u = z[:, 0:D_POOL]
    q = z[:, D_POOL:D_POOL + D_QK]
    k = z[:, D_POOL + D_QK:D_POOL + 2 * D_QK]
    v = z[:, D_POOL + 2 * D_QK:D_QKVU]
    qt_ref[...] = (q * (HEAD_DIM ** -0.5 * LOG2E)).T.astype(BF16)
    vt_ref[...] = v.T.astype(BF16)
    kb_ref[...] = k.astype(BF16)
    _store_head_major(kf_ref, k, tm)
    _store_head_major(vf_ref, v, tm)

    @pl.when(t_in_seq == 0)
    def _():
        ext_ref[0:POOL_HIST, :] = jnp.zeros((POOL_HIST, D_POOL), F32)

    ext_ref[POOL_HIST:POOL_HIST + tm, :] = u
    wins = [s[POOL_HIST:, :] for s in _window_sums(ext_ref[...])]
    pos = t_in_seq * tm + lax.broadcasted_iota(jnp.int32, (tm, 1), 0)
    inv_cnts = [1.0 / jnp.minimum(pos + 1, w).astype(F32) for w in POOL_WINDOWS]
    yp_ref[...] = _pool_branch(wins, u, inv_cnts, wg_ref, sp_ref).astype(BF16)
    tail = u[tm - POOL_HIST:, :]
    ext_ref[0:POOL_HIST, :] = tail
    tail_ref[...] = tail


def _proj_sample_kernel(x_ref, w_ref, b_ref, wg_ref, sp_ref, st_ref,
                        q_ref, kf_ref, vf_ref, yp_ref, tail_ref,
                        ext_ref, *, n_seq, t_new, n_past):
    z = jnp.dot(x_ref[...].astype(BF16), w_ref[...], preferred_element_type=F32) + b_ref[...]
    u = z[:, 0:D_POOL]
    q_ref[...] = z[:, D_POOL:D_POOL + D_QK] * (HEAD_DIM ** -0.5)
    _store_head_major(kf_ref, z[:, D_POOL + D_QK:D_POOL + 2 * D_QK], n_seq * t_new)
    _store_head_major(vf_ref, z[:, D_POOL + 2 * D_QK:D_QKVU], n_seq * t_new)
    rows = POOL_HIST + t_new
    ext_ref[:, 0:POOL_HIST, :] = st_ref[...]
    ext_ref[:, POOL_HIST:rows, :] = u.reshape(n_seq, t_new, D_POOL)
    e3 = ext_ref[...]
    wins = [s.reshape(n_seq, rows, POOL_GROUP)[:, POOL_HIST:, :].reshape(n_seq * t_new, POOL_GROUP)
            for s in _window_sums(e3.reshape(n_seq * rows, D_POOL))]
    inv_cnts = [1.0 / float(min(n_past + 1, w)) for w in POOL_WINDOWS]
    yp_ref[...] = _pool_branch(wins, u, inv_cnts, wg_ref, sp_ref).astype(BF16)
    tail_ref[...] = e3[:, rows - POOL_HIST:, :]


def _attn_prompt_pair(slope, lam, qt_refs, k_ref, vt_ref, boff_ref, bdiag_ref, g_ref, o_ref,
                      q2t_ref, m_ref, acc_ref, *, tq, nq):
    pair = pl.program_id(2)
    tiles = (nq - 1 - pair, pair)
    half = tq // 2
    all_lanes = slice(0, 2 * tq)
    late_lanes = (slice(half, tq), slice(tq + half, 2 * tq))
    for slot, qt_ref in enumerate(qt_refs):
        qt = qt_ref[...]
        row = lax.broadcasted_iota(jnp.int32, qt.shape, 0)
        zero = jnp.zeros_like(qt)
        q2t_ref[slot] = jnp.concatenate([jnp.where(row < HEAD_DIM, qt, zero), jnp.where(row >= HEAD_DIM, qt, zero)],
                                        axis=1)

    def both_maps(bias):
        return jnp.concatenate([bias, bias], axis=1)

    def scores(slot, k_start, n_keys, lanes, bias):
        kblk = k_ref[pl.ds(k_start, n_keys), :]
        return jnp.dot(kblk, q2t_ref[slot, :, lanes], preferred_element_type=F32) + bias

    def absorb(t, slot, k_start, n_keys, lanes, shift, first):
        vt_aug = jnp.concatenate([vt_ref[:, pl.ds(k_start, n_keys)], jnp.ones((DENOM_ROWS, n_keys), BF16)], axis=0)
        m_new = jnp.max(t, axis=0, keepdims=True) + shift
        if not first:
            m_prev = m_ref[slot, :, lanes]
            m_new = jnp.maximum(m_prev, m_new)
        pv = jnp.dot(vt_aug, jnp.exp2(t - (m_new - shift)).astype(BF16), preferred_element_type=F32)
        acc_ref[slot, :, lanes] = pv if first else jnp.exp2(m_prev - m_new) * acc_ref[slot, :, lanes] + pv
        m_ref[slot, :, lanes] = m_new

    zero_shift = jnp.float32(0.0)
    tasks = []
    for slot in range(2):
        k_diag = pl.multiple_of(tiles[slot] * tq, tq)
        k_diag2 = pl.multiple_of(k_diag + half, half)
        tasks.append((slot, k_diag, half, all_lanes, lambda: both_maps(bdiag_ref[0:half, :]), zero_shift, True))
        for lanes in late_lanes:
            tasks.append((slot, k_diag2, half, lanes, lambda: bdiag_ref[half:tq, half:tq], zero_shift, False))
    for j in range(nq - 1):
        in_second = j >= tiles[0]
        kb = jnp.where(in_second, j - tiles[0], j)
        q_tile = jnp.where(in_second, tiles[1], tiles[0])
        shift = -(slope * LOG2E) * ((q_tile - kb) * tq).astype(F32)
        tasks.append((in_second.astype(jnp.int32), pl.multiple_of(kb * tq, tq), tq, all_lanes,
                      lambda: both_maps(boff_ref[...]), shift, False))
    queue = []
    for slot, k_start, n_keys, lanes, bias_fn, shift, first in tasks:
        queue.append((scores(slot, k_start, n_keys, lanes, bias_fn()), slot, k_start, n_keys, lanes, shift, first))
        if len(queue) > SCORES_AHEAD:
            absorb(*queue.pop(0))
    while queue:
        absorb(*queue.pop(0))

    for slot in range(2):
        acc = acc_ref[slot]
        on = acc[0:V_DIM, :] / acc[V_DIM:V_DIM + 1, :]
        o = (on[:, 0:tq] - lam * on[:, tq:]).T
        o = o * lax.rsqrt(jnp.mean(o * o, axis=1, keepdims=True) + LN_EPS) * g_ref[...] * (1.0 - LAM_INIT)
        o_ref[slot * tq:(slot + 1) * tq, :] = o.astype(BF16)


def _attn_sample_body(lam, sj, q_ref, kn_ref, vn_ref, bias_ref, g_ref, k_pages, v_pages, *, t_new):
    n_pages = len(k_pages)
    n_past = n_pages * PAGE_SIZE
    rows = slice(sj * t_new, (sj + 1) * t_new)
    q = q_ref[rows, :]
    lane = lax.broadcasted_iota(jnp.int32, (t_new, V_DIM), 1)
    pad = jnp.zeros((PAGE_SIZE - t_new, V_DIM), BF16)
    nt_dims = (((1,), (1,)), ((), ()))

    def head_rows(pages, new_ref, hh):
        past = jnp.concatenate([pg[pl.ds(hh, PAGE_SIZE, stride=N_HEADS), :] for pg in pages], axis=0)
        new = new_ref[pl.ds(sj * t_new * N_HEADS + hh, t_new, stride=N_HEADS), :]
        new = jnp.concatenate([new.astype(BF16), pad], axis=0)
        return past.astype(BF16), new

    s_rows = []
    for hh in range(N_HEADS):
        qh = q[:, hh * V_DIM:(hh + 1) * V_DIM]
        qh2 = jnp.concatenate([jnp.where(lane < HEAD_DIM, qh, 0.0), jnp.where(lane >= HEAD_DIM, qh, 0.0)],
                              axis=0).astype(BF16)
        k_past, k_new = head_rows(k_pages, kn_ref, hh)
        s_rows.append(jnp.concatenate(
            [lax.dot_general(qh2, k_past, nt_dims, preferred_element_type=F32),
             lax.dot_general(qh2, k_new, nt_dims, preferred_element_type=F32)], axis=1))
    s = jnp.concatenate(s_rows, axis=0) + bias_ref[...]
    m = jnp.max(s, axis=1, keepdims=True)
    p = jnp.exp(s - m)
    p = p / jnp.sum(p, axis=1, keepdims=True)
    outs = []
    for h0 in range(0, N_HEADS, 2):
        a2 = jnp.concatenate(
            [p[hh * 2 * t_new:hh * 2 * t_new + t_new, :] - lam * p[hh * 2 * t_new + t_new:(hh + 1) * 2 * t_new, :]
             for hh in (h0, h0 + 1)], axis=0).astype(BF16)
        (vp0, vn0), (vp1, vn1) = head_rows(v_pages, vn_ref, h0), head_rows(v_pages, vn_ref, h0 + 1)
        o2 = (jnp.dot(a2[:, 0:n_past], jnp.concatenate([vp0, vp1], axis=1), preferred_element_type=F32)
              + jnp.dot(a2[:, n_past:], jnp.concatenate([vn0, vn1], axis=1), preferred_element_type=F32))
        for k in range(2):
            oh = o2[k * t_new:(k + 1) * t_new, k * V_DIM:(k + 1) * V_DIM]
            oh = oh * lax.rsqrt(jnp.mean(oh * oh, axis=1, keepdims=True) + LN_EPS) * g_ref[...] * (1.0 - LAM_INIT)
            outs.append(oh)
    return jnp.concatenate(outs, axis=1)


def _attn_kernel(pt_ref, slopes_ref, lq1_ref, lk1_ref, lq2_ref, lk2_ref,
                 qta_ref, qtb_ref, k_ref, vt_ref, boff_ref, bdiag_ref, g_ref, bias_s_ref, *rest,
                 n_pages, t_new, tq, nq):
    qs_ref, kn_ref, vn_ref = rest[:3]
    pages = rest[3:3 + SAMPLES_PER_STEP * 2 * n_pages]
    op_ref, os_ref, q2t_ref, m_ref, acc_ref = rest[3 + SAMPLES_PER_STEP * 2 * n_pages:]
    lam = _lam_value(lq1_ref, lk1_ref, lq2_ref, lk2_ref)
    def sample(sj):
        pg = pages[sj * 2 * n_pages:(sj + 1) * 2 * n_pages]
        return _attn_sample_body(lam, sj, qs_ref, kn_ref, vn_ref, bias_s_ref, g_ref, pg[:n_pages], pg[n_pages:],
                                 t_new=t_new)

    outs = [sample(sj) for sj in range(SAMPLES_PER_STEP // 2)]
    _attn_prompt_pair(slopes_ref[pl.program_id(1)], lam, (qta_ref, qtb_ref), k_ref, vt_ref, boff_ref, bdiag_ref,
                      g_ref, op_ref, q2t_ref, m_ref, acc_ref, tq=tq, nq=nq)
    outs += [sample(sj) for sj in range(SAMPLES_PER_STEP // 2, SAMPLES_PER_STEP)]
    os_ref[...] = jnp.concatenate(outs, axis=0).astype(BF16)


def _merge_kernel(xp_ref, ypp_ref, yap_ref, xs_in_ref, yps_ref, yas_ref,
                  wgate_ref, bgate_ref, wpb_ref, wab_ref, wo_ref,
                  g1_ref, b1_ref, wrt_hi_ref, wrt_lo_ref, brt_ref, upper_ref, lower_ref,
                  x1f_ref, xs_ref, route_ref, cnt_ref, *, tm, prompt_tiles):
    is_sample = pl.program_id(0) >= prompt_tiles
    x = jnp.where(is_sample, xs_in_ref[...], xp_ref[...])
    y_pool = jnp.where(is_sample, yps_ref[...], ypp_ref[...])
    y_attn = jnp.where(is_sample, yas_ref[...], yap_ref[...])
    zg = jnp.dot(x.astype(BF16), wgate_ref[...], preferred_element_type=F32) + bgate_ref[...]
    gp = _sigmoid(zg[:, 0:D_MODEL])
    ga = _sigmoid(zg[:, D_MODEL:])
    merged = (gp * jnp.dot(y_pool, wpb_ref[...], preferred_element_type=F32)
              + ga * jnp.dot(y_attn, wab_ref[...], preferred_element_type=F32))
    x1 = _layer_norm(DEEPNORM_ALPHA * x + jnp.dot(merged.astype(BF16), wo_ref[...], preferred_element_type=F32),
                     g1_ref[...], b1_ref[...])
    x1f_ref[...] = x1
    x1_hi = x1.astype(BF16)
    x1_lo = (x1 - x1_hi.astype(F32)).astype(BF16)
    nt_dims = (((1,), (1,)), ((), ()))
    lg = (lax.dot_general(wrt_hi_ref[...], x1_hi, nt_dims, preferred_element_type=F32)
          + lax.dot_general(wrt_hi_ref[...], x1_lo, nt_dims, preferred_element_type=F32)
          + lax.dot_general(wrt_lo_ref[...], x1_hi, nt_dims, preferred_element_type=F32)) + brt_ref[...]
    row = lax.broadcasted_iota(jnp.int32, lg.shape, 0)
    big = jnp.int32(4 * ROUTER_COLS)
    lgg = jnp.where(row < N_EXPERT_GROUPS, lg, NEG_INF)
    gmax = jnp.max(lgg, axis=0, keepdims=True)
    g_w = 1.0 / jnp.sum(jnp.exp(lgg - gmax), axis=0, keepdims=True)
    g_idx = jnp.min(jnp.where(lgg == gmax, row, big), axis=0, keepdims=True)
    lo_row = N_EXPERT_GROUPS + g_idx * EXPERTS_PER_GROUP
    le = jnp.where((row >= lo_row) & (row < lo_row + EXPERTS_PER_GROUP), lg, NEG_INF)
    v1 = jnp.max(le, axis=0, keepdims=True)
    i1 = jnp.min(jnp.where(le == v1, row, big), axis=0, keepdims=True)
    le2 = jnp.where(row == i1, NEG_INF, le)
    v2 = jnp.max(le2, axis=0, keepdims=True)
    i2 = jnp.min(jnp.where(le2 == v2, row, big), axis=0, keepdims=True)
    e21 = jnp.exp(v2 - v1)
    c1 = g_w / (1.0 + e21)
    c2 = g_w * e21 / (1.0 + e21)
    sel1 = row == i1
    sel2 = row == i2
    member = jnp.where(sel1 | sel2, 1.0, 0.0)
    before = jnp.dot(member.astype(BF16), upper_ref[...], preferred_element_type=F32)
    cnt = jnp.sum(member, axis=1, keepdims=True)
    cnt_pad = jnp.ceil(cnt * (1.0 / MOE_GRANULE)) * MOE_GRANULE
    cnt_b = jnp.broadcast_to(cnt_pad, (ROUTER_COLS, ROUTER_COLS))
    seg_start = jnp.dot(lower_ref[...], cnt_b.astype(BF16), preferred_element_type=F32)
    cnt_ref[...] = cnt_b
    slot = seg_start[:, 0:1] + before
    pos1 = jnp.sum(jnp.where(sel1, slot, 0.0), axis=0, keepdims=True)
    pos2 = jnp.sum(jnp.where(sel2, slot, 0.0), axis=0, keepdims=True)
    srow = lax.broadcasted_iota(jnp.int32, (xs_ref.shape[0], tm), 0).astype(F32)
    perm = jnp.where((srow == pos1) | (srow == pos2), 1.0, 0.0).astype(BF16)
    xs_ref[...] = jnp.dot(perm, x1_hi, preferred_element_type=F32).astype(BF16)
    info = (jnp.where(row == 0, pos1, 0.0) + jnp.where(row == 1, pos2, 0.0)
            + jnp.where(row == 2, c1, 0.0) + jnp.where(row == 3, c2, 0.0))
    route_ref[...] = info.T


def _experts_kernel(gin_ref, sexp_ref, xs_hbm, wg_ref, wu_ref, wd_ref, y_ref,
                    xbuf, sems, wg_b, wu_b, wd_b, *, n_slots, n_steps):
    t = pl.program_id(0)
    expert = sexp_ref[t]
    is_expert = expert < N_EXPERTS
    cur = t % 2

    def gather(step, slot):
        for k in range(n_slots):
            src = pl.multiple_of(gin_ref[step * n_slots + k] * MOE_GRANULE, MOE_GRANULE)
            pltpu.make_async_copy(xs_hbm.at[pl.ds(src, MOE_GRANULE), :],
                                  xbuf.at[slot, pl.ds(k * MOE_GRANULE, MOE_GRANULE), :], sems.at[slot]).start()

    @pl.when((t == 0) & is_expert)
    def _():
        gather(0, 0)

    nxt = jnp.minimum(t + 1, n_steps - 1)

    @pl.when((t + 1 < n_steps) & (sexp_ref[nxt] < N_EXPERTS))
    def _():
        gather(nxt, 1 - cur)

    @pl.when(is_expert & ((t == 0) | (expert != sexp_ref[jnp.maximum(t - 1, 0)])))
    def _():
        wg_b[...] = wg_ref[...].astype(BF16)
        wu_b[...] = wu_ref[...].astype(BF16)
        wd_b[...] = wd_ref[...].astype(BF16)

    @pl.when(is_expert)
    def _():
        pltpu.make_async_copy(xbuf.at[cur], xbuf.at[cur], sems.at[cur]).wait()
        rows = (n_slots // 2) * MOE_GRANULE

        def gate_up(row0):
            xg = xbuf[cur, row0:row0 + rows, :]
            return (jnp.dot(xg, wg_b[...], preferred_element_type=F32),
                    jnp.dot(xg, wu_b[...], preferred_element_type=F32))

        def act_down(gate, up):
            hact = (gate * _sigmoid(gate) * up).astype(BF16)
            return jnp.dot(hact, wd_b[...], preferred_element_type=F32).astype(BF16)

        gate_a, up_a = gate_up(0)
        gate_b, up_b = gate_up(rows)
        y_ref[0:rows, :] = act_down(gate_a, up_a)
        y_ref[rows:, :] = act_down(gate_b, up_b)

    @pl.when(jnp.logical_not(is_expert))
    def _():
        y_ref[...] = jnp.zeros(y_ref.shape, BF16)


def _final_kernel(loc_ref, x1f_ref, route_ref, *rest, n_granules):
    g_refs = rest[:n_granules]
    p_ref, g2_ref, b2_ref, wpg_ref, bpg_ref, wple_ref, g3_ref, b3_ref, y_ref = rest[n_granules:]
    ys = jnp.concatenate([r[...] for r in g_refs], axis=0)
    tm = route_ref.shape[0]
    part = tm // COMBINE_PARTS
    parts = [slice(k * part, (k + 1) * part) for k in range(COMBINE_PARTS)]
    scol = lax.broadcasted_iota(jnp.int32, (part, ys.shape[0]), 1).astype(F32)

    def combine(rows):
        route = route_ref[rows, :]
        comb = (jnp.where(scol == route[:, 0:1], route[:, 2:3], 0.0)
                + jnp.where(scol == route[:, 1:2], route[:, 3:4], 0.0)).astype(BF16)
        return jnp.dot(comb, ys, preferred_element_type=F32)

    def ple(rows, moe):
        x2 = _layer_norm(DEEPNORM_ALPHA * x1f_ref[rows, :] + moe, g2_ref[...], b2_ref[...])
        z = jnp.dot(x2.astype(BF16), wpg_ref[...], preferred_element_type=F32) + bpg_ref[...]
        pe = jnp.dot(p_ref[rows, :].astype(BF16), wple_ref[...], preferred_element_type=F32)
        return x2, z, pe

    def finish(rows, x2, z, pe):
        y_ref[rows, :] = _layer_norm(DEEPNORM_ALPHA * x2 + _sigmoid(z) * pe, g3_ref[...], b3_ref[...])

    moe, mid = {}, {}
    for k in range(COMBINE_PARTS + 2):
        if k < COMBINE_PARTS:
            moe[k] = combine(parts[k])
        if 0 <= k - 1 < COMBINE_PARTS:
            mid[k - 1] = ple(parts[k - 1], moe.pop(k - 1))
        if 0 <= k - 2 < COMBINE_PARTS:
            finish(parts[k - 2], *mid.pop(k - 2))


def _full(shape):
    nd = len(shape)
    return pl.BlockSpec(shape, lambda *_: (0,) * nd)


def _cparams(sem):
    return pltpu.CompilerParams(dimension_semantics=sem, vmem_limit_bytes=V7X_VMEM_LIMIT)


def _proj_prompt(x2d, w_qkvu, b_qkvu, wg_bd, s_pool, n_seq, seq_len, tm=512):
    n = x2d.shape[0]
    tps = seq_len // tm
    row = lambda i: (i, 0)
    col = lambda i: (0, i)
    outs = pl.pallas_call(
        functools.partial(_proj_prompt_kernel, tm=tm, tiles_per_seq=tps),
        grid=(n // tm,),
        in_specs=[pl.BlockSpec((tm, D_MODEL), row), _full(w_qkvu.shape), _full(b_qkvu.shape),
                  _full(wg_bd.shape), _full(s_pool.shape)],
        out_specs=[pl.BlockSpec((D_QK, tm), col), pl.BlockSpec((tm * N_HEADS, V_DIM), row),
                   pl.BlockSpec((tm * N_HEADS, V_DIM), row), pl.BlockSpec((tm, D_QK), row),
                   pl.BlockSpec((D_ATTN, tm), col), pl.BlockSpec((tm, D_POOL), row),
                   pl.BlockSpec((None, POOL_HIST, D_POOL), lambda i: (i // tps, 0, 0))],
        out_shape=[jax.ShapeDtypeStruct((D_QK, n), BF16), jax.ShapeDtypeStruct((n * N_HEADS, V_DIM), F32),
                   jax.ShapeDtypeStruct((n * N_HEADS, V_DIM), F32), jax.ShapeDtypeStruct((n, D_QK), BF16),
                   jax.ShapeDtypeStruct((D_ATTN, n), BF16), jax.ShapeDtypeStruct((n, D_POOL), BF16),
                   jax.ShapeDtypeStruct((n_seq, POOL_HIST, D_POOL), F32)],
        scratch_shapes=[pltpu.VMEM((POOL_HIST + tm, D_POOL), F32)],
        compiler_params=_cparams(("arbitrary",)),
        name="proj_pool_prompt",
    )(x2d, w_qkvu, b_qkvu, wg_bd, s_pool)
    return outs


def _proj_sample(x2d, w_qkvu, b_qkvu, wg_bd, s_pool, state_pad, n_seq, t_new, n_past):
    n = x2d.shape[0]
    rows = POOL_HIST + t_new
    return pl.pallas_call(
        functools.partial(_proj_sample_kernel, n_seq=n_seq, t_new=t_new, n_past=n_past),
        grid=(1,),
        in_specs=[_full(x2d.shape), _full(w_qkvu.shape), _full(b_qkvu.shape), _full(wg_bd.shape),
                  _full(s_pool.shape), _full(state_pad.shape)],
        out_specs=[_full((n, D_QK)), _full((n * N_HEADS, V_DIM)), _full((n * N_HEADS, V_DIM)),
                   _full((n, D_POOL)), _full((n_seq, POOL_HIST, D_POOL))],
        out_shape=[jax.ShapeDtypeStruct((n, D_QK), F32), jax.ShapeDtypeStruct((n * N_HEADS, V_DIM), F32),
                   jax.ShapeDtypeStruct((n * N_HEADS, V_DIM), F32), jax.ShapeDtypeStruct((n, D_POOL), BF16),
                   jax.ShapeDtypeStruct((n_seq, POOL_HIST, D_POOL), F32)],
        scratch_shapes=[pltpu.VMEM((n_seq, rows, D_POOL), F32)],
        compiler_params=_cparams(("arbitrary",)),
        name="proj_pool_sample",
    )(x2d, w_qkvu, b_qkvu, wg_bd, s_pool, state_pad)


def _alibi_slopes():
    return 2.0 ** (-8.0 * jnp.arange(1, N_HEADS + 1, dtype=F32) / N_HEADS)


def _attention(qt, kb, vt, q_s, k_new, v_new, cache_k, cache_v, page_table, lam_vecs, g_sub,
               n_seq, seq_len, t_new, tq=512):
    n = kb.shape[0]
    nq = seq_len // tq
    n_pairs = nq // 2
    n_smp, n_pages = page_table.shape
    n_past = n_pages * PAGE_SIZE
    assert nq % 2 == 0 and n_smp == SAMPLES_PER_STEP * n_seq * N_HEADS * n_pairs, (n_smp, n_seq, nq)
    slopes = _alibi_slopes()
    rel = (jnp.arange(tq, dtype=jnp.int32)[None, :] - jnp.arange(tq, dtype=jnp.int32)[:, None])
    b_off = -(slopes * LOG2E)[:, None, None] * rel.astype(F32)[None]
    b_diag = jnp.where(rel[None] >= 0, b_off, NEG_INF)
    r = jnp.arange(2 * N_HEADS * t_new, dtype=jnp.int32)
    r_head, r_q = r // (2 * t_new), r % t_new
    row_slope = slopes[r_head][:, None]
    kpos = jnp.arange(n_past, dtype=jnp.int32)[None, :]
    b_past = -row_slope * (n_past + r_q[:, None] - kpos).astype(F32)
    j = jnp.arange(PAGE_SIZE, dtype=jnp.int32)[None, :]
    dist_new = r_q[:, None] - j
    b_new = jnp.where((dist_new >= 0) & (j < t_new), -row_slope * dist_new.astype(F32), NEG_INF)
    bias_s = jnp.concatenate([b_past, b_new], axis=1)
    page_rows = PAGE_SIZE * N_HEADS
    ck = cache_k.reshape(-1, V_DIM)
    cv = cache_v.reshape(-1, V_DIM)
    pt = page_table.reshape(-1).astype(jnp.int32)
    step = lambda b, h, i: (b * N_HEADS + h) * n_pairs + i
    const2 = lambda b, h, i, pt_ref: (0, 0)
    vec = pl.BlockSpec((1, HEAD_DIM), const2)
    smp_rows = SAMPLES_PER_STEP * t_new
    tok = pl.BlockSpec((smp_rows, D_QK), lambda b, h, i, pt_ref: (step(b, h, i), 0))
    tok_hm = pl.BlockSpec((smp_rows * N_HEADS, V_DIM), lambda b, h, i, pt_ref: (step(b, h, i), 0))
    table = pl.BlockSpec((None, tq, tq), lambda b, h, i, pt_ref: (h, 0, 0))

    def page_spec(sj, pi):
        return pl.BlockSpec((page_rows, V_DIM), lambda b, h, i, pt_ref: (
            pt_ref[(step(b, h, i) * SAMPLES_PER_STEP + sj) * n_pages + pi], 0))

    pages = []
    for sj in range(SAMPLES_PER_STEP):
        pages += [(page_spec(sj, pi), ck) for pi in range(n_pages)] + [(page_spec(sj, pi), cv) for pi in range(n_pages)]
    grid_spec = pltpu.PrefetchScalarGridSpec(
        num_scalar_prefetch=1,
        grid=(n_seq, N_HEADS, n_pairs),
        in_specs=[pl.BlockSpec(memory_space=pltpu.SMEM), vec, vec, vec, vec,
                  pl.BlockSpec((V_DIM, tq), lambda b, h, i, pt_ref: (h, b * nq + nq - 1 - i)),
                  pl.BlockSpec((V_DIM, tq), lambda b, h, i, pt_ref: (h, b * nq + i)),
                  pl.BlockSpec((seq_len, V_DIM), lambda b, h, i, pt_ref: (b, h)),
                  pl.BlockSpec((V_DIM, seq_len), lambda b, h, i, pt_ref: (h, b)),
                  table, table, pl.BlockSpec((1, V_DIM), const2), pl.BlockSpec(bias_s.shape, const2),
                  tok, tok_hm, tok_hm] + [spec for spec, _ in pages],
        out_specs=[pl.BlockSpec((2 * tq, V_DIM), lambda b, h, i, pt_ref: (b * n_pairs + i, h)), tok],
        scratch_shapes=[pltpu.VMEM((2, V_DIM, 2 * tq), BF16), pltpu.VMEM((2, 1, 2 * tq), F32),
                        pltpu.VMEM((2, V_DIM + DENOM_ROWS, 2 * tq), F32)],
    )
    return pl.pallas_call(
        functools.partial(_attn_kernel, n_pages=n_pages, t_new=t_new, tq=tq, nq=nq),
        grid_spec=grid_spec,
        out_shape=[jax.ShapeDtypeStruct((n, D_ATTN), BF16), jax.ShapeDtypeStruct((n_smp * t_new, D_ATTN), BF16)],
        compiler_params=_cparams(("arbitrary", "arbitrary", "arbitrary")),
        name="attention",
    )(pt, slopes, *lam_vecs, qt, qt, kb, vt, b_off, b_diag, g_sub, bias_s, q_s, k_new, v_new,
      *[arr for _, arr in pages])


def _pair_major_tile(i, nq):
    b, qi = i // nq, i % nq
    return b * nq + jnp.where(qi >= nq // 2, 2 * (nq - 1 - qi), 2 * qi + 1)


def _merge(prompt, sample, wts, attn_tile):
    tm = MOE_TILE
    tiles_p = prompt[0].shape[0] // tm
    n_tiles = tiles_p + sample[0].shape[0] // tm
    p_row = lambda i: (jnp.minimum(i, tiles_p - 1), 0)
    s_row = lambda i: (jnp.maximum(i - tiles_p, 0), 0)
    row = lambda i: (i, 0)
    consts = [wts[k] for k in ("w_gate", "b_gate", "w_pool_br", "w_attn_br", "w_o", "ln1_g", "ln1_b",
                               "wrt_hi", "wrt_lo", "b_rt", "upper", "lower")]
    return pl.pallas_call(
        functools.partial(_merge_kernel, tm=tm, prompt_tiles=tiles_p),
        grid=(n_tiles,),
        in_specs=[pl.BlockSpec((tm, D_MODEL), p_row), pl.BlockSpec((tm, D_POOL), p_row),
                  pl.BlockSpec((tm, D_ATTN), lambda i: (attn_tile(jnp.minimum(i, tiles_p - 1)), 0)),
                  pl.BlockSpec((tm, D_MODEL), s_row), pl.BlockSpec((tm, D_POOL), s_row),
                  pl.BlockSpec((tm, D_ATTN), s_row)] + [_full(c.shape) for c in consts],
        out_specs=[pl.BlockSpec((tm, D_MODEL), row), pl.BlockSpec((MOE_REGION, D_MODEL), row),
                   pl.BlockSpec((tm, ROUTER_COLS), row), pl.BlockSpec((ROUTER_COLS, ROUTER_COLS), row)],
        out_shape=[jax.ShapeDtypeStruct((n_tiles * tm, D_MODEL), F32),
                   jax.ShapeDtypeStruct((n_tiles * MOE_REGION, D_MODEL), BF16),
                   jax.ShapeDtypeStruct((n_tiles * tm, ROUTER_COLS), F32),
                   jax.ShapeDtypeStruct((n_tiles * ROUTER_COLS, ROUTER_COLS), F32)],
        compiler_params=_cparams(("arbitrary",)),
        name="merge_ln1_route_dispatch",
    )(*prompt, *sample, *consts)


def _granule_schedule(cnt_pad, n_steps):
    n_tiles = cnt_pad.shape[0]
    gc = cnt_pad // MOE_GRANULE
    seg = jnp.cumsum(gc, axis=1) - gc
    tot = jnp.sum(gc, axis=0)
    ahead = jnp.cumsum(gc, axis=0) - gc
    slots_e = ((tot + MOE_SLOTS - 1) // MOE_SLOTS) * MOE_SLOTS
    first_slot = jnp.cumsum(slots_e) - slots_e
    p0 = (jnp.arange(n_steps, dtype=jnp.int32) * MOE_SLOTS)[None, :]
    in_e = ((p0 >= first_slot[:, None]) & (p0 < (first_slot + slots_e)[:, None])).astype(jnp.int32)
    q0 = p0 - jnp.sum(in_e * first_slot[:, None], axis=0, keepdims=True)
    tot_s = jnp.sum(in_e * tot[:, None], axis=0, keepdims=True)
    e_id = jnp.sum(in_e * jnp.arange(N_EXPERTS, dtype=jnp.int32)[:, None], axis=0)
    step_expert = jnp.where(q0[0] < tot_s[0], e_id, N_EXPERTS).astype(jnp.int32)
    pick = lambda tab: jnp.sum(tab[:, :, None] * in_e[None, :, :], axis=1)[:, :, None]
    ahead_s, gc_s, seg_s = pick(ahead), pick(gc), pick(seg)
    q = q0[0][None, :, None] + jnp.arange(MOE_SLOTS, dtype=jnp.int32)[None, None, :]
    in_t = ((q >= ahead_s) & (q < ahead_s + gc_s)).astype(jnp.int32)
    tile_base = (jnp.arange(n_tiles, dtype=jnp.int32) * REGION_GRANULES)[:, None, None]
    gid = jnp.sum(in_t * (tile_base + seg_s + q - ahead_s), axis=0)
    gin = jnp.where(q[0] < tot_s[0][:, None], gid, 0).astype(jnp.int32).reshape(-1)
    s = jnp.arange(REGION_GRANULES, dtype=jnp.int32)[None, None, :]
    in_g = ((s >= seg[:, :, None]) & (s < (seg + gc)[:, :, None])).astype(jnp.int32)
    loc = jnp.sum(in_g * ((first_slot[None, :] + ahead - seg)[:, :, None] + s), axis=1)
    used = jnp.sum(in_g, axis=1) > 0
    loc = jnp.where(used, loc, loc[:, 0:1]).astype(jnp.int32)
    return gin, step_expert, loc.reshape(-1)


def _experts(xs, gin, step_expert, w_gate, w_up, w_down, n_steps):
    rows = MOE_SLOTS * MOE_GRANULE
    w_sel = lambda t, gin_ref, se_ref: (jnp.minimum(se_ref[t], N_EXPERTS - 1), 0, 0)
    grid_spec = pltpu.PrefetchScalarGridSpec(
        num_scalar_prefetch=2,
        grid=(n_steps,),
        in_specs=[pl.BlockSpec(memory_space=pl.ANY),
                  pl.BlockSpec((None, D_MODEL, D_EXPERT), w_sel), pl.BlockSpec((None, D_MODEL, D_EXPERT), w_sel),
                  pl.BlockSpec((None, D_EXPERT, D_MODEL), w_sel)],
        out_specs=pl.BlockSpec((rows, D_MODEL), lambda t, gin_ref, se_ref: (t, 0)),
        scratch_shapes=[pltpu.VMEM((2, rows, D_MODEL), BF16), pltpu.SemaphoreType.DMA((2,)),
                        pltpu.VMEM((D_MODEL, D_EXPERT), BF16), pltpu.VMEM((D_MODEL, D_EXPERT), BF16),
                        pltpu.VMEM((D_EXPERT, D_MODEL), BF16)],
    )
    return pl.pallas_call(
        functools.partial(_experts_kernel, n_slots=MOE_SLOTS, n_steps=n_steps),
        grid_spec=grid_spec,
        out_shape=jax.ShapeDtypeStruct((n_steps * rows, D_MODEL), BF16),
        compiler_params=_cparams(("arbitrary",)),
        name="expert_mlps",
    )(gin, step_expert, xs, w_gate, w_up, w_down)


def _final(x1f, route, ys, loc, p2d, wts, tile0):
    tm = MOE_TILE
    n = p2d.shape[0]
    consts = [wts[k] for k in ("ln2_g", "ln2_b", "w_ple_gate", "b_ple_gate", "w_ple", "ln3_g", "ln3_b")]

    def granule_spec(s):
        return pl.BlockSpec((MOE_GRANULE, D_MODEL),
                            lambda i, loc_ref: (loc_ref[(i + tile0) * REGION_GRANULES + s], 0))

    grid_spec = pltpu.PrefetchScalarGridSpec(
        num_scalar_prefetch=1,
        grid=(n // tm,),
        in_specs=[pl.BlockSpec((tm, D_MODEL), lambda i, loc_ref: (i + tile0, 0)),
                  pl.BlockSpec((tm, ROUTER_COLS), lambda i, loc_ref: (i + tile0, 0))]
                 + [granule_spec(s) for s in range(REGION_GRANULES)]
                 + [pl.BlockSpec((tm, D_PLE), lambda i, loc_ref: (i, 0))]
                 + [pl.BlockSpec(c.shape, lambda i, loc_ref: (0, 0)) for c in consts],
        out_specs=pl.BlockSpec((tm, D_MODEL), lambda i, loc_ref: (i, 0)),
    )
    return pl.pallas_call(
        functools.partial(_final_kernel, n_granules=REGION_GRANULES),
        grid_spec=grid_spec,
        out_shape=jax.ShapeDtypeStruct((n, D_MODEL), F32),
        compiler_params=_cparams(("arbitrary",)),
        name="combine_ln2_ple_ln3",
    )(loc, x1f, route, *([ys] * REGION_GRANULES), p2d, *consts)


def _prepare_weights(w_in, b_in, w_pool_grp, s_pool, w_pool_br, w_attn_br, w_o, ln1_g, ln1_b,
                     w_rg, b_rg, w_re, b_re, w_gate, w_up, w_down, ln2_g, ln2_b,
                     w_ple_gate, b_ple_gate, w_ple, ln3_g, ln3_b):
    i = 0
    row = lambda a: a[i].reshape(1, -1).astype(F32)
    wg_bd = jnp.zeros((D_POOL, D_POOL), F32)
    for gi in range(len(POOL_WINDOWS)):
        sl = slice(gi * POOL_GROUP, (gi + 1) * POOL_GROUP)
        wg_bd = wg_bd.at[sl, sl].set(w_pool_grp[i, gi])
    w_r = jnp.zeros((D_MODEL, ROUTER_COLS), F32)
    w_r = w_r.at[:, 0:N_EXPERT_GROUPS].set(w_rg[i])
    w_r = w_r.at[:, N_EXPERT_GROUPS:N_EXPERT_GROUPS + N_EXPERTS].set(w_re[i].reshape(D_MODEL, N_EXPERTS))
    b_r = jnp.zeros((1, ROUTER_COLS), F32)
    b_r = b_r.at[0, 0:N_EXPERT_GROUPS].set(b_rg[i])
    b_r = b_r.at[0, N_EXPERT_GROUPS:N_EXPERT_GROUPS + N_EXPERTS].set(b_re[i].reshape(-1))
    wrt_hi = w_r.T.astype(BF16)
    wrt_lo = (w_r.T - wrt_hi.astype(F32)).astype(BF16)
    tok = jnp.arange(MOE_TILE, dtype=jnp.int32)
    upper = (tok[:, None] < tok[None, :]).astype(BF16)
    rr = jnp.arange(ROUTER_COLS, dtype=jnp.int32)
    lower = (rr[None, :] < rr[:, None]).astype(BF16)
    return dict(
        w_gate_e=w_gate[i].reshape(N_EXPERTS, D_MODEL, D_EXPERT), w_up_e=w_up[i].reshape(N_EXPERTS, D_MODEL, D_EXPERT),
        w_down_e=w_down[i].reshape(N_EXPERTS, D_EXPERT, D_MODEL),
        wrt_hi=wrt_hi, wrt_lo=wrt_lo, b_rt=b_r.reshape(ROUTER_COLS, 1), upper=upper, lower=lower,
        w_qkvu=w_in[i, :, 0:D_QKVU].astype(BF16), b_qkvu=b_in[i, 0:D_QKVU].reshape(1, -1),
        w_gate=w_in[i, :, D_QKVU:].astype(BF16), b_gate=b_in[i, D_QKVU:].reshape(1, -1),
        wg_bd=wg_bd.astype(BF16), s_pool=row(s_pool),
        w_pool_br=w_pool_br[i].astype(BF16), w_attn_br=w_attn_br[i].astype(BF16), w_o=w_o[i].astype(BF16),
        ln1_g=row(ln1_g), ln1_b=row(ln1_b), ln2_g=row(ln2_g), ln2_b=row(ln2_b),
        w_ple_gate=w_ple_gate[i].astype(BF16), b_ple_gate=row(b_ple_gate), w_ple=w_ple[i].astype(BF16),
        ln3_g=row(ln3_g), ln3_b=row(ln3_b))


def kernel(x_prompt, x_sample, p_prompt, p_sample, cache_k, cache_v, state_pool, page_table, w_in, b_in, lam_q1, lam_k1, lam_q2, lam_k2, g_sub, w_pool_grp, s_pool, w_pool_br, w_attn_br, w_o, ln1_g, ln1_b, w_rg, b_rg, w_re, b_re, w_gate, w_up, w_down, ln2_g, ln2_b, w_ple_gate, b_ple_gate, w_ple, ln3_g, ln3_b):
    assert w_in.shape[0] == DEPTH == 1
    bsz, seq, _ = x_prompt.shape
    dbs, dseq, _ = x_sample.shape
    n_past = page_table.shape[1] * PAGE_SIZE
    wts = _prepare_weights(w_in, b_in, w_pool_grp, s_pool, w_pool_br, w_attn_br, w_o, ln1_g, ln1_b,
                           w_rg, b_rg, w_re, b_re, w_gate, w_up, w_down, ln2_g, ln2_b,
                           w_ple_gate, b_ple_gate, w_ple, ln3_g, ln3_b)
    lam_vecs = [a[0].reshape(1, HEAD_DIM).astype(F32) for a in (lam_q1, lam_k1, lam_q2, lam_k2)]
    g_row = g_sub[0].reshape(1, V_DIM).astype(F32)

    xp = x_prompt.reshape(bsz * seq, D_MODEL)
    qt_p, kf_p, vf_p, kb_p, vt_p, yp_p, tail_p = _proj_prompt(
        xp, wts["w_qkvu"], wts["b_qkvu"], wts["wg_bd"], wts["s_pool"], bsz, seq)

    xs = x_sample.reshape(dbs * dseq, D_MODEL)
    state_pad = jnp.pad(state_pool[0], ((0, 0), (POOL_HIST - POOL_STATE, 0), (0, 0)))
    q_s, kf_s, vf_s, yp_s, tail_s = _proj_sample(
        xs, wts["w_qkvu"], wts["b_qkvu"], wts["wg_bd"], wts["s_pool"], state_pad, dbs, dseq, n_past)

    ya_p, ya_s = _attention(qt_p, kb_p, vt_p, q_s, kf_s, vf_s, cache_k, cache_v, page_table, lam_vecs, g_row,
                            bsz, seq, dseq, tq=MOE_TILE)

    tiles_p = (bsz * seq) // MOE_TILE
    tiles_s = (dbs * dseq) // MOE_TILE
    n_tiles = tiles_p + tiles_s
    x1f, xsort, route, cnt = _merge((xp, yp_p, ya_p), (xs, yp_s, ya_s), wts,
                                    attn_tile=lambda i: _pair_major_tile(i, seq // MOE_TILE))
    cnt_pad = cnt[:, 0].reshape(n_tiles, ROUTER_COLS)[:, N_EXPERT_GROUPS:N_EXPERT_GROUPS + N_EXPERTS]
    n_steps = (n_tiles * REGION_GRANULES + N_EXPERTS * (MOE_SLOTS - 1) + MOE_SLOTS - 1) // MOE_SLOTS
    gin, step_expert, loc = _granule_schedule(cnt_pad.astype(jnp.int32), n_steps)
    ysort = _experts(xsort, gin, step_expert, wts["w_gate_e"], wts["w_up_e"], wts["w_down_e"], n_steps)
    y_p = _final(x1f, route, ysort, loc, p_prompt[0].reshape(bsz * seq, D_PLE), wts, 0)
    y_s = _final(x1f, route, ysort, loc, p_sample[0].reshape(dbs * dseq, D_PLE), wts, tiles_p)

    drop = POOL_HIST - POOL_STATE
    return (y_p.reshape(bsz, seq, D_MODEL),
            y_s.reshape(dbs, dseq, D_MODEL),
            kf_p.reshape(1, bsz, seq, N_HEADS, V_DIM),
            vf_p.reshape(1, bsz, seq, N_HEADS, V_DIM),
            tail_p[None, :, drop:, :],
            kf_s.reshape(1, dbs, dseq, N_HEADS, V_DIM),
            vf_s.reshape(1, dbs, dseq, N_HEADS, V_DIM),
            tail_s[None, :, drop:, :])
```

```python
import functools
import math

import jax
import jax.numpy as jnp
from jax import lax
from jax.experimental import pallas as pl
from jax.experimental.pallas import tpu as pltpu

F32 = jnp.float32
BF16 = jnp.bfloat16

D_MODEL = 1024
N_HEADS = 4
HEAD_DIM = 64
V_DIM = 2 * HEAD_DIM
D_QK = N_HEADS * 2 * HEAD_DIM
D_ATTN = N_HEADS * V_DIM
D_POOL = 512
POOL_WINDOWS = (2, 4, 8, 16)
POOL_GROUP = D_POOL // len(POOL_WINDOWS)
POOL_STATE = max(POOL_WINDOWS) - 1
POOL_HIST = POOL_STATE + 1
N_EXPERT_GROUPS = 4
EXPERTS_PER_GROUP = 8
N_EXPERTS = N_EXPERT_GROUPS * EXPERTS_PER_GROUP
D_EXPERT = 256
D_PLE = 256
PAGE_SIZE = 128
LN_EPS = 1e-5
DEPTH = 1
DEEPNORM_ALPHA = (2 * DEPTH) ** 0.25
LAM_INIT = 0.8 - 0.6 * math.exp(-0.3 * 0)
D_QKVU = D_POOL + 2 * D_QK + D_ATTN
ROUTER_COLS = 128
TOP_K_IN_GROUP = 2
MOE_TILE = 512
MOE_GRANULE = 16
MOE_REGION = -(-(TOP_K_IN_GROUP * MOE_TILE + N_EXPERTS * (MOE_GRANULE - 1)) // 128) * 128
REGION_GRANULES = MOE_REGION // MOE_GRANULE
MOE_SLOTS = 32
LOG2E = 1.4426950408889634
DENOM_ROWS = 16
SAMPLES_PER_STEP = 2
SCORES_AHEAD = 2
COMBINE_PARTS = 2
V7X_VMEM_LIMIT = 56 * 1024 * 1024
NEG_INF = float("-inf")


def _sigmoid(x):
    return 1.0 / (1.0 + jnp.exp(-x))


def _layer_norm(x, g, b):
    mu = jnp.mean(x, axis=-1, keepdims=True)
    xc = x - mu
    var = jnp.mean(xc * xc, axis=-1, keepdims=True)
    return xc * lax.rsqrt(var + LN_EPS) * g + b


def _lam_value(lq1_ref, lk1_ref, lq2_ref, lk2_ref):
    a = jnp.sum(lq1_ref[...] * lk1_ref[...], axis=1, keepdims=True)
    b = jnp.sum(lq2_ref[...] * lk2_ref[...], axis=1, keepdims=True)
    return jnp.exp(a) - jnp.exp(b) + LAM_INIT


def _window_sums(e):
    outs = []
    for gi, w in enumerate(POOL_WINDOWS):
        s = e[:, gi * POOL_GROUP:(gi + 1) * POOL_GROUP]
        step = 1
        while step < w:
            s = s + pltpu.roll(s, step, axis=0)
            step *= 2
        outs.append(s)
    return outs


def _pool_branch(wins, u, inv_cnts, wg_ref, sp_ref):
    ds = []
    for gi in range(len(POOL_WINDOWS)):
        ds.append(wins[gi] * inv_cnts[gi] - u[:, gi * POOL_GROUP:(gi + 1) * POOL_GROUP])
    d = jnp.concatenate(ds, axis=1).astype(BF16)
    return jnp.dot(d, wg_ref[...], preferred_element_type=F32) * sp_ref[...]


def _store_head_major(ref, x, n_rows):
    for hh in range(N_HEADS):
        ref[pl.ds(hh, n_rows, stride=N_HEADS), :] = x[:, hh * V_DIM:(hh + 1) * V_DIM]


def _proj_prompt_kernel(x_ref, w_ref, b_ref, wg_ref, sp_ref,
                        qt_ref, kf_ref, vf_ref, kb_ref, vt_ref, yp_ref, tail_ref,
                        ext_ref, *, tm, tiles_per_seq):
    t_in_seq = pl.program_id(0) % tiles_per_seq
    z = jnp.dot(x_ref[...].astype(BF16), w_ref[...], preferred_element_type=F32) + b_ref[...]
    u = z[:, 0:D_POOL]
    q = z[:, D_POOL:D_POOL + D_QK]
    k = z[:, D_POOL + D_QK:D_POOL + 2 * D_QK]
    v = z[:, D_POOL + 2 * D_QK:D_QKVU]
    qt_ref[...] = (q * (HEAD_DIM ** -0.5 * LOG2E)).T.astype(BF16)
    vt_ref[...] = v.T.astype(BF16)
    kb_ref[...] = k.astype(BF16)
    _store_head_major(kf_ref, k, tm)
    _store_head_major(vf_ref, v, tm)

    @pl.when(t_in_seq == 0)
    def _():
        ext_ref[0:POOL_HIST, :] = jnp.zeros((POOL_HIST, D_POOL), F32)

    ext_ref[POOL_HIST:POOL_HIST + tm, :] = u
    wins = [s[POOL_HIST:, :] for s in _window_sums(ext_ref[...])]
    pos = t_in_seq * tm + lax.broadcasted_iota(jnp.int32, (tm, 1), 0)
    inv_cnts = [1.0 / jnp.minimum(pos + 1, w).astype(F32) for w in POOL_WINDOWS]
    yp_ref[...] = _pool_branch(wins, u, inv_cnts, wg_ref, sp_ref).astype(BF16)
    tail = u[tm - POOL_HIST:, :]
    ext_ref[0:POOL_HIST, :] = tail
    tail_ref[...] = tail


def _proj_sample_kernel(x_ref, w_ref, b_ref, wg_ref, sp_ref, st_ref,
                        q_ref, kf_ref, vf_ref, yp_ref, tail_ref,
                        ext_ref, *, n_seq, t_new, n_past):
    z = jnp.dot(x_ref[...].astype(BF16), w_ref[...], preferred_element_type=F32) + b_ref[...]
    u = z[:, 0:D_POOL]
    q_ref[...] = z[:, D_POOL:D_POOL + D_QK] * (HEAD_DIM ** -0.5)
    _store_head_major(kf_ref, z[:, D_POOL + D_QK:D_POOL + 2 * D_QK], n_seq * t_new)
    _store_head_major(vf_ref, z[:, D_POOL + 2 * D_QK:D_QKVU], n_seq * t_new)
    rows = POOL_HIST + t_new
    ext_ref[:, 0:POOL_HIST, :] = st_ref[...]
    ext_ref[:, POOL_HIST:rows, :] = u.reshape(n_seq, t_new, D_POOL)
    e3 = ext_ref[...]
    wins = [s.reshape(n_seq, rows, POOL_GROUP)[:, POOL_HIST:, :].reshape(n_seq * t_new, POOL_GROUP)
            for s in _window_sums(e3.reshape(n_seq * rows, D_POOL))]
    inv_cnts = [1.0 / float(min(n_past + 1, w)) for w in POOL_WINDOWS]
    yp_ref[...] = _pool_branch(wins, u, inv_cnts, wg_ref, sp_ref).astype(BF16)
    tail_ref[...] = e3[:, rows - POOL_HIST:, :]


def _attn_prompt_pair(slope, lam, qt_refs, k_ref, vt_ref, boff_ref, bdiag_ref, g_ref, o_ref,
                      q2t_ref, m_ref, acc_ref, *, tq, nq):
    pair = pl.program_id(2)
    tiles = (nq - 1 - pair, pair)
    half = tq // 2
    all_lanes = slice(0, 2 * tq)
    late_lanes = (slice(half, tq), slice(tq + half, 2 * tq))
    for slot, qt_ref in enumerate(qt_refs):
        qt = qt_ref[...]
        row = lax.broadcasted_iota(jnp.int32, qt.shape, 0)
        zero = jnp.zeros_like(qt)
        q2t_ref[slot] = jnp.concatenate([jnp.where(row < HEAD_DIM, qt, zero), jnp.where(row >= HEAD_DIM, qt, zero)],
                                        axis=1)

    def both_maps(bias):
        return jnp.concatenate([bias, bias], axis=1)

    def scores(slot, k_start, n_keys, lanes, bias):
        kblk = k_ref[pl.ds(k_start, n_keys), :]
        return jnp.dot(kblk, q2t_ref[slot, :, lanes], preferred_element_type=F32) + bias

    def absorb(t, slot, k_start, n_keys, lanes, shift, first):
        vt_aug = jnp.concatenate([vt_ref[:, pl.ds(k_start, n_keys)], jnp.ones((DENOM_ROWS, n_keys), BF16)], axis=0)
        m_new = jnp.max(t, axis=0, keepdims=True) + shift
        if not first:
            m_prev = m_ref[slot, :, lanes]
            m_new = jnp.maximum(m_prev, m_new)
        pv = jnp.dot(vt_aug, jnp.exp2(t - (m_new - shift)).astype(BF16), preferred_element_type=F32)
        acc_ref[slot, :, lanes] = pv if first else jnp.exp2(m_prev - m_new) * acc_ref[slot, :, lanes] + pv
        m_ref[slot, :, lanes] = m_new

    zero_shift = jnp.float32(0.0)
    tasks = []
    for slot in range(2):
        k_diag = pl.multiple_of(tiles[slot] * tq, tq)
        k_diag2 = pl.multiple_of(k_diag + half, half)
        tasks.append((slot, k_diag, half, all_lanes, lambda: both_maps(bdiag_ref[0:half, :]), zero_shift, True))
        for lanes in late_lanes:
            tasks.append((slot, k_diag2, half, lanes, lambda: bdiag_ref[half:tq, half:tq], zero_shift, False))
    for j in range(nq - 1):
        in_second = j >= tiles[0]
        kb = jnp.where(in_second, j - tiles[0], j)
        q_tile = jnp.where(in_second, tiles[1], tiles[0])
        shift = -(slope * LOG2E) * ((q_tile - kb) * tq).astype(F32)
        tasks.append((in_second.astype(jnp.int32), pl.multiple_of(kb * tq, tq), tq, all_lanes,
                      lambda: both_maps(boff_ref[...]), shift, False))
    queue = []
    for slot, k_start, n_keys, lanes, bias_fn, shift, first in tasks:
        queue.append((scores(slot, k_start, n_keys, lanes, bias_fn()), slot, k_start, n_keys, lanes, shift, first))
        if len(queue) > SCORES_AHEAD:
            absorb(*queue.pop(0))
    while queue:
        absorb(*queue.pop(0))

    for slot in range(2):
        acc = acc_ref[slot]
        on = acc[0:V_DIM, :] / acc[V_DIM:V_DIM + 1, :]
        o = (on[:, 0:tq] - lam * on[:, tq:]).T
        o = o * lax.rsqrt(jnp.mean(o * o, axis=1, keepdims=True) + LN_EPS) * g_ref[...] * (1.0 - LAM_INIT)
        o_ref[slot * tq:(slot + 1) * tq, :] = o.astype(BF16)


def _attn_sample_body(lam, sj, q_ref, kn_ref, vn_ref, bias_ref, g_ref, k_pages, v_pages, *, t_new):
    n_pages = len(k_pages)
    n_past = n_pages * PAGE_SIZE
    rows = slice(sj * t_new, (sj + 1) * t_new)
    q = q_ref[rows, :]
    lane = lax.broadcasted_iota(jnp.int32, (t_new, V_DIM), 1)
    pad = jnp.zeros((PAGE_SIZE - t_new, V_DIM), BF16)
    nt_dims = (((1,), (1,)), ((), ()))

    def head_rows(pages, new_ref, hh):
        past = jnp.concatenate([pg[pl.ds(hh, PAGE_SIZE, stride=N_HEADS), :] for pg in pages], axis=0)
        new = new_ref[pl.ds(sj * t_new * N_HEADS + hh, t_new, stride=N_HEADS), :]
        new = jnp.concatenate([new.astype(BF16), pad], axis=0)
        return past.astype(BF16), new

    s_rows = []
    for hh in range(N_HEADS):
        qh = q[:, hh * V_DIM:(hh + 1) * V_DIM]
        qh2 = jnp.concatenate([jnp.where(lane < HEAD_DIM, qh, 0.0), jnp.where(lane >= HEAD_DIM, qh, 0.0)],
                              axis=0).astype(BF16)
        k_past, k_new = head_rows(k_pages, kn_ref, hh)
        s_rows.append(jnp.concatenate(
            [lax.dot_general(qh2, k_past, nt_dims, preferred_element_type=F32),
             lax.dot_general(qh2, k_new, nt_dims, preferred_element_type=F32)], axis=1))
    s = jnp.concatenate(s_rows, axis=0) + bias_ref[...]
    m = jnp.max(s, axis=1, keepdims=True)
    p = jnp.exp(s - m)
    p = p / jnp.sum(p, axis=1, keepdims=True)
    outs = []
    for h0 in range(0, N_HEADS, 2):
        a2 = jnp.concatenate(
            [p[hh * 2 * t_new:hh * 2 * t_new + t_new, :] - lam * p[hh * 2 * t_new + t_new:(hh + 1) * 2 * t_new, :]
             for hh in (h0, h0 + 1)], axis=0).astype(BF16)
        (vp0, vn0), (vp1, vn1) = head_rows(v_pages, vn_ref, h0), head_rows(v_pages, vn_ref, h0 + 1)
        o2 = (jnp.dot(a2[:, 0:n_past], jnp.concatenate([vp0, vp1], axis=1), preferred_element_type=F32)
              + jnp.dot(a2[:, n_past:], jnp.concatenate([vn0, vn1], axis=1), preferred_element_type=F32))
        for k in range(2):
            oh = o2[k * t_new:(k + 1) * t_new, k * V_DIM:(k + 1) * V_DIM]
            oh = oh * lax.rsqrt(jnp.mean(oh * oh, axis=1, keepdims=True) + LN_EPS) * g_ref[...] * (1.0 - LAM_INIT)
            outs.append(oh)
    return jnp.concatenate(outs, axis=1)


def _attn_kernel(pt_ref, slopes_ref, lq1_ref, lk1_ref, lq2_ref, lk2_ref,
                 qta_ref, qtb_ref, k_ref, vt_ref, boff_ref, bdiag_ref, g_ref, bias_s_ref, *rest,
                 n_pages, t_new, tq, nq):
    qs_ref, kn_ref, vn_ref = rest[:3]
    pages = rest[3:3 + SAMPLES_PER_STEP * 2 * n_pages]
    op_ref, os_ref, q2t_ref, m_ref, acc_ref = rest[3 + SAMPLES_PER_STEP * 2 * n_pages:]
    lam = _lam_value(lq1_ref, lk1_ref, lq2_ref, lk2_ref)
    def sample(sj):
        pg = pages[sj * 2 * n_pages:(sj + 1) * 2 * n_pages]
        return _attn_sample_body(lam, sj, qs_ref, kn_ref, vn_ref, bias_s_ref, g_ref, pg[:n_pages], pg[n_pages:],
                                 t_new=t_new)

    outs = [sample(sj) for sj in range(SAMPLES_PER_STEP // 2)]
    _attn_prompt_pair(slopes_ref[pl.program_id(1)], lam, (qta_ref, qtb_ref), k_ref, vt_ref, boff_ref, bdiag_ref,
                      g_ref, op_ref, q2t_ref, m_ref, acc_ref, tq=tq, nq=nq)
    outs += [sample(sj) for sj in range(SAMPLES_PER_STEP // 2, SAMPLES_PER_STEP)]
    os_ref[...] = jnp.concatenate(outs, axis=0).astype(BF16)


def _merge_kernel(xp_ref, ypp_ref, yap_ref, xs_in_ref, yps_ref, yas_ref,
                  wgate_ref, bgate_ref, wpb_ref, wab_ref, wo_ref,
                  g1_ref, b1_ref, wrt_hi_ref, wrt_lo_ref, brt_ref, upper_ref, lower_ref,
                  x1f_ref, xs_ref, route_ref, cnt_ref, *, tm, prompt_tiles):
    is_sample = pl.program_id(0) >= prompt_tiles
    x = jnp.where(is_sample, xs_in_ref[...], xp_ref[...])
    y_pool = jnp.where(is_sample, yps_ref[...], ypp_ref[...])
    y_attn = jnp.where(is_sample, yas_ref[...], yap_ref[...])
    zg = jnp.dot(x.astype(BF16), wgate_ref[...], preferred_element_type=F32) + bgate_ref[...]
    gp = _sigmoid(zg[:, 0:D_MODEL])
    ga = _sigmoid(zg[:, D_MODEL:])
    merged = (gp * jnp.dot(y_pool, wpb_ref[...], preferred_element_type=F32)
              + ga * jnp.dot(y_attn, wab_ref[...], preferred_element_type=F32))
    x1 = _layer_norm(DEEPNORM_ALPHA * x + jnp.dot(merged.astype(BF16), wo_ref[...], preferred_element_type=F32),
                     g1_ref[...], b1_ref[...])
    x1f_ref[...] = x1
    x1_hi = x1.astype(BF16)
    x1_lo = (x1 - x1_hi.astype(F32)).astype(BF16)
    nt_dims = (((1,), (1,)), ((), ()))
    lg = (lax.dot_general(wrt_hi_ref[...], x1_hi, nt_dims, preferred_element_type=F32)
          + lax.dot_general(wrt_hi_ref[...], x1_lo, nt_dims, preferred_element_type=F32)
          + lax.dot_general(wrt_lo_ref[...], x1_hi, nt_dims, preferred_element_type=F32)) + brt_ref[...]
    row = lax.broadcasted_iota(jnp.int32, lg.shape, 0)
    big = jnp.int32(4 * ROUTER_COLS)
    lgg = jnp.where(row < N_EXPERT_GROUPS, lg, NEG_INF)
    gmax = jnp.max(lgg, axis=0, keepdims=True)
    g_w = 1.0 / jnp.sum(jnp.exp(lgg - gmax), axis=0, keepdims=True)
    g_idx = jnp.min(jnp.where(lgg == gmax, row, big), axis=0, keepdims=True)
    lo_row = N_EXPERT_GROUPS + g_idx * EXPERTS_PER_GROUP
    le = jnp.where((row >= lo_row) & (row < lo_row + EXPERTS_PER_GROUP), lg, NEG_INF)
    v1 = jnp.max(le, axis=0, keepdims=True)
    i1 = jnp.min(jnp.where(le == v1, row, big), axis=0, keepdims=True)
    le2 = jnp.where(row == i1, NEG_INF, le)
    v2 = jnp.max(le2, axis=0, keepdims=True)
    i2 = jnp.min(jnp.where(le2 == v2, row, big), axis=0, keepdims=True)
    e21 = jnp.exp(v2 - v1)
    c1 = g_w / (1.0 + e21)
    c2 = g_w * e21 / (1.0 + e21)
    sel1 = row == i1
    sel2 = row == i2
    member = jnp.where(sel1 | sel2, 1.0, 0.0)
    before = jnp.dot(member.astype(BF16), upper_ref[...], preferred_element_type=F32)
    cnt = jnp.sum(member, axis=1, keepdims=True)
    cnt_pad = jnp.ceil(cnt * (1.0 / MOE_GRANULE)) * MOE_GRANULE
    cnt_b = jnp.broadcast_to(cnt_pad, (ROUTER_COLS, ROUTER_COLS))
    seg_start = jnp.dot(lower_ref[...], cnt_b.astype(BF16), preferred_element_type=F32)
    cnt_ref[...] = cnt_b
    slot = seg_start[:, 0:1] + before
    pos1 = jnp.sum(jnp.where(sel1, slot, 0.0), axis=0, keepdims=True)
    pos2 = jnp.sum(jnp.where(sel2, slot, 0.0), axis=0, keepdims=True)
    srow = lax.broadcasted_iota(jnp.int32, (xs_ref.shape[0], tm), 0).astype(F32)
    perm = jnp.where((srow == pos1) | (srow == pos2), 1.0, 0.0).astype(BF16)
    xs_ref[...] = jnp.dot(perm, x1_hi, preferred_element_type=F32).astype(BF16)
    info = (jnp.where(row == 0, pos1, 0.0) + jnp.where(row == 1, pos2, 0.0)
            + jnp.where(row == 2, c1, 0.0) + jnp.where(row == 3, c2, 0.0))
    route_ref[...] = info.T


def _experts_kernel(gin_ref, sexp_ref, xs_hbm, wg_ref, wu_ref, wd_ref, y_ref,
                    xbuf, sems, wg_b, wu_b, wd_b, *, n_slots, n_steps):
    t = pl.program_id(0)
    expert = sexp_ref[t]
    is_expert = expert < N_EXPERTS
    cur = t % 2

    def gather(step, slot):
        for k in range(n_slots):
            src = pl.multiple_of(gin_ref[step * n_slots + k] * MOE_GRANULE, MOE_GRANULE)
            pltpu.make_async_copy(xs_hbm.at[pl.ds(src, MOE_GRANULE), :],
                                  xbuf.at[slot, pl.ds(k * MOE_GRANULE, MOE_GRANULE), :], sems.at[slot]).start()

    @pl.when((t == 0) & is_expert)
    def _():
        gather(0, 0)

    nxt = jnp.minimum(t + 1, n_steps - 1)

    @pl.when((t + 1 < n_steps) & (sexp_ref[nxt] < N_EXPERTS))
    def _():
        gather(nxt, 1 - cur)

    @pl.when(is_expert & ((t == 0) | (expert != sexp_ref[jnp.maximum(t - 1, 0)])))
    def _():
        wg_b[...] = wg_ref[...].astype(BF16)
        wu_b[...] = wu_ref[...].astype(BF16)
        wd_b[...] = wd_ref[...].astype(BF16)

    @pl.when(is_expert)
    def _():
        pltpu.make_async_copy(xbuf.at[cur], xbuf.at[cur], sems.at[cur]).wait()
        rows = (n_slots // 2) * MOE_GRANULE

        def gate_up(row0):
            xg = xbuf[cur, row0:row0 + rows, :]
            return (jnp.dot(xg, wg_b[...], preferred_element_type=F32),
                    jnp.dot(xg, wu_b[...], preferred_element_type=F32))

        def act_down(gate, up):
            hact = (gate * _sigmoid(gate) * up).astype(BF16)
            return jnp.dot(hact, wd_b[...], preferred_element_type=F32).astype(BF16)

        gate_a, up_a = gate_up(0)
        gate_b, up_b = gate_up(rows)
        y_ref[0:rows, :] = act_down(gate_a, up_a)
        y_ref[rows:, :] = act_down(gate_b, up_b)

    @pl.when(jnp.logical_not(is_expert))
    def _():
        y_ref[...] = jnp.zeros(y_ref.shape, BF16)


def _final_kernel(loc_ref, x1f_ref, route_ref, *rest, n_granules):
    g_refs = rest[:n_granules]
    p_ref, g2_ref, b2_ref, wpg_ref, bpg_ref, wple_ref, g3_ref, b3_ref, y_ref = rest[n_granules:]
    ys = jnp.concatenate([r[...] for r in g_refs], axis=0)
    tm = route_ref.shape[0]
    part = tm // COMBINE_PARTS
    parts = [slice(k * part, (k + 1) * part) for k in range(COMBINE_PARTS)]
    scol = lax.broadcasted_iota(jnp.int32, (part, ys.shape[0]), 1).astype(F32)

    def combine(rows):
        route = route_ref[rows, :]
        comb = (jnp.where(scol == route[:, 0:1], route[:, 2:3], 0.0)
                + jnp.where(scol == route[:, 1:2], route[:, 3:4], 0.0)).astype(BF16)
        return jnp.dot(comb, ys, preferred_element_type=F32)

    def ple(rows, moe):
        x2 = _layer_norm(DEEPNORM_ALPHA * x1f_ref[rows, :] + moe, g2_ref[...], b2_ref[...])
        z = jnp.dot(x2.astype(BF16), wpg_ref[...], preferred_element_type=F32) + bpg_ref[...]
        pe = jnp.dot(p_ref[rows, :].astype(BF16), wple_ref[...], preferred_element_type=F32)
        return x2, z, pe

    def finish(rows, x2, z, pe):
        y_ref[rows, :] = _layer_norm(DEEPNORM_ALPHA * x2 + _sigmoid(z) * pe, g3_ref[...], b3_ref[...])

    moe, mid = {}, {}
    for k in range(COMBINE_PARTS + 2):
        if k < COMBINE_PARTS:
            moe[k] = combine(parts[k])
        if 0 <= k - 1 < COMBINE_PARTS:
            mid[k - 1] = ple(parts[k - 1], moe.pop(k - 1))
        if 0 <= k - 2 < COMBINE_PARTS:
            finish(parts[k - 2], *mid.pop(k - 2))


def _full(shape):
    nd = len(shape)
    return pl.BlockSpec(shape, lambda *_: (0,) * nd)


def _cparams(sem):
    return pltpu.CompilerParams(dimension_semantics=sem, vmem_limit_bytes=V7X_VMEM_LIMIT)


def _proj_prompt(x2d, w_qkvu, b_qkvu, wg_bd, s_pool, n_seq, seq_len, tm=1024):
    n = x2d.shape[0]
    tps = seq_len // tm
    row = lambda i: (i, 0)
    col = lambda i: (0, i)
    outs = pl.pallas_call(
        functools.partial(_proj_prompt_kernel, tm=tm, tiles_per_seq=tps),
        grid=(n // tm,),
        in_specs=[pl.BlockSpec((tm, D_MODEL), row), _full(w_qkvu.shape), _full(b_qkvu.shape),
                  _full(wg_bd.shape), _full(s_pool.shape)],
        out_specs=[pl.BlockSpec((D_QK, tm), col), pl.BlockSpec((tm * N_HEADS, V_DIM), row),
                   pl.BlockSpec((tm * N_HEADS, V_DIM), row), pl.BlockSpec((tm, D_QK), row),
                   pl.BlockSpec((D_ATTN, tm), col), pl.BlockSpec((tm, D_POOL), row),
                   pl.BlockSpec((None, POOL_HIST, D_POOL), lambda i: (i // tps, 0, 0))],
        out_shape=[jax.ShapeDtypeStruct((D_QK, n), BF16), jax.ShapeDtypeStruct((n * N_HEADS, V_DIM), F32),
                   jax.ShapeDtypeStruct((n * N_HEADS, V_DIM), F32), jax.ShapeDtypeStruct((n, D_QK), BF16),
                   jax.ShapeDtypeStruct((D_ATTN, n), BF16), jax.ShapeDtypeStruct((n, D_POOL), BF16),
                   jax.ShapeDtypeStruct((n_seq, POOL_HIST, D_POOL), F32)],
        scratch_shapes=[pltpu.VMEM((POOL_HIST + tm, D_POOL), F32)],
        compiler_params=_cparams(("arbitrary",)),
        name="proj_pool_prompt",
    )(x2d, w_qkvu, b_qkvu, wg_bd, s_pool)
    return outs


def _proj_sample(x2d, w_qkvu, b_qkvu, wg_bd, s_pool, state_pad, n_seq, t_new, n_past):
    n = x2d.shape[0]
    rows = POOL_HIST + t_new
    return pl.pallas_call(
        functools.partial(_proj_sample_kernel, n_seq=n_seq, t_new=t_new, n_past=n_past),
        grid=(1,),
        in_specs=[_full(x2d.shape), _full(w_qkvu.shape), _full(b_qkvu.shape), _full(wg_bd.shape),
                  _full(s_pool.shape), _full(state_pad.shape)],
        out_specs=[_full((n, D_QK)), _full((n * N_HEADS, V_DIM)), _full((n * N_HEADS, V_DIM)),
                   _full((n, D_POOL)), _full((n_seq, POOL_HIST, D_POOL))],
        out_shape=[jax.ShapeDtypeStruct((n, D_QK), F32), jax.ShapeDtypeStruct((n * N_HEADS, V_DIM), F32),
                   jax.ShapeDtypeStruct((n * N_HEADS, V_DIM), F32), jax.ShapeDtypeStruct((n, D_POOL), BF16),
                   jax.ShapeDtypeStruct((n_seq, POOL_HIST, D_POOL), F32)],
        scratch_shapes=[pltpu.VMEM((n_seq, rows, D_POOL), F32)],
        compiler_params=_cparams(("arbitrary",)),
        name="proj_pool_sample",
    )(x2d, w_qkvu, b_qkvu, wg_bd, s_pool, state_pad)


def _alibi_slopes():
    return 2.0 ** (-8.0 * jnp.arange(1, N_HEADS + 1, dtype=F32) / N_HEADS)


def _attention(qt, kb, vt, q_s, k_new, v_new, cache_k, cache_v, page_table, lam_vecs, g_sub,
               n_seq, seq_len, t_new, tq=512):
    n = kb.shape[0]
    nq = seq_len // tq
    n_pairs = nq // 2
    n_smp, n_pages = page_table.shape
    n_past = n_pages * PAGE_SIZE
    assert nq % 2 == 0 and n_smp == SAMPLES_PER_STEP * n_seq * N_HEADS * n_pairs, (n_smp, n_seq, nq)
    slopes = _alibi_slopes()
    rel = (jnp.arange(tq, dtype=jnp.int32)[None, :] - jnp.arange(tq, dtype=jnp.int32)[:, None])
    b_off = -(slopes * LOG2E)[:, None, None] * rel.astype(F32)[None]
    b_diag = jnp.where(rel[None] >= 0, b_off, NEG_INF)
    r = jnp.arange(2 * N_HEADS * t_new, dtype=jnp.int32)
    r_head, r_q = r // (2 * t_new), r % t_new
    row_slope = slopes[r_head][:, None]
    kpos = jnp.arange(n_past, dtype=jnp.int32)[None, :]
    b_past = -row_slope * (n_past + r_q[:, None] - kpos).astype(F32)
    j = jnp.arange(PAGE_SIZE, dtype=jnp.int32)[None, :]
    dist_new = r_q[:, None] - j
    b_new = jnp.where((dist_new >= 0) & (j < t_new), -row_slope * dist_new.astype(F32), NEG_INF)
    bias_s = jnp.concatenate([b_past, b_new], axis=1)
    page_rows = PAGE_SIZE * N_HEADS
    ck = cache_k.reshape(-1, V_DIM)
    cv = cache_v.reshape(-1, V_DIM)
    pt = page_table.reshape(-1).astype(jnp.int32)
    step = lambda b, h, i: (b * N_HEADS + h) * n_pairs + i
    const2 = lambda b, h, i, pt_ref: (0, 0)
    vec = pl.BlockSpec((1, HEAD_DIM), const2)
    smp_rows = SAMPLES_PER_STEP * t_new
    tok = pl.BlockSpec((smp_rows, D_QK), lambda b, h, i, pt_ref: (step(b, h, i), 0))
    tok_hm = pl.BlockSpec((smp_rows * N_HEADS, V_DIM), lambda b, h, i, pt_ref: (step(b, h, i), 0))
    table = pl.BlockSpec((None, tq, tq), lambda b, h, i, pt_ref: (h, 0, 0))

    def page_spec(sj, pi):
        return pl.BlockSpec((page_rows, V_DIM), lambda b, h, i, pt_ref: (
            pt_ref[(step(b, h, i) * SAMPLES_PER_STEP + sj) * n_pages + pi], 0))

    pages = []
    for sj in range(SAMPLES_PER_STEP):
        pages += [(page_spec(sj, pi), ck) for pi in range(n_pages)] + [(page_spec(sj, pi), cv) for pi in range(n_pages)]
    grid_spec = pltpu.PrefetchScalarGridSpec(
        num_scalar_prefetch=1,
        grid=(n_seq, N_HEADS, n_pairs),
        in_specs=[pl.BlockSpec(memory_space=pltpu.SMEM), vec, vec, vec, vec,
                  pl.BlockSpec((V_DIM, tq), lambda b, h, i, pt_ref: (h, b * nq + nq - 1 - i)),
                  pl.BlockSpec((V_DIM, tq), lambda b, h, i, pt_ref: (h, b * nq + i)),
                  pl.BlockSpec((seq_len, V_DIM), lambda b, h, i, pt_ref: (b, h)),
                  pl.BlockSpec((V_DIM, seq_len), lambda b, h, i, pt_ref: (h, b)),
                  table, table, pl.BlockSpec((1, V_DIM), const2), pl.BlockSpec(bias_s.shape, const2),
                  tok, tok_hm, tok_hm] + [spec for spec, _ in pages],
        out_specs=[pl.BlockSpec((2 * tq, V_DIM), lambda b, h, i, pt_ref: (b * n_pairs + i, h)), tok],
        scratch_shapes=[pltpu.VMEM((2, V_DIM, 2 * tq), BF16), pltpu.VMEM((2, 1, 2 * tq), F32),
                        pltpu.VMEM((2, V_DIM + DENOM_ROWS, 2 * tq), F32)],
    )
    return pl.pallas_call(
        functools.partial(_attn_kernel, n_pages=n_pages, t_new=t_new, tq=tq, nq=nq),
        grid_spec=grid_spec,
        out_shape=[jax.ShapeDtypeStruct((n, D_ATTN), BF16), jax.ShapeDtypeStruct((n_smp * t_new, D_ATTN), BF16)],
        compiler_params=_cparams(("arbitrary", "arbitrary", "arbitrary")),
        name="attention",
    )(pt, slopes, *lam_vecs, qt, qt, kb, vt, b_off, b_diag, g_sub, bias_s, q_s, k_new, v_new,
      *[arr for _, arr in pages])


def _pair_major_tile(i, nq):
    b, qi = i // nq, i % nq
    return b * nq + jnp.where(qi >= nq // 2, 2 * (nq - 1 - qi), 2 * qi + 1)


def _merge(prompt, sample, wts, attn_tile):
    tm = MOE_TILE
    tiles_p = prompt[0].shape[0] // tm
    n_tiles = tiles_p + sample[0].shape[0] // tm
    p_row = lambda i: (jnp.minimum(i, tiles_p - 1), 0)
    s_row = lambda i: (jnp.maximum(i - tiles_p, 0), 0)
    row = lambda i: (i, 0)
    consts = [wts[k] for k in ("w_gate", "b_gate", "w_pool_br", "w_attn_br", "w_o", "ln1_g", "ln1_b",
                               "wrt_hi", "wrt_lo", "b_rt", "upper", "lower")]
    return pl.pallas_call(
        functools.partial(_merge_kernel, tm=tm, prompt_tiles=tiles_p),
        grid=(n_tiles,),
        in_specs=[pl.BlockSpec((tm, D_MODEL), p_row), pl.BlockSpec((tm, D_POOL), p_row),
                  pl.BlockSpec((tm, D_ATTN), lambda i: (attn_tile(jnp.minimum(i, tiles_p - 1)), 0)),
                  pl.BlockSpec((tm, D_MODEL), s_row), pl.BlockSpec((tm, D_POOL), s_row),
                  pl.BlockSpec((tm, D_ATTN), s_row)] + [_full(c.shape) for c in consts],
        out_specs=[pl.BlockSpec((tm, D_MODEL), row), pl.BlockSpec((MOE_REGION, D_MODEL), row),
                   pl.BlockSpec((tm, ROUTER_COLS), row), pl.BlockSpec((ROUTER_COLS, ROUTER_COLS), row)],
        out_shape=[jax.ShapeDtypeStruct((n_tiles * tm, D_MODEL), F32),
                   jax.ShapeDtypeStruct((n_tiles * MOE_REGION, D_MODEL), BF16),
                   jax.ShapeDtypeStruct((n_tiles * tm, ROUTER_COLS), F32),
                   jax.ShapeDtypeStruct((n_tiles * ROUTER_COLS, ROUTER_COLS), F32)],
        compiler_params=_cparams(("arbitrary",)),
        name="merge_ln1_route_dispatch",
    )(*prompt, *sample, *consts)


def _granule_schedule(cnt_pad, n_steps):
    n_tiles = cnt_pad.shape[0]
    gc = cnt_pad // MOE_GRANULE
    seg = jnp.cumsum(gc, axis=1) - gc
    tot = jnp.sum(gc, axis=0)
    ahead = jnp.cumsum(gc, axis=0) - gc
    slots_e = ((tot + MOE_SLOTS - 1) // MOE_SLOTS) * MOE_SLOTS
    first_slot = jnp.cumsum(slots_e) - slots_e
    p0 = (jnp.arange(n_steps, dtype=jnp.int32) * MOE_SLOTS)[None, :]
    in_e = ((p0 >= first_slot[:, None]) & (p0 < (first_slot + slots_e)[:, None])).astype(jnp.int32)
    q0 = p0 - jnp.sum(in_e * first_slot[:, None], axis=0, keepdims=True)
    tot_s = jnp.sum(in_e * tot[:, None], axis=0, keepdims=True)
    e_id = jnp.sum(in_e * jnp.arange(N_EXPERTS, dtype=jnp.int32)[:, None], axis=0)
    step_expert = jnp.where(q0[0] < tot_s[0], e_id, N_EXPERTS).astype(jnp.int32)
    pick = lambda tab: jnp.sum(tab[:, :, None] * in_e[None, :, :], axis=1)[:, :, None]
    ahead_s, gc_s, seg_s = pick(ahead), pick(gc), pick(seg)
    q = q0[0][None, :, None] + jnp.arange(MOE_SLOTS, dtype=jnp.int32)[None, None, :]
    in_t = ((q >= ahead_s) & (q < ahead_s + gc_s)).astype(jnp.int32)
    tile_base = (jnp.arange(n_tiles, dtype=jnp.int32) * REGION_GRANULES)[:, None, None]
    gid = jnp.sum(in_t * (tile_base + seg_s + q - ahead_s), axis=0)
    gin = jnp.where(q[0] < tot_s[0][:, None], gid, 0).astype(jnp.int32).reshape(-1)
    s = jnp.arange(REGION_GRANULES, dtype=jnp.int32)[None, None, :]
    in_g = ((s >= seg[:, :, None]) & (s < (seg + gc)[:, :, None])).astype(jnp.int32)
    loc = jnp.sum(in_g * ((first_slot[None, :] + ahead - seg)[:, :, None] + s), axis=1)
    used = jnp.sum(in_g, axis=1) > 0
    loc = jnp.where(used, loc, loc[:, 0:1]).astype(jnp.int32)
    return gin, step_expert, loc.reshape(-1)


def _experts(xs, gin, step_expert, w_gate, w_up, w_down, n_steps):
    rows = MOE_SLOTS * MOE_GRANULE
    w_sel = lambda t, gin_ref, se_ref: (jnp.minimum(se_ref[t], N_EXPERTS - 1), 0, 0)
    grid_spec = pltpu.PrefetchScalarGridSpec(
        num_scalar_prefetch=2,
        grid=(n_steps,),
        in_specs=[pl.BlockSpec(memory_space=pl.ANY),
                  pl.BlockSpec((None, D_MODEL, D_EXPERT), w_sel), pl.BlockSpec((None, D_MODEL, D_EXPERT), w_sel),
                  pl.BlockSpec((None, D_EXPERT, D_MODEL), w_sel)],
        out_specs=pl.BlockSpec((rows, D_MODEL), lambda t, gin_ref, se_ref: (t, 0)),
        scratch_shapes=[pltpu.VMEM((2, rows, D_MODEL), BF16), pltpu.SemaphoreType.DMA((2,)),
                        pltpu.VMEM((D_MODEL, D_EXPERT), BF16), pltpu.VMEM((D_MODEL, D_EXPERT), BF16),
                        pltpu.VMEM((D_EXPERT, D_MODEL), BF16)],
    )
    return pl.pallas_call(
        functools.partial(_experts_kernel, n_slots=MOE_SLOTS, n_steps=n_steps),
        grid_spec=grid_spec,
        out_shape=jax.ShapeDtypeStruct((n_steps * rows, D_MODEL), BF16),
        compiler_params=_cparams(("arbitrary",)),
        name="expert_mlps",
    )(gin, step_expert, xs, w_gate, w_up, w_down)


def _final(x1f, route, ys, loc, p2d, wts, tile0):
    tm = MOE_TILE
    n = p2d.shape[0]
    consts = [wts[k] for k in ("ln2_g", "ln2_b", "w_ple_gate", "b_ple_gate", "w_ple", "ln3_g", "ln3_b")]

    def granule_spec(s):
        return pl.BlockSpec((MOE_GRANULE, D_MODEL),
                            lambda i, loc_ref: (loc_ref[(i + tile0) * REGION_GRANULES + s], 0))

    grid_spec = pltpu.PrefetchScalarGridSpec(
        num_scalar_prefetch=1,
        grid=(n // tm,),
        in_specs=[pl.BlockSpec((tm, D_MODEL), lambda i, loc_ref: (i + tile0, 0)),
                  pl.BlockSpec((tm, ROUTER_COLS), lambda i, loc_ref: (i + tile0, 0))]
                 + [granule_spec(s) for s in range(REGION_GRANULES)]
                 + [pl.BlockSpec((tm, D_PLE), lambda i, loc_ref: (i, 0))]
                 + [pl.BlockSpec(c.shape, lambda i, loc_ref: (0, 0)) for c in consts],
        out_specs=pl.BlockSpec((tm, D_MODEL), lambda i, loc_ref: (i, 0)),
    )
    return pl.pallas_call(
        functools.partial(_final_kernel, n_granules=REGION_GRANULES),
        grid_spec=grid_spec,
        out_shape=jax.ShapeDtypeStruct((n, D_MODEL), F32),
        compiler_params=_cparams(("arbitrary",)),
        name="combine_ln2_ple_ln3",
    )(loc, x1f, route, *([ys] * REGION_GRANULES), p2d, *consts)


def _prepare_weights(w_in, b_in, w_pool_grp, s_pool, w_pool_br, w_attn_br, w_o, ln1_g, ln1_b,
                     w_rg, b_rg, w_re, b_re, w_gate, w_up, w_down, ln2_g, ln2_b,
                     w_ple_gate, b_ple_gate, w_ple, ln3_g, ln3_b):
    i = 0
    row = lambda a: a[i].reshape(1, -1).astype(F32)
    wg_bd = jnp.zeros((D_POOL, D_POOL), F32)
    for gi in range(len(POOL_WINDOWS)):
        sl = slice(gi * POOL_GROUP, (gi + 1) * POOL_GROUP)
        wg_bd = wg_bd.at[sl, sl].set(w_pool_grp[i, gi])
    w_r = jnp.zeros((D_MODEL, ROUTER_COLS), F32)
    w_r = w_r.at[:, 0:N_EXPERT_GROUPS].set(w_rg[i])
    w_r = w_r.at[:, N_EXPERT_GROUPS:N_EXPERT_GROUPS + N_EXPERTS].set(w_re[i].reshape(D_MODEL, N_EXPERTS))
    b_r = jnp.zeros((1, ROUTER_COLS), F32)
    b_r = b_r.at[0, 0:N_EXPERT_GROUPS].set(b_rg[i])
    b_r = b_r.at[0, N_EXPERT_GROUPS:N_EXPERT_GROUPS + N_EXPERTS].set(b_re[i].reshape(-1))
    wrt_hi = w_r.T.astype(BF16)
    wrt_lo = (w_r.T - wrt_hi.astype(F32)).astype(BF16)
    tok = jnp.arange(MOE_TILE, dtype=jnp.int32)
    upper = (tok[:, None] < tok[None, :]).astype(BF16)
    rr = jnp.arange(ROUTER_COLS, dtype=jnp.int32)
    lower = (rr[None, :] < rr[:, None]).astype(BF16)
    return dict(
        w_gate_e=w_gate[i].reshape(N_EXPERTS, D_MODEL, D_EXPERT), w_up_e=w_up[i].reshape(N_EXPERTS, D_MODEL, D_EXPERT),
        w_down_e=w_down[i].reshape(N_EXPERTS, D_EXPERT, D_MODEL),
        wrt_hi=wrt_hi, wrt_lo=wrt_lo, b_rt=b_r.reshape(ROUTER_COLS, 1), upper=upper, lower=lower,
        w_qkvu=w_in[i, :, 0:D_QKVU].astype(BF16), b_qkvu=b_in[i, 0:D_QKVU].reshape(1, -1),
        w_gate=w_in[i, :, D_QKVU:].astype(BF16), b_gate=b_in[i, D_QKVU:].reshape(1, -1),
        wg_bd=wg_bd.astype(BF16), s_pool=row(s_pool),
        w_pool_br=w_pool_br[i].astype(BF16), w_attn_br=w_attn_br[i].astype(BF16), w_o=w_o[i].astype(BF16),
        ln1_g=row(ln1_g), ln1_b=row(ln1_b), ln2_g=row(ln2_g), ln2_b=row(ln2_b),
        w_ple_gate=w_ple_gate[i].astype(BF16), b_ple_gate=row(b_ple_gate), w_ple=w_ple[i].astype(BF16),
        ln3_g=row(ln3_g), ln3_b=row(ln3_b))


def kernel(x_prompt, x_sample, p_prompt, p_sample, cache_k, cache_v, state_pool, page_table, w_in, b_in, lam_q1, lam_k1, lam_q2, lam_k2, g_sub, w_pool_grp, s_pool, w_pool_br, w_attn_br, w_o, ln1_g, ln1_b, w_rg, b_rg, w_re, b_re, w_gate, w_up, w_down, ln2_g, ln2_b, w_ple_gate, b_ple_gate, w_ple, ln3_g, ln3_b):
    assert w_in.shape[0] == DEPTH == 1
    bsz, seq, _ = x_prompt.shape
    dbs, dseq, _ = x_sample.shape
    n_past = page_table.shape[1] * PAGE_SIZE
    wts = _prepare_weights(w_in, b_in, w_pool_grp, s_pool, w_pool_br, w_attn_br, w_o, ln1_g, ln1_b,
                           w_rg, b_rg, w_re, b_re, w_gate, w_up, w_down, ln2_g, ln2_b,
                           w_ple_gate, b_ple_gate, w_ple, ln3_g, ln3_b)
    lam_vecs = [a[0].reshape(1, HEAD_DIM).astype(F32) for a in (lam_q1, lam_k1, lam_q2, lam_k2)]
    g_row = g_sub[0].reshape(1, V_DIM).astype(F32)

    xp = x_prompt.reshape(bsz * seq, D_MODEL)
    qt_p, kf_p, vf_p, kb_p, vt_p, yp_p, tail_p = _proj_prompt(
        xp, wts["w_qkvu"], wts["b_qkvu"], wts["wg_bd"], wts["s_pool"], bsz, seq)

    xs = x_sample.reshape(dbs * dseq, D_MODEL)
    state_pad = jnp.pad(state_pool[0], ((0, 0), (POOL_HIST - POOL_STATE, 0), (0, 0)))
    q_s, kf_s, vf_s, yp_s, tail_s = _proj_sample(
        xs, wts["w_qkvu"], wts["b_qkvu"], wts["wg_bd"], wts["s_pool"], state_pad, dbs, dseq, n_past)

    ya_p, ya_s = _attention(qt_p, kb_p, vt_p, q_s, kf_s, vf_s, cache_k, cache_v, page_table, lam_vecs, g_row,
                            bsz, seq, dseq, tq=MOE_TILE)

    tiles_p = (bsz * seq) // MOE_TILE
    tiles_s = (dbs * dseq) // MOE_TILE
    n_tiles = tiles_p + tiles_s
    x1f, xsort, route, cnt = _merge((xp, yp_p, ya_p), (xs, yp_s, ya_s), wts,
                                    attn_tile=lambda i: _pair_major_tile(i, seq // MOE_TILE))
    cnt_pad = cnt[:, 0].reshape(n_tiles, ROUTER_COLS)[:, N_EXPERT_GROUPS:N_EXPERT_GROUPS + N_EXPERTS]
    n_steps = (n_tiles * REGION_GRANULES + N_EXPERTS * (MOE_SLOTS - 1) + MOE_SLOTS - 1) // MOE_SLOTS
    gin, step_expert, loc = _granule_schedule(cnt_pad.astype(jnp.int32), n_steps)
    ysort = _experts(xsort, gin, step_expert, wts["w_gate_e"], wts["w_up_e"], wts["w_down_e"], n_steps)
    y_p = _final(x1f, route, ysort, loc, p_prompt[0].reshape(bsz * seq, D_PLE), wts, 0)
    y_s = _final(x1f, route, ysort, loc, p_sample[0].reshape(dbs * dseq, D_PLE), wts, tiles_p)

    drop = POOL_HIST - POOL_STATE
    return (y_p.reshape(bsz, seq, D_MODEL),
            y_s.reshape(dbs, dseq, D_MODEL),
            kf_p.reshape(1, bsz, seq, N_HEADS, V_DIM),
            vf_p.reshape(1, bsz, seq, N_HEADS, V_DIM),
            tail_p[None, :, drop:, :],
            kf_s.reshape(1, dbs, dseq, N_HEADS, V_DIM),
            vf_s.reshape(1, dbs, dseq, N_HEADS, V_DIM),
            tail_s[None, :, drop:, :])
```
